```python
import math
import jax, jax.numpy as jnp
from jax import lax
import numpy as np

D_MODEL = 1024
BATCH = 8
SEQ = 2048
DEPTH = 1

MLA_HEADS = 8
MLA_NOPE_DIM = 64
MLA_ROPE_DIM = 32
MLA_V_DIM = 64
MLA_QK_DIM = MLA_NOPE_DIM + MLA_ROPE_DIM
MLA_Q_RANK = 384
MLA_KV_RANK = 256
DIFF_HEADS = 4
DIFF_HEAD_DIM = 64
DIFF_V_DIM = 2 * DIFF_HEAD_DIM
MIX_WIDTH = MLA_HEADS * MLA_V_DIM + DIFF_HEADS * DIFF_V_DIM
IN_SPLITS = (MLA_Q_RANK, MLA_KV_RANK, MLA_ROPE_DIM,
             DIFF_HEADS * 2 * DIFF_HEAD_DIM,
             DIFF_HEADS * 2 * DIFF_HEAD_DIM,
             DIFF_HEADS * DIFF_V_DIM)
IN_COLS = sum(IN_SPLITS)
D_FF = 2816
CONV_WIDTH = 3
ROPE_THETA = 10000.0
NORM_EPS = 1e-6
Q_BLOCK = 128

kernel_name = "hybrid_mla_diffattn_convglu"


def rms_norm(x, g):
    xf = x.astype(jnp.float32)
    y = xf * lax.rsqrt(jnp.mean(xf * xf, axis=-1, keepdims=True) + NORM_EPS)
    return (y * g.astype(jnp.float32)).astype(x.dtype)


def rope_tables(seq, dim):
    inv = 1.0 / (ROPE_THETA ** (jnp.arange(0, dim, 2, dtype=jnp.float32) / dim))
    ang = jnp.arange(seq, dtype=jnp.float32)[:, None] * inv[None, :]
    return jnp.cos(ang), jnp.sin(ang)


def apply_rope(x, cos, sin):
    xf = x.astype(jnp.float32)
    half = xf.shape[-1] // 2
    x1, x2 = xf[..., :half], xf[..., half:]
    out = jnp.concatenate([x1 * cos - x2 * sin, x2 * cos + x1 * sin], axis=-1)
    return out.astype(x.dtype)


def causal_block_probs(q_blk, k_pre, scale, q_start):
    s = jnp.einsum('bhqd,bhkd->bhqk', q_blk, k_pre).astype(jnp.float32) * scale
    q_pos = q_start + jnp.arange(q_blk.shape[2])
    k_pos = jnp.arange(k_pre.shape[2])
    mask = k_pos[None, :] <= q_pos[:, None]
    s = jnp.where(mask, s, -jnp.inf)
    return jax.nn.softmax(s, axis=-1)


def mla_attention(q, k, v, scale):
    seq = q.shape[2]
    outs = []
    for i in range(seq // Q_BLOCK):
        s0, e = i * Q_BLOCK, (i + 1) * Q_BLOCK
        p = causal_block_probs(q[:, :, s0:e], k[:, :, :e], scale, s0)
        outs.append(jnp.einsum('bhqk,bhkd->bhqd', p.astype(v.dtype), v[:, :, :e]))
    return jnp.concatenate(outs, axis=2)


def differential_attention(q1, q2, k1, k2, v, lam, scale):
    seq = q1.shape[2]
    outs = []
    for i in range(seq // Q_BLOCK):
        s0, e = i * Q_BLOCK, (i + 1) * Q_BLOCK
        p1 = causal_block_probs(q1[:, :, s0:e], k1[:, :, :e], scale, s0)
        p2 = causal_block_probs(q2[:, :, s0:e], k2[:, :, :e], scale, s0)
        w = (p1 - lam * p2).astype(v.dtype)
        outs.append(jnp.einsum('bhqk,bhkd->bhqd', w, v[:, :, :e]))
    return jnp.concatenate(outs, axis=2)


def causal_depthwise_conv(x, w, b):
    seq = x.shape[1]
    xp = jnp.pad(x, ((0, 0), (CONV_WIDTH - 1, 0), (0, 0)))
    out = b
    for j in range(CONV_WIDTH):
        out = out + xp[:, j:j + seq] * w[j]
    return out


def setup_inputs(seed: int = 0) -> dict:
    key = jax.random.key(seed)
    ks = jax.random.split(key, 24)
    L = DEPTH

    def nrm(k, shape, fan_in):
        return jax.random.normal(k, shape, jnp.float32) * (fan_in ** -0.5)

    def gain(k, shape):
        return 1.0 + 0.02 * jax.random.normal(k, shape, jnp.float32)

    return {
        "x": jax.random.normal(ks[0], (BATCH, SEQ, D_MODEL), jnp.float32),
        "attn_norm_g": gain(ks[1], (L, D_MODEL)),
        "w_in": nrm(ks[2], (L, D_MODEL, IN_COLS), D_MODEL),
        "q_a_norm_g": gain(ks[3], (L, MLA_Q_RANK)),
        "w_q_up": nrm(ks[4], (L, MLA_Q_RANK, MLA_HEADS * MLA_QK_DIM), MLA_Q_RANK),
        "kv_a_norm_g": gain(ks[5], (L, MLA_KV_RANK)),
        "w_kv_up": nrm(ks[6], (L, MLA_KV_RANK, MLA_HEADS * (MLA_NOPE_DIM + MLA_V_DIM)), MLA_KV_RANK),
        "mla_q_norm_g": gain(ks[7], (L, MLA_QK_DIM)),
        "mla_k_norm_g": gain(ks[8], (L, MLA_QK_DIM)),
        "diff_q_norm_g": gain(ks[9], (L, DIFF_HEAD_DIM)),
        "diff_k_norm_g": gain(ks[10], (L, DIFF_HEAD_DIM)),
        "lambda_q1": 0.1 * jax.random.normal(ks[11], (L, DIFF_HEAD_DIM), jnp.float32),
        "lambda_k1": 0.1 * jax.random.normal(ks[12], (L, DIFF_HEAD_DIM), jnp.float32),
        "lambda_q2": 0.1 * jax.random.normal(ks[13], (L, DIFF_HEAD_DIM), jnp.float32),
        "lambda_k2": 0.1 * jax.random.normal(ks[14], (L, DIFF_HEAD_DIM), jnp.float32),
        "diff_subln_g": gain(ks[15], (L, DIFF_V_DIM)),
        "w_out": nrm(ks[16], (L, MIX_WIDTH, D_MODEL), MIX_WIDTH),
        "ffn_norm_g": gain(ks[17], (L, D_MODEL)),
        "w_gate": nrm(ks[18], (L, D_MODEL, D_FF), D_MODEL),
        "w_up": nrm(ks[19], (L, D_MODEL, D_FF), D_MODEL),
        "conv_w": nrm(ks[20], (L, CONV_WIDTH, D_FF), CONV_WIDTH),
        "conv_b": 0.02 * jax.random.normal(ks[21], (L, D_FF), jnp.float32),
        "w_down": nrm(ks[22], (L, D_FF, D_MODEL), D_FF),
    }


def reference(x, attn_norm_g, w_in, q_a_norm_g, w_q_up, kv_a_norm_g, w_kv_up,
              mla_q_norm_g, mla_k_norm_g, diff_q_norm_g, diff_k_norm_g,
              lambda_q1, lambda_k1, lambda_q2, lambda_k2, diff_subln_g, w_out,
              ffn_norm_g, w_gate, w_up, conv_w, conv_b, w_down):
    B, S, _ = x.shape
    cos_a, sin_a = rope_tables(S, MLA_ROPE_DIM)
    cos_b, sin_b = rope_tables(S, DIFF_HEAD_DIM)
    split_idx = []
    acc = 0
    for n in IN_SPLITS[:-1]:
        acc += n
        split_idx.append(acc)
    mla_scale = MLA_QK_DIM ** -0.5
    diff_scale = DIFF_HEAD_DIM ** -0.5

    for l in range(DEPTH):
        h = rms_norm(x, attn_norm_g[l])
        proj = h @ w_in[l]
        cq, ckv, kpe, dq, dk, dv = jnp.split(proj, split_idx, axis=-1)

        q = (rms_norm(cq, q_a_norm_g[l]) @ w_q_up[l]).reshape(B, S, MLA_HEADS, MLA_QK_DIM)
        kv = (rms_norm(ckv, kv_a_norm_g[l]) @ w_kv_up[l]).reshape(B, S, MLA_HEADS, MLA_NOPE_DIM + MLA_V_DIM)
        k_nope, v_a = kv[..., :MLA_NOPE_DIM], kv[..., MLA_NOPE_DIM:]
        gq, gk = mla_q_norm_g[l], mla_k_norm_g[l]
        q_nope = rms_norm(q[..., :MLA_NOPE_DIM], gq[:MLA_NOPE_DIM]).transpose(0, 2, 1, 3)
        q_pe = rms_norm(q[..., MLA_NOPE_DIM:], gq[MLA_NOPE_DIM:]).transpose(0, 2, 1, 3)
        k_nope = rms_norm(k_nope, gk[:MLA_NOPE_DIM]).transpose(0, 2, 1, 3)
        k_pe = rms_norm(kpe, gk[MLA_NOPE_DIM:])[:, None]
        q_pe = apply_rope(q_pe, cos_a, sin_a)
        k_pe = apply_rope(k_pe, cos_a, sin_a)
        q_a = jnp.concatenate([q_nope, q_pe], axis=-1)
        k_a = jnp.concatenate([k_nope, jnp.broadcast_to(k_pe, (B, MLA_HEADS, S, MLA_ROPE_DIM))], axis=-1)
        o_a = mla_attention(q_a, k_a, v_a.transpose(0, 2, 1, 3), mla_scale)
        o_a = o_a.transpose(0, 2, 1, 3).reshape(B, S, MLA_HEADS * MLA_V_DIM)

        dq = rms_norm(dq.reshape(B, S, DIFF_HEADS, 2, DIFF_HEAD_DIM), diff_q_norm_g[l])
        dk = rms_norm(dk.reshape(B, S, DIFF_HEADS, 2, DIFF_HEAD_DIM), diff_k_norm_g[l])
        dq = apply_rope(dq.transpose(0, 2, 3, 1, 4), cos_b, sin_b)
        dk = apply_rope(dk.transpose(0, 2, 3, 1, 4), cos_b, sin_b)
        v_b = dv.reshape(B, S, DIFF_HEADS, DIFF_V_DIM).transpose(0, 2, 1, 3)
        lam_init = 0.8 - 0.6 * math.exp(-0.3 * l)
        lam = (jnp.exp(jnp.sum(lambda_q1[l].astype(jnp.float32) * lambda_k1[l].astype(jnp.float32)))
               - jnp.exp(jnp.sum(lambda_q2[l].astype(jnp.float32) * lambda_k2[l].astype(jnp.float32)))
               + lam_init)
        o_b = differential_attention(dq[:, :, 0], dq[:, :, 1], dk[:, :, 0], dk[:, :, 1], v_b, lam, diff_scale)
        o_b = rms_norm(o_b, diff_subln_g[l]) * (1.0 - lam_init)
        o_b = o_b.transpose(0, 2, 1, 3).reshape(B, S, DIFF_HEADS * DIFF_V_DIM)

        mix = jnp.concatenate([o_a, o_b], axis=-1) @ w_out[l]
        x = x + mix

        h = rms_norm(x, ffn_norm_g[l])
        g = causal_depthwise_conv(h @ w_gate[l], conv_w[l], conv_b[l])
        u = h @ w_up[l]
        x = x + (jax.nn.silu(g) * u) @ w_down[l]
    return x
```

```python
import functools
import math

import jax
import jax.numpy as jnp
from jax import lax
from jax.experimental import pallas as pl
from jax.experimental.pallas import tpu as pltpu

D_MODEL = 1024
MLA_HEADS = 8
MLA_NOPE = 64
MLA_ROPE = 32
MLA_V = 64
MLA_QK = MLA_NOPE + MLA_ROPE
Q_RANK = 384
KV_RANK = 256
DIFF_HEADS = 4
DIFF_D = 64
DIFF_V = 2 * DIFF_D
D_FF = 2816
CONV_WIDTH = 3
ROPE_THETA = 10000.0
EPS = 1e-6
LANES = 128

C_Q = 0
C_KV = C_Q + Q_RANK
C_KPE = C_KV + KV_RANK
C_DQ = C_KPE + LANES
C_DK = C_DQ + DIFF_HEADS * LANES
C_DV = C_DK + DIFF_HEADS * LANES
IN_COLS_PADDED = C_DV + DIFF_HEADS * DIFF_V

VMEM_LIMIT = 56 * 1024 * 1024


def _bf16_dot(a, b):
    return jnp.dot(a.astype(jnp.bfloat16), b.astype(jnp.bfloat16),
                   preferred_element_type=jnp.float32)


def _rms_rows(x, g):
    ms = jnp.mean(x * x, axis=-1, keepdims=True)
    return x * lax.rsqrt(ms + EPS) * g


def _proj_kernel(x_ref, g_attn_ref, w_in_ref, g_qa_ref, w_q_ref, g_kva_ref, w_k_ref, w_v_ref,
                 g_q_ref, g_k_ref, g_dq_ref, g_dk_ref,
                 cos_a_ref, sin_a_ref, cos_b_ref, sin_b_ref,
                 qt_ref, k_ref, vt_ref, dqt_ref, dk_ref, dvt_ref):
    tm = x_ref.shape[1]
    lane = lax.broadcasted_iota(jnp.int32, (tm, LANES), 1)
    nope_mask = lane < MLA_NOPE

    h = _rms_rows(x_ref[0], g_attn_ref[...])
    proj = _bf16_dot(h, w_in_ref[...])

    cos_a = cos_a_ref[...]
    sin_a = sin_a_ref[...]
    cos_b = cos_b_ref[...]
    sin_b = sin_b_ref[...]

    def rope_a(xn):
        rot = jnp.where(lane < MLA_NOPE + MLA_ROPE // 2,
                        pltpu.roll(xn, LANES - MLA_ROPE // 2, axis=1),
                        pltpu.roll(xn, MLA_ROPE // 2, axis=1))
        return xn * cos_a + rot * sin_a

    def rope_b(xn):
        rot = jnp.where((lane & (DIFF_D - 1)) < DIFF_D // 2,
                        pltpu.roll(xn, LANES - DIFF_D // 2, axis=1),
                        pltpu.roll(xn, DIFF_D // 2, axis=1))
        return xn * cos_b + rot * sin_b

    def split_norm(xg, n_lo, n_hi, g):
        sq = xg * xg
        ss_lo = jnp.sum(jnp.where(nope_mask, sq, 0.0), axis=-1, keepdims=True)
        ss_hi = jnp.sum(jnp.where(nope_mask, 0.0, sq), axis=-1, keepdims=True)
        r_lo = lax.rsqrt(ss_lo * (1.0 / n_lo) + EPS)
        r_hi = lax.rsqrt(ss_hi * (1.0 / n_hi) + EPS)
        return xg * jnp.where(nope_mask, r_lo, r_hi) * g

    cq = _rms_rows(proj[:, C_Q:C_Q + Q_RANK], g_qa_ref[...])
    q = _bf16_dot(cq, w_q_ref[...])
    g_q = g_q_ref[...] * (MLA_QK ** -0.5)
    for hd in range(MLA_HEADS):
        qh = split_norm(q[:, hd * LANES:(hd + 1) * LANES], MLA_NOPE, MLA_ROPE, g_q)
        qh = rope_a(qh)
        qt_ref[0, hd * LANES:(hd + 1) * LANES, :] = qh.T.astype(qt_ref.dtype)

    ckv = _rms_rows(proj[:, C_KV:C_KV + KV_RANK], g_kva_ref[...])
    g_k = g_k_ref[...]
    kpe = split_norm(proj[:, C_KPE:C_KPE + LANES], MLA_NOPE, MLA_ROPE, g_k)
    kpe = rope_a(kpe)
    kn = _bf16_dot(ckv, w_k_ref[...])
    for hd in range(MLA_HEADS):
        kh = kn[:, hd * LANES:(hd + 1) * LANES]
        ss = jnp.sum(kh * kh, axis=-1, keepdims=True)
        kh = kh * lax.rsqrt(ss * (1.0 / MLA_NOPE) + EPS) * g_k
        k_ref[0, :, hd * LANES:(hd + 1) * LANES] = jnp.where(nope_mask, kh, kpe).astype(k_ref.dtype)
    v = _bf16_dot(ckv, w_v_ref[...])
    for pr in range(MLA_HEADS * MLA_V // LANES):
        vt_ref[0, 0, pr * LANES:(pr + 1) * LANES, :] = (
            v[:, pr * LANES:(pr + 1) * LANES].T.astype(vt_ref.dtype))

    g_dq = g_dq_ref[...] * (DIFF_D ** -0.5)
    g_dk = g_dk_ref[...]
    row = lax.broadcasted_iota(jnp.int32, (LANES, tm), 0)
    for hd in range(DIFF_HEADS):
        dq = split_norm(proj[:, C_DQ + hd * LANES:C_DQ + (hd + 1) * LANES], DIFF_D, DIFF_D, g_dq)
        dqt = rope_b(dq).T
        dqt_ref[0, (2 * hd) * LANES:(2 * hd + 1) * LANES, :] = (
            jnp.where(row < DIFF_D, dqt, 0.0).astype(dqt_ref.dtype))
        dqt_ref[0, (2 * hd + 1) * LANES:(2 * hd + 2) * LANES, :] = (
            jnp.where(row < DIFF_D, 0.0, dqt).astype(dqt_ref.dtype))
        dk = split_norm(proj[:, C_DK + hd * LANES:C_DK + (hd + 1) * LANES], DIFF_D, DIFF_D, g_dk)
        dk_ref[0, :, hd * LANES:(hd + 1) * LANES] = rope_b(dk).astype(dk_ref.dtype)
        dvt_ref[0, 0, hd * LANES:(hd + 1) * LANES, :] = (
            proj[:, C_DV + hd * LANES:C_DV + (hd + 1) * LANES].T.astype(dvt_ref.dtype))


def _softmax_step(s_t, m, l):
    m_new = jnp.maximum(m, jnp.max(s_t, axis=0, keepdims=True))
    alpha = jnp.exp(m - m_new)
    p = jnp.exp(s_t - m_new)
    l_new = alpha * l + jnp.sum(p, axis=0, keepdims=True)
    return p, alpha, m_new, l_new


def _causal_mask(s_t, tk, tq):
    key = lax.broadcasted_iota(jnp.int32, (tk, tq), 0)
    qry = lax.broadcasted_iota(jnp.int32, (tk, tq), 1)
    return jnp.where(key <= qry, s_t, -jnp.inf)


def _mla_kernel(qt_ref, k_ref, vt_ref, o_ref, *, tq):
    qi = pl.program_id(2)
    qt = qt_ref[0]

    def tile(j, carry, masked):
        m, l, acc = carry
        off = pl.multiple_of(j * tq, tq)
        s_t = jnp.dot(k_ref[0, pl.ds(off, tq), :], qt, preferred_element_type=jnp.float32)
        if masked:
            s_t = _causal_mask(s_t, tq, tq)
        p, alpha, m, l = _softmax_step(s_t, m, l)
        pv = jnp.dot(vt_ref[0, j], p.astype(jnp.bfloat16),
                     preferred_element_type=jnp.float32)
        return m, l, alpha * acc + pv

    init = (jnp.full((1, tq), -jnp.inf, jnp.float32), jnp.zeros((1, tq), jnp.float32),
            jnp.zeros((MLA_V, tq), jnp.float32))
    carry = lax.fori_loop(0, qi, functools.partial(tile, masked=False), init)
    m, l, acc = tile(qi, carry, True)
    o_ref[0] = (acc / l).astype(o_ref.dtype)


def _diff_kernel(lq1_ref, lk1_ref, lq2_ref, lk2_ref, g_sub_ref, dqt_ref, dk_ref, dvt_ref, o_ref,
                 *, tq, lam_init):
    qi = pl.program_id(2)
    q1t = dqt_ref[0, 0:LANES, :]
    q2t = dqt_ref[0, LANES:2 * LANES, :]
    lam = (jnp.exp(jnp.sum(lq1_ref[...] * lk1_ref[...], axis=-1, keepdims=True))
           - jnp.exp(jnp.sum(lq2_ref[...] * lk2_ref[...], axis=-1, keepdims=True))
           + lam_init)

    def tile(j, carry, masked):
        m1, l1, a1, m2, l2, a2 = carry
        off = pl.multiple_of(j * tq, tq)
        kt = dk_ref[0, pl.ds(off, tq), :]
        vt = dvt_ref[0, j]
        s1 = jnp.dot(kt, q1t, preferred_element_type=jnp.float32)
        s2 = jnp.dot(kt, q2t, preferred_element_type=jnp.float32)
        if masked:
            s1 = _causal_mask(s1, tq, tq)
            s2 = _causal_mask(s2, tq, tq)
        p1, al1, m1, l1 = _softmax_step(s1, m1, l1)
        p2, al2, m2, l2 = _softmax_step(s2, m2, l2)
        a1 = al1 * a1 + jnp.dot(vt, p1.astype(jnp.bfloat16), preferred_element_type=jnp.float32)
        a2 = al2 * a2 + jnp.dot(vt, p2.astype(jnp.bfloat16), preferred_element_type=jnp.float32)
        return m1, l1, a1, m2, l2, a2

    neg = jnp.full((1, tq), -jnp.inf, jnp.float32)
    zero = jnp.zeros((1, tq), jnp.float32)
    zacc = jnp.zeros((DIFF_V, tq), jnp.float32)
    carry = lax.fori_loop(0, qi, functools.partial(tile, masked=False),
                          (neg, zero, zacc, neg, zero, zacc))
    m1, l1, a1, m2, l2, a2 = tile(qi, carry, True)
    o = a1 / l1 - lam * (a2 / l2)
    ms = jnp.mean(o * o, axis=0, keepdims=True)
    o = o * lax.rsqrt(ms + EPS) * g_sub_ref[...] * (1.0 - lam_init)
    o_ref[0] = o.astype(o_ref.dtype)


FF_CHUNK = 512


def _ffn_kernel(x_ref, oa_ref, ob_ref, w_oa_ref, w_ob_ref, g_ffn_ref, w_gate_ref, w_up_ref,
                conv_w_ref, conv_b_ref, w_down_ref, out_ref, prev_ref, y_ref):
    si = pl.program_id(1)
    tm = x_ref.shape[1]
    contract0 = (((0,), (0,)), ((), ()))
    mix = (lax.dot_general(oa_ref[0], w_oa_ref[...], contract0, preferred_element_type=jnp.float32)
           + lax.dot_general(ob_ref[0], w_ob_ref[...], contract0,
                             preferred_element_type=jnp.float32))
    x1 = x_ref[0] + mix
    h = _rms_rows(x1, g_ffn_ref[...]).astype(jnp.bfloat16)

    @pl.when(si == 0)
    def _():
        prev_ref[...] = jnp.zeros_like(prev_ref)

    for c0 in range(0, D_FF, FF_CHUNK):
        cw = min(FF_CHUNK, D_FF - c0)
        g = jnp.dot(h, w_gate_ref[:, c0:c0 + cw], preferred_element_type=jnp.float32)
        u = jnp.dot(h, w_up_ref[:, c0:c0 + cw], preferred_element_type=jnp.float32)
        row = lax.broadcasted_iota(jnp.int32, (tm, cw), 0)
        p1 = prev_ref[7:8, c0:c0 + cw]
        p2 = prev_ref[6:7, c0:c0 + cw]
        g1 = jnp.where(row == 0, p1, pltpu.roll(g, 1, axis=0))
        g2 = jnp.where(row == 0, p2, jnp.where(row == 1, p1, pltpu.roll(g, 2, axis=0)))
        prev_ref[:, c0:c0 + cw] = g[tm - 8:tm, :]
        cg = (conv_b_ref[:, c0:c0 + cw] + g2 * conv_w_ref[0:1, c0:c0 + cw]
              + g1 * conv_w_ref[1:2, c0:c0 + cw] + g * conv_w_ref[2:3, c0:c0 + cw])
        y_ref[:, c0:c0 + cw] = (jax.nn.silu(cg) * u).astype(y_ref.dtype)

    out_ref[0] = x1 + jnp.dot(y_ref[...], w_down_ref[...], preferred_element_type=jnp.float32)


def _rope_tables(seq):
    pos = jnp.arange(seq, dtype=jnp.float32)[:, None]

    def tables(dim):
        inv = 1.0 / (ROPE_THETA ** (jnp.arange(0, dim, 2, dtype=jnp.float32) / dim))
        ang = pos * inv[None, :]
        return jnp.cos(ang), jnp.sin(ang)

    ca, sa = tables(MLA_ROPE)
    cb, sb = tables(DIFF_D)
    ones = jnp.ones((seq, MLA_NOPE), jnp.float32)
    zeros_n = jnp.zeros((seq, MLA_NOPE), jnp.float32)
    zeros_p = jnp.zeros((seq, LANES - MLA_QK), jnp.float32)
    cos_a = jnp.concatenate([ones, ca, ca, zeros_p], axis=1)
    sin_a = jnp.concatenate([zeros_n, -sa, sa, zeros_p], axis=1)
    cos_b = jnp.concatenate([cb, cb, cb, cb], axis=1)
    sin_b = jnp.concatenate([-sb, sb, -sb, sb], axis=1)
    return cos_a, sin_a, cos_b, sin_b


def _pad_heads(w, heads, width):
    r = w.shape[0]
    w = w.reshape(r, heads, width)
    return jnp.pad(w, ((0, 0), (0, 0), (0, LANES - width))).reshape(r, heads * LANES)


def _pad_lanes(v, offset=0):
    return jnp.pad(v, (offset, LANES - offset - v.shape[0])).reshape(1, LANES)


def _const_spec(shape):
    return pl.BlockSpec(shape, lambda *_: (0,) * len(shape))


def kernel(x, attn_norm_g, w_in, q_a_norm_g, w_q_up, kv_a_norm_g, w_kv_up, mla_q_norm_g,
           mla_k_norm_g, diff_q_norm_g, diff_k_norm_g, lambda_q1, lambda_k1, lambda_q2, lambda_k2,
           diff_subln_g, w_out, ffn_norm_g, w_gate, w_up, conv_w, conv_b, w_down):
    B, S, _ = x.shape
    depth = w_in.shape[0]
    bf = jnp.bfloat16
    cos_a, sin_a, cos_b, sin_b = _rope_tables(S)

    tm_proj = 256
    tq = 256
    tm_ffn = 512
    assert tq == tm_proj
    assert S % tq == 0 and S % tm_ffn == 0

    for l in range(depth):
        wi = w_in[l]
        o_kv = Q_RANK
        o_kpe = o_kv + KV_RANK
        o_dq = o_kpe + MLA_ROPE
        kpe_cols = jnp.pad(wi[:, o_kpe:o_dq], ((0, 0), (MLA_NOPE, LANES - MLA_QK)))
        w_in_p = jnp.concatenate([wi[:, :o_kpe], kpe_cols, wi[:, o_dq:]], axis=1).astype(bf)
        w_q_p = _pad_heads(w_q_up[l], MLA_HEADS, MLA_QK).astype(bf)
        wkv = w_kv_up[l].reshape(KV_RANK, MLA_HEADS, MLA_NOPE + MLA_V)
        w_k_p = _pad_heads(wkv[:, :, :MLA_NOPE].reshape(KV_RANK, -1), MLA_HEADS, MLA_NOPE).astype(bf)
        w_v_p = wkv[:, :, MLA_NOPE:].reshape(KV_RANK, MLA_HEADS * MLA_V).astype(bf)
        g_q = _pad_lanes(mla_q_norm_g[l])
        g_k = _pad_lanes(mla_k_norm_g[l])
        g_dq = jnp.tile(diff_q_norm_g[l], 2).reshape(1, LANES)
        g_dk = jnp.tile(diff_k_norm_g[l], 2).reshape(1, LANES)

        n_tok_tiles = S // tm_proj
        tok3 = lambda b, s: (b, s, 0)
        feat3 = lambda b, s: (b, 0, s)
        blk4 = lambda b, s: (b, s, 0, 0)
        rope_spec = pl.BlockSpec((tm_proj, LANES), lambda b, s: (s, 0))
        qt, k_a, vt, dqt, dk, dvt = pl.pallas_call(
            _proj_kernel,
            grid=(B, n_tok_tiles),
            in_specs=[
                pl.BlockSpec((1, tm_proj, D_MODEL), tok3),
                _const_spec((1, D_MODEL)),
                _const_spec((D_MODEL, IN_COLS_PADDED)),
                _const_spec((1, Q_RANK)),
                _const_spec((Q_RANK, MLA_HEADS * LANES)),
                _const_spec((1, KV_RANK)),
                _const_spec((KV_RANK, MLA_HEADS * LANES)),
                _const_spec((KV_RANK, MLA_HEADS * MLA_V)),
                _const_spec((1, LANES)), _const_spec((1, LANES)),
                _const_spec((1, LANES)), _const_spec((1, LANES)),
                rope_spec, rope_spec, rope_spec, rope_spec,
            ],
            out_specs=[
                pl.BlockSpec((1, MLA_HEADS * LANES, tm_proj), feat3),
                pl.BlockSpec((1, tm_proj, MLA_HEADS * LANES), tok3),
                pl.BlockSpec((1, 1, MLA_HEADS * MLA_V, tm_proj), blk4),
                pl.BlockSpec((1, DIFF_HEADS * 2 * LANES, tm_proj), feat3),
                pl.BlockSpec((1, tm_proj, DIFF_HEADS * LANES), tok3),
                pl.BlockSpec((1, 1, DIFF_HEADS * DIFF_V, tm_proj), blk4),
            ],
            out_shape=[
                jax.ShapeDtypeStruct((B, MLA_HEADS * LANES, S), bf),
                jax.ShapeDtypeStruct((B, S, MLA_HEADS * LANES), bf),
                jax.ShapeDtypeStruct((B, n_tok_tiles, MLA_HEADS * MLA_V, tm_proj), bf),
                jax.ShapeDtypeStruct((B, DIFF_HEADS * 2 * LANES, S), bf),
                jax.ShapeDtypeStruct((B, S, DIFF_HEADS * LANES), bf),
                jax.ShapeDtypeStruct((B, n_tok_tiles, DIFF_HEADS * DIFF_V, tm_proj), bf),
            ],
            compiler_params=pltpu.CompilerParams(
                dimension_semantics=("arbitrary", "arbitrary"), vmem_limit_bytes=VMEM_LIMIT),
            name="proj",
        )(x, attn_norm_g[l].reshape(1, -1), w_in_p, q_a_norm_g[l].reshape(1, -1), w_q_p,
          kv_a_norm_g[l].reshape(1, -1), w_k_p, w_v_p, g_q, g_k, g_dq, g_dk,
          cos_a, sin_a, cos_b, sin_b)

        nq = S // tq
        o_a = pl.pallas_call(
            functools.partial(_mla_kernel, tq=tq),
            grid=(B, MLA_HEADS, nq),
            in_specs=[
                pl.BlockSpec((1, LANES, tq), lambda b, h, i: (b, h, i)),
                pl.BlockSpec((1, S, LANES), lambda b, h, i: (b, 0, h)),
                pl.BlockSpec((1, nq, MLA_V, tq), lambda b, h, i: (b, 0, h, 0)),
            ],
            out_specs=pl.BlockSpec((1, MLA_V, tq), lambda b, h, i: (b, h, i)),
            out_shape=jax.ShapeDtypeStruct((B, MLA_HEADS * MLA_V, S), bf),
            compiler_params=pltpu.CompilerParams(
                dimension_semantics=("arbitrary", "arbitrary", "arbitrary"),
                vmem_limit_bytes=VMEM_LIMIT),
            name="mla_attn",
        )(qt, k_a, vt)

        lam_init = 0.8 - 0.6 * math.exp(-0.3 * l)
        lam_spec = _const_spec((1, DIFF_D))
        o_b = pl.pallas_call(
            functools.partial(_diff_kernel, tq=tq, lam_init=lam_init),
            grid=(B, DIFF_HEADS, nq),
            in_specs=[
                lam_spec, lam_spec, lam_spec, lam_spec,
                _const_spec((DIFF_V, 1)),
                pl.BlockSpec((1, 2 * LANES, tq), lambda b, h, i: (b, h, i)),
                pl.BlockSpec((1, S, LANES), lambda b, h, i: (b, 0, h)),
                pl.BlockSpec((1, nq, DIFF_V, tq), lambda b, h, i: (b, 0, h, 0)),
            ],
            out_specs=pl.BlockSpec((1, DIFF_V, tq), lambda b, h, i: (b, h, i)),
            out_shape=jax.ShapeDtypeStruct((B, DIFF_HEADS * DIFF_V, S), bf),
            compiler_params=pltpu.CompilerParams(
                dimension_semantics=("arbitrary", "arbitrary", "arbitrary"),
                vmem_limit_bytes=VMEM_LIMIT),
            name="diff_attn",
        )(lambda_q1[l].reshape(1, -1), lambda_k1[l].reshape(1, -1),
          lambda_q2[l].reshape(1, -1), lambda_k2[l].reshape(1, -1),
          diff_subln_g[l].reshape(-1, 1), dqt, dk, dvt)

        n_a = MLA_HEADS * MLA_V
        x = pl.pallas_call(
            _ffn_kernel,
            grid=(B, S // tm_ffn),
            in_specs=[
                pl.BlockSpec((1, tm_ffn, D_MODEL), lambda b, s: (b, s, 0)),
                pl.BlockSpec((1, n_a, tm_ffn), lambda b, s: (b, 0, s)),
                pl.BlockSpec((1, D_MODEL - n_a, tm_ffn), lambda b, s: (b, 0, s)),
                _const_spec((n_a, D_MODEL)),
                _const_spec((D_MODEL - n_a, D_MODEL)),
                _const_spec((1, D_MODEL)),
                _const_spec((D_MODEL, D_FF)),
                _const_spec((D_MODEL, D_FF)),
                _const_spec((CONV_WIDTH, D_FF)),
                _const_spec((1, D_FF)),
                _const_spec((D_FF, D_MODEL)),
            ],
            out_specs=pl.BlockSpec((1, tm_ffn, D_MODEL), lambda b, s: (b, s, 0)),
            out_shape=jax.ShapeDtypeStruct((B, S, D_MODEL), x.dtype),
            scratch_shapes=[pltpu.VMEM((8, D_FF), jnp.float32),
                            pltpu.VMEM((tm_ffn, D_FF), bf)],
            compiler_params=pltpu.CompilerParams(
                dimension_semantics=("arbitrary", "arbitrary"), vmem_limit_bytes=VMEM_LIMIT),
            name="ffn",
        )(x, o_a, o_b, w_out[l][:n_a].astype(bf), w_out[l][n_a:].astype(bf),
          ffn_norm_g[l].reshape(1, -1), w_gate[l].astype(bf), w_up[l].astype(bf),
          conv_w[l], conv_b[l].reshape(1, -1), w_down[l].astype(bf))
    return x
```

```python
import functools
import math

import jax
import jax.numpy as jnp
from jax import lax
from jax.experimental import pallas as pl
from jax.experimental.pallas import tpu as pltpu

D_MODEL = 1024
MLA_HEADS = 8
MLA_NOPE = 64
MLA_ROPE = 32
MLA_V = 64
MLA_QK = MLA_NOPE + MLA_ROPE
Q_RANK = 384
KV_RANK = 256
DIFF_HEADS = 4
DIFF_D = 64
DIFF_V = 2 * DIFF_D
D_FF = 2816
CONV_WIDTH = 3
ROPE_THETA = 10000.0
EPS = 1e-6
LANES = 128

C_Q = 0
C_KV = C_Q + Q_RANK
C_KPE = C_KV + KV_RANK
C_DQ = C_KPE + LANES
C_DK = C_DQ + DIFF_HEADS * LANES
C_DV = C_DK + DIFF_HEADS * LANES
IN_COLS_PADDED = C_DV + DIFF_HEADS * DIFF_V

VMEM_LIMIT = 56 * 1024 * 1024


def _bf16_dot(a, b):
    return jnp.dot(a.astype(jnp.bfloat16), b.astype(jnp.bfloat16),
                   preferred_element_type=jnp.float32)


def _rms_rows(x, g):
    ms = jnp.mean(x * x, axis=-1, keepdims=True)
    return x * lax.rsqrt(ms + EPS) * g


def _proj_kernel(x_ref, g_attn_ref, w_in_ref, g_qa_ref, w_q_ref, g_kva_ref, w_k_ref, w_v_ref,
                 g_q_ref, g_k_ref, g_dq_ref, g_dk_ref,
                 cos_a_ref, sin_a_ref, cos_b_ref, sin_b_ref,
                 qt_ref, k_ref, vt_ref, dqt_ref, dk_ref, dvt_ref):
    tm = x_ref.shape[1]
    lane = lax.broadcasted_iota(jnp.int32, (tm, LANES), 1)
    nope_mask = lane < MLA_NOPE

    h = _rms_rows(x_ref[0], g_attn_ref[...])
    proj = _bf16_dot(h, w_in_ref[...])

    cos_a = cos_a_ref[...]
    sin_a = sin_a_ref[...]
    cos_b = cos_b_ref[...]
    sin_b = sin_b_ref[...]

    def rope_a(xn):
        rot = jnp.where(lane < MLA_NOPE + MLA_ROPE // 2,
                        pltpu.roll(xn, LANES - MLA_ROPE // 2, axis=1),
                        pltpu.roll(xn, MLA_ROPE // 2, axis=1))
        return xn * cos_a + rot * sin_a

    def rope_b(xn):
        rot = jnp.where((lane & (DIFF_D - 1)) < DIFF_D // 2,
                        pltpu.roll(xn, LANES - DIFF_D // 2, axis=1),
                        pltpu.roll(xn, DIFF_D // 2, axis=1))
        return xn * cos_b + rot * sin_b

    def split_norm(xg, n_lo, n_hi, g):
        sq = xg * xg
        ss_lo = jnp.sum(jnp.where(nope_mask, sq, 0.0), axis=-1, keepdims=True)
        ss_hi = jnp.sum(jnp.where(nope_mask, 0.0, sq), axis=-1, keepdims=True)
        r_lo = lax.rsqrt(ss_lo * (1.0 / n_lo) + EPS)
        r_hi = lax.rsqrt(ss_hi * (1.0 / n_hi) + EPS)
        return xg * jnp.where(nope_mask, r_lo, r_hi) * g

    cq = _rms_rows(proj[:, C_Q:C_Q + Q_RANK], g_qa_ref[...])
    q = _bf16_dot(cq, w_q_ref[...])
    g_q = g_q_ref[...] * (MLA_QK ** -0.5)
    for hd in range(MLA_HEADS):
        qh = split_norm(q[:, hd * LANES:(hd + 1) * LANES], MLA_NOPE, MLA_ROPE, g_q)
        qh = rope_a(qh)
        qt_ref[0, hd * LANES:(hd + 1) * LANES, :] = qh.T.astype(qt_ref.dtype)

    ckv = _rms_rows(proj[:, C_KV:C_KV + KV_RANK], g_kva_ref[...])
    g_k = g_k_ref[...]
    kpe = split_norm(proj[:, C_KPE:C_KPE + LANES], MLA_NOPE, MLA_ROPE, g_k)
    kpe = rope_a(kpe)
    kn = _bf16_dot(ckv, w_k_ref[...])
    for hd in range(MLA_HEADS):
        kh = kn[:, hd * LANES:(hd + 1) * LANES]
        ss = jnp.sum(kh * kh, axis=-1, keepdims=True)
        kh = kh * lax.rsqrt(ss * (1.0 / MLA_NOPE) + EPS) * g_k
        k_ref[0, :, hd * LANES:(hd + 1) * LANES] = jnp.where(nope_mask, kh, kpe).astype(k_ref.dtype)
    v = _bf16_dot(ckv, w_v_ref[...])
    for pr in range(MLA_HEADS * MLA_V // LANES):
        vt_ref[0, 0, pr * LANES:(pr + 1) * LANES, :] = (
            v[:, pr * LANES:(pr + 1) * LANES].T.astype(vt_ref.dtype))

    g_dq = g_dq_ref[...] * (DIFF_D ** -0.5)
    g_dk = g_dk_ref[...]
    row = lax.broadcasted_iota(jnp.int32, (LANES, tm), 0)
    for hd in range(DIFF_HEADS):
        dq = split_norm(proj[:, C_DQ + hd * LANES:C_DQ + (hd + 1) * LANES], DIFF_D, DIFF_D, g_dq)
        dqt = rope_b(dq).T
        dqt_ref[0, (2 * hd) * LANES:(2 * hd + 1) * LANES, :] = (
            jnp.where(row < DIFF_D, dqt, 0.0).astype(dqt_ref.dtype))
        dqt_ref[0, (2 * hd + 1) * LANES:(2 * hd + 2) * LANES, :] = (
            jnp.where(row < DIFF_D, 0.0, dqt).astype(dqt_ref.dtype))
        dk = split_norm(proj[:, C_DK + hd * LANES:C_DK + (hd + 1) * LANES], DIFF_D, DIFF_D, g_dk)
        dk_ref[0, :, hd * LANES:(hd + 1) * LANES] = rope_b(dk).astype(dk_ref.dtype)
        dvt_ref[0, 0, hd * LANES:(hd + 1) * LANES, :] = (
            proj[:, C_DV + hd * LANES:C_DV + (hd + 1) * LANES].T.astype(dvt_ref.dtype))


def _softmax_step(s_t, m, l):
    m_new = jnp.maximum(m, jnp.max(s_t, axis=0, keepdims=True))
    alpha = jnp.exp(m - m_new)
    p = jnp.exp(s_t - m_new)
    l_new = alpha * l + jnp.sum(p, axis=0, keepdims=True)
    return p, alpha, m_new, l_new


def _causal_mask(s_t, tk, tq):
    key = lax.broadcasted_iota(jnp.int32, (tk, tq), 0)
    qry = lax.broadcasted_iota(jnp.int32, (tk, tq), 1)
    return jnp.where(key <= qry, s_t, -jnp.inf)


def _flash_chain(s_t, vt, m_ref, l_ref, acc_ref, c, rows):
    p, alpha, m_new, l_new = _softmax_step(s_t, m_ref[c], l_ref[c])
    m_ref[c] = m_new
    l_ref[c] = l_new
    acc_ref[rows, :] = alpha * acc_ref[rows, :] + jnp.dot(
        vt, p.astype(jnp.bfloat16), preferred_element_type=jnp.float32)


def _init_stats(m_ref, l_ref, acc_ref):
    m_ref[...] = jnp.full(m_ref.shape, -jnp.inf, m_ref.dtype)
    l_ref[...] = jnp.zeros_like(l_ref)
    acc_ref[...] = jnp.zeros_like(acc_ref)


def _sweep_key_tiles(tile, qi):
    def body(j, carry):
        tile(j, False)
        return carry

    lax.fori_loop(0, qi, body, 0)
    tile(qi, True)


def _mla_kernel(qt_ref, k_ref, vt_ref, o_ref, m_ref, l_ref, acc_ref, *, tq):
    qi = pl.program_id(1)
    _init_stats(m_ref, l_ref, acc_ref)

    def tile(j, masked):
        off = pl.multiple_of(j * tq, tq)

        def scores(hd):
            s_t = jnp.dot(k_ref[0, pl.ds(off, tq), hd * LANES:(hd + 1) * LANES],
                          qt_ref[0, hd * LANES:(hd + 1) * LANES, :],
                          preferred_element_type=jnp.float32)
            return _causal_mask(s_t, tq, tq) if masked else s_t

        s_all = [scores(hd) for hd in range(MLA_HEADS)]
        for hd in range(MLA_HEADS):
            rows = slice(hd * MLA_V, (hd + 1) * MLA_V)
            _flash_chain(s_all[hd], vt_ref[0, j, rows, :], m_ref, l_ref, acc_ref, hd, rows)

    _sweep_key_tiles(tile, qi)
    for hd in range(MLA_HEADS):
        rows = slice(hd * MLA_V, (hd + 1) * MLA_V)
        o_ref[0, rows, :] = (acc_ref[rows, :] / l_ref[hd]).astype(o_ref.dtype)


def _diff_kernel(lq1_ref, lk1_ref, lq2_ref, lk2_ref, g_sub_ref, dqt_ref, dk_ref, dvt_ref, o_ref,
                 m_ref, l_ref, acc_ref, *, tq, lam_init):
    qi = pl.program_id(1)
    _init_stats(m_ref, l_ref, acc_ref)
    lam = (jnp.exp(jnp.sum(lq1_ref[...] * lk1_ref[...], axis=-1, keepdims=True))
           - jnp.exp(jnp.sum(lq2_ref[...] * lk2_ref[...], axis=-1, keepdims=True))
           + lam_init)

    def tile(j, masked):
        off = pl.multiple_of(j * tq, tq)

        def scores(c):
            hd = c // 2
            s_t = jnp.dot(dk_ref[0, pl.ds(off, tq), hd * LANES:(hd + 1) * LANES],
                          dqt_ref[0, c * LANES:(c + 1) * LANES, :],
                          preferred_element_type=jnp.float32)
            return _causal_mask(s_t, tq, tq) if masked else s_t

        s_all = [scores(c) for c in range(2 * DIFF_HEADS)]
        for c in range(2 * DIFF_HEADS):
            hd = c // 2
            vt = dvt_ref[0, j, hd * DIFF_V:(hd + 1) * DIFF_V, :]
            _flash_chain(s_all[c], vt, m_ref, l_ref, acc_ref, c, slice(c * DIFF_V, (c + 1) * DIFF_V))

    _sweep_key_tiles(tile, qi)
    for hd in range(DIFF_HEADS):
        r1 = slice((2 * hd) * DIFF_V, (2 * hd + 1) * DIFF_V)
        r2 = slice((2 * hd + 1) * DIFF_V, (2 * hd + 2) * DIFF_V)
        o = acc_ref[r1, :] / l_ref[2 * hd] - lam * (acc_ref[r2, :] / l_ref[2 * hd + 1])
        ms = jnp.mean(o * o, axis=0, keepdims=True)
        o = o * lax.rsqrt(ms + EPS) * g_sub_ref[...] * (1.0 - lam_init)
        o_ref[0, hd * DIFF_V:(hd + 1) * DIFF_V, :] = o.astype(o_ref.dtype)


FF_CHUNK = 512


def _ffn_kernel(x_ref, oa_ref, ob_ref, w_oa_ref, w_ob_ref, g_ffn_ref, w_gate_ref, w_up_ref,
                conv_w_ref, conv_b_ref, w_down_ref, out_ref, prev_ref, y_ref):
    si = pl.program_id(1)
    tm = x_ref.shape[1]
    contract0 = (((0,), (0,)), ((), ()))
    mix = (lax.dot_general(oa_ref[0], w_oa_ref[...], contract0, preferred_element_type=jnp.float32)
           + lax.dot_general(ob_ref[0], w_ob_ref[...], contract0,
                             preferred_element_type=jnp.float32))
    x1 = x_ref[0] + mix
    h = _rms_rows(x1, g_ffn_ref[...]).astype(jnp.bfloat16)

    @pl.when(si == 0)
    def _():
        prev_ref[...] = jnp.zeros_like(prev_ref)

    for c0 in range(0, D_FF, FF_CHUNK):
        cw = min(FF_CHUNK, D_FF - c0)
        g = jnp.dot(h, w_gate_ref[:, c0:c0 + cw], preferred_element_type=jnp.float32)
        u = jnp.dot(h, w_up_ref[:, c0:c0 + cw], preferred_element_type=jnp.float32)
        row = lax.broadcasted_iota(jnp.int32, (tm, cw), 0)
        p1 = prev_ref[7:8, c0:c0 + cw]
        p2 = prev_ref[6:7, c0:c0 + cw]
        g1 = jnp.where(row == 0, p1, pltpu.roll(g, 1, axis=0))
        g2 = jnp.where(row == 0, p2, jnp.where(row == 1, p1, pltpu.roll(g, 2, axis=0)))
        prev_ref[:, c0:c0 + cw] = g[tm - 8:tm, :]
        cg = (conv_b_ref[:, c0:c0 + cw] + g2 * conv_w_ref[0:1, c0:c0 + cw]
              + g1 * conv_w_ref[1:2, c0:c0 + cw] + g * conv_w_ref[2:3, c0:c0 + cw])
        y_ref[:, c0:c0 + cw] = (jax.nn.silu(cg) * u).astype(y_ref.dtype)

    out_ref[0] = x1 + jnp.dot(y_ref[...], w_down_ref[...], preferred_element_type=jnp.float32)


def _rope_tables(seq):
    pos = jnp.arange(seq, dtype=jnp.float32)[:, None]

    def tables(dim):
        inv = 1.0 / (ROPE_THETA ** (jnp.arange(0, dim, 2, dtype=jnp.float32) / dim))
        ang = pos * inv[None, :]
        return jnp.cos(ang), jnp.sin(ang)

    ca, sa = tables(MLA_ROPE)
    cb, sb = tables(DIFF_D)
    ones = jnp.ones((seq, MLA_NOPE), jnp.float32)
    zeros_n = jnp.zeros((seq, MLA_NOPE), jnp.float32)
    zeros_p = jnp.zeros((seq, LANES - MLA_QK), jnp.float32)
    cos_a = jnp.concatenate([ones, ca, ca, zeros_p], axis=1)
    sin_a = jnp.concatenate([zeros_n, -sa, sa, zeros_p], axis=1)
    cos_b = jnp.concatenate([cb, cb, cb, cb], axis=1)
    sin_b = jnp.concatenate([-sb, sb, -sb, sb], axis=1)
    return cos_a, sin_a, cos_b, sin_b


def _pad_heads(w, heads, width):
    r = w.shape[0]
    w = w.reshape(r, heads, width)
    return jnp.pad(w, ((0, 0), (0, 0), (0, LANES - width))).reshape(r, heads * LANES)


def _pad_lanes(v, offset=0):
    return jnp.pad(v, (offset, LANES - offset - v.shape[0])).reshape(1, LANES)


def _const_spec(shape):
    return pl.BlockSpec(shape, lambda *_: (0,) * len(shape))


def kernel(x, attn_norm_g, w_in, q_a_norm_g, w_q_up, kv_a_norm_g, w_kv_up, mla_q_norm_g,
           mla_k_norm_g, diff_q_norm_g, diff_k_norm_g, lambda_q1, lambda_k1, lambda_q2, lambda_k2,
           diff_subln_g, w_out, ffn_norm_g, w_gate, w_up, conv_w, conv_b, w_down):
    B, S, _ = x.shape
    depth = w_in.shape[0]
    bf = jnp.bfloat16
    cos_a, sin_a, cos_b, sin_b = _rope_tables(S)

    tm_proj = 256
    tq = 256
    tm_ffn = 512
    assert tq == tm_proj
    assert S % tq == 0 and S % tm_ffn == 0

    for l in range(depth):
        wi = w_in[l]
        o_kv = Q_RANK
        o_kpe = o_kv + KV_RANK
        o_dq = o_kpe + MLA_ROPE
        kpe_cols = jnp.pad(wi[:, o_kpe:o_dq], ((0, 0), (MLA_NOPE, LANES - MLA_QK)))
        w_in_p = jnp.concatenate([wi[:, :o_kpe], kpe_cols, wi[:, o_dq:]], axis=1).astype(bf)
        w_q_p = _pad_heads(w_q_up[l], MLA_HEADS, MLA_QK).astype(bf)
        wkv = w_kv_up[l].reshape(KV_RANK, MLA_HEADS, MLA_NOPE + MLA_V)
        w_k_p = _pad_heads(wkv[:, :, :MLA_NOPE].reshape(KV_RANK, -1), MLA_HEADS, MLA_NOPE).astype(bf)
        w_v_p = wkv[:, :, MLA_NOPE:].reshape(KV_RANK, MLA_HEADS * MLA_V).astype(bf)
        g_q = _pad_lanes(mla_q_norm_g[l])
        g_k = _pad_lanes(mla_k_norm_g[l])
        g_dq = jnp.tile(diff_q_norm_g[l], 2).reshape(1, LANES)
        g_dk = jnp.tile(diff_k_norm_g[l], 2).reshape(1, LANES)

        n_tok_tiles = S // tm_proj
        tok3 = lambda b, s: (b, s, 0)
        feat3 = lambda b, s: (b, 0, s)
        blk4 = lambda b, s: (b, s, 0, 0)
        rope_spec = pl.BlockSpec((tm_proj, LANES), lambda b, s: (s, 0))
        qt, k_a, vt, dqt, dk, dvt = pl.pallas_call(
            _proj_kernel,
            grid=(B, n_tok_tiles),
            in_specs=[
                pl.BlockSpec((1, tm_proj, D_MODEL), tok3),
                _const_spec((1, D_MODEL)),
                _const_spec((D_MODEL, IN_COLS_PADDED)),
                _const_spec((1, Q_RANK)),
                _const_spec((Q_RANK, MLA_HEADS * LANES)),
                _const_spec((1, KV_RANK)),
                _const_spec((KV_RANK, MLA_HEADS * LANES)),
                _const_spec((KV_RANK, MLA_HEADS * MLA_V)),
                _const_spec((1, LANES)), _const_spec((1, LANES)),
                _const_spec((1, LANES)), _const_spec((1, LANES)),
                rope_spec, rope_spec, rope_spec, rope_spec,
            ],
            out_specs=[
                pl.BlockSpec((1, MLA_HEADS * LANES, tm_proj), feat3),
                pl.BlockSpec((1, tm_proj, MLA_HEADS * LANES), tok3),
                pl.BlockSpec((1, 1, MLA_HEADS * MLA_V, tm_proj), blk4),
                pl.BlockSpec((1, DIFF_HEADS * 2 * LANES, tm_proj), feat3),
                pl.BlockSpec((1, tm_proj, DIFF_HEADS * LANES), tok3),
                pl.BlockSpec((1, 1, DIFF_HEADS * DIFF_V, tm_proj), blk4),
            ],
            out_shape=[
                jax.ShapeDtypeStruct((B, MLA_HEADS * LANES, S), bf),
                jax.ShapeDtypeStruct((B, S, MLA_HEADS * LANES), bf),
                jax.ShapeDtypeStruct((B, n_tok_tiles, MLA_HEADS * MLA_V, tm_proj), bf),
                jax.ShapeDtypeStruct((B, DIFF_HEADS * 2 * LANES, S), bf),
                jax.ShapeDtypeStruct((B, S, DIFF_HEADS * LANES), bf),
                jax.ShapeDtypeStruct((B, n_tok_tiles, DIFF_HEADS * DIFF_V, tm_proj), bf),
            ],
            compiler_params=pltpu.CompilerParams(
                dimension_semantics=("arbitrary", "arbitrary"), vmem_limit_bytes=VMEM_LIMIT),
            name="proj",
        )(x, attn_norm_g[l].reshape(1, -1), w_in_p, q_a_norm_g[l].reshape(1, -1), w_q_p,
          kv_a_norm_g[l].reshape(1, -1), w_k_p, w_v_p, g_q, g_k, g_dq, g_dk,
          cos_a, sin_a, cos_b, sin_b)

        nq = S // tq
        o_a = pl.pallas_call(
            functools.partial(_mla_kernel, tq=tq),
            grid=(B, nq),
            in_specs=[
                pl.BlockSpec((1, MLA_HEADS * LANES, tq), lambda b, i: (b, 0, i)),
                pl.BlockSpec((1, S, MLA_HEADS * LANES), lambda b, i: (b, 0, 0)),
                pl.BlockSpec((1, nq, MLA_HEADS * MLA_V, tq), lambda b, i: (b, 0, 0, 0)),
            ],
            out_specs=pl.BlockSpec((1, MLA_HEADS * MLA_V, tq), lambda b, i: (b, 0, i)),
            out_shape=jax.ShapeDtypeStruct((B, MLA_HEADS * MLA_V, S), bf),
            scratch_shapes=[pltpu.VMEM((MLA_HEADS, 1, tq), jnp.float32),
                            pltpu.VMEM((MLA_HEADS, 1, tq), jnp.float32),
                            pltpu.VMEM((MLA_HEADS * MLA_V, tq), jnp.float32)],
            compiler_params=pltpu.CompilerParams(
                dimension_semantics=("arbitrary", "arbitrary"), vmem_limit_bytes=VMEM_LIMIT),
            name="mla_attn",
        )(qt, k_a, vt)

        lam_init = 0.8 - 0.6 * math.exp(-0.3 * l)
        lam_spec = _const_spec((1, DIFF_D))
        o_b = pl.pallas_call(
            functools.partial(_diff_kernel, tq=tq, lam_init=lam_init),
            grid=(B, nq),
            in_specs=[
                lam_spec, lam_spec, lam_spec, lam_spec,
                _const_spec((DIFF_V, 1)),
                pl.BlockSpec((1, DIFF_HEADS * 2 * LANES, tq), lambda b, i: (b, 0, i)),
                pl.BlockSpec((1, S, DIFF_HEADS * LANES), lambda b, i: (b, 0, 0)),
                pl.BlockSpec((1, nq, DIFF_HEADS * DIFF_V, tq), lambda b, i: (b, 0, 0, 0)),
            ],
            out_specs=pl.BlockSpec((1, DIFF_HEADS * DIFF_V, tq), lambda b, i: (b, 0, i)),
            out_shape=jax.ShapeDtypeStruct((B, DIFF_HEADS * DIFF_V, S), bf),
            scratch_shapes=[pltpu.VMEM((2 * DIFF_HEADS, 1, tq), jnp.float32),
                            pltpu.VMEM((2 * DIFF_HEADS, 1, tq), jnp.float32),
                            pltpu.VMEM((2 * DIFF_HEADS * DIFF_V, tq), jnp.float32)],
            compiler_params=pltpu.CompilerParams(
                dimension_semantics=("arbitrary", "arbitrary"), vmem_limit_bytes=VMEM_LIMIT),
            name="diff_attn",
        )(lambda_q1[l].reshape(1, -1), lambda_k1[l].reshape(1, -1),
          lambda_q2[l].reshape(1, -1), lambda_k2[l].reshape(1, -1),
          diff_subln_g[l].reshape(-1, 1), dqt, dk, dvt)

        n_a = MLA_HEADS * MLA_V
        x = pl.pallas_call(
            _ffn_kernel,
            grid=(B, S // tm_ffn),
            in_specs=[
                pl.BlockSpec((1, tm_ffn, D_MODEL), lambda b, s: (b, s, 0)),
                pl.BlockSpec((1, n_a, tm_ffn), lambda b, s: (b, 0, s)),
                pl.BlockSpec((1, D_MODEL - n_a, tm_ffn), lambda b, s: (b, 0, s)),
                _const_spec((n_a, D_MODEL)),
                _const_spec((D_MODEL - n_a, D_MODEL)),
                _const_spec((1, D_MODEL)),
                _const_spec((D_MODEL, D_FF)),
                _const_spec((D_MODEL, D_FF)),
                _const_spec((CONV_WIDTH, D_FF)),
                _const_spec((1, D_FF)),
                _const_spec((D_FF, D_MODEL)),
            ],
            out_specs=pl.BlockSpec((1, tm_ffn, D_MODEL), lambda b, s: (b, s, 0)),
            out_shape=jax.ShapeDtypeStruct((B, S, D_MODEL), x.dtype),
            scratch_shapes=[pltpu.VMEM((8, D_FF), jnp.float32),
                            pltpu.VMEM((tm_ffn, D_FF), bf)],
            compiler_params=pltpu.CompilerParams(
                dimension_semantics=("arbitrary", "arbitrary"), vmem_limit_bytes=VMEM_LIMIT),
            name="ffn",
        )(x, o_a, o_b, w_out[l][:n_a].astype(bf), w_out[l][n_a:].astype(bf),
          ffn_norm_g[l].reshape(1, -1), w_gate[l].astype(bf), w_up[l].astype(bf),
          conv_w[l], conv_b[l].reshape(1, -1), w_down[l].astype(bf))
    return x
```

```python
import functools
import math

import jax
import jax.numpy as jnp
from jax import lax
from jax.experimental import pallas as pl
from jax.experimental.pallas import tpu as pltpu

D_MODEL = 1024
MLA_HEADS = 8
MLA_NOPE = 64
MLA_ROPE = 32
MLA_V = 64
MLA_QK = MLA_NOPE + MLA_ROPE
Q_RANK = 384
KV_RANK = 256
DIFF_HEADS = 4
DIFF_D = 64
DIFF_V = 2 * DIFF_D
D_FF = 2816
CONV_WIDTH = 3
ROPE_THETA = 10000.0
EPS = 1e-6
LANES = 128
SUM_ROWS = 16
LOG2E = math.log2(math.e)

C_Q = 0
C_KV = C_Q + Q_RANK
C_KPE = C_KV + KV_RANK
C_DQ = C_KPE + LANES
C_DK = C_DQ + DIFF_HEADS * LANES
C_DV = C_DK + DIFF_HEADS * LANES
IN_COLS_PADDED = C_DV + DIFF_HEADS * DIFF_V

VMEM_LIMIT = 56 * 1024 * 1024


def _bf16_dot(a, b):
    return jnp.dot(a.astype(jnp.bfloat16), b.astype(jnp.bfloat16),
                   preferred_element_type=jnp.float32)


def _rms_rows(x, g):
    ms = jnp.mean(x * x, axis=-1, keepdims=True)
    return x * lax.rsqrt(ms + EPS) * g


def _proj_kernel(x_ref, g_attn_ref, w_in_ref, g_qa_ref, w_q_ref, g_kva_ref, w_k_ref, w_v_ref,
                 g_q_ref, g_k_ref, g_dq_ref, g_dk_ref,
                 cos_a_ref, sin_a_ref, cos_b_ref, sin_b_ref,
                 qt_ref, k_ref, vt_ref, dqt_ref, dk_ref, dvt_ref):
    tm = x_ref.shape[1]
    lane = lax.broadcasted_iota(jnp.int32, (tm, LANES), 1)
    nope_mask = lane < MLA_NOPE

    h = _rms_rows(x_ref[0], g_attn_ref[...])
    proj = _bf16_dot(h, w_in_ref[...])

    cos_a = cos_a_ref[...]
    sin_a = sin_a_ref[...]
    cos_b = cos_b_ref[...]
    sin_b = sin_b_ref[...]

    def rope_a(xn):
        rot = jnp.where(lane < MLA_NOPE + MLA_ROPE // 2,
                        pltpu.roll(xn, LANES - MLA_ROPE // 2, axis=1),
                        pltpu.roll(xn, MLA_ROPE // 2, axis=1))
        return xn * cos_a + rot * sin_a

    def rope_b(xn):
        rot = jnp.where((lane & (DIFF_D - 1)) < DIFF_D // 2,
                        pltpu.roll(xn, LANES - DIFF_D // 2, axis=1),
                        pltpu.roll(xn, DIFF_D // 2, axis=1))
        return xn * cos_b + rot * sin_b

    def split_norm(xg, n_lo, n_hi, g):
        sq = xg * xg
        ss_lo = jnp.sum(jnp.where(nope_mask, sq, 0.0), axis=-1, keepdims=True)
        ss_hi = jnp.sum(jnp.where(nope_mask, 0.0, sq), axis=-1, keepdims=True)
        r_lo = lax.rsqrt(ss_lo * (1.0 / n_lo) + EPS)
        r_hi = lax.rsqrt(ss_hi * (1.0 / n_hi) + EPS)
        return xg * jnp.where(nope_mask, r_lo, r_hi) * g

    cq = _rms_rows(proj[:, C_Q:C_Q + Q_RANK], g_qa_ref[...])
    q = _bf16_dot(cq, w_q_ref[...])
    g_q = g_q_ref[...] * (MLA_QK ** -0.5 * LOG2E)
    for hd in range(MLA_HEADS):
        qh = split_norm(q[:, hd * LANES:(hd + 1) * LANES], MLA_NOPE, MLA_ROPE, g_q)
        qh = rope_a(qh)
        qt_ref[0, hd * LANES:(hd + 1) * LANES, :] = qh.T.astype(qt_ref.dtype)

    ckv = _rms_rows(proj[:, C_KV:C_KV + KV_RANK], g_kva_ref[...])
    g_k = g_k_ref[...]
    kpe = split_norm(proj[:, C_KPE:C_KPE + LANES], MLA_NOPE, MLA_ROPE, g_k)
    kpe = rope_a(kpe)
    kn = _bf16_dot(ckv, w_k_ref[...])
    for hd in range(MLA_HEADS):
        kh = kn[:, hd * LANES:(hd + 1) * LANES]
        ss = jnp.sum(kh * kh, axis=-1, keepdims=True)
        kh = kh * lax.rsqrt(ss * (1.0 / MLA_NOPE) + EPS) * g_k
        k_ref[0, :, hd * LANES:(hd + 1) * LANES] = jnp.where(nope_mask, kh, kpe).astype(k_ref.dtype)
    v = _bf16_dot(ckv, w_v_ref[...])
    for pr in range(MLA_HEADS * MLA_V // LANES):
        vt_ref[0, 0, pr * LANES:(pr + 1) * LANES, :] = (
            v[:, pr * LANES:(pr + 1) * LANES].T.astype(vt_ref.dtype))

    g_dq = g_dq_ref[...] * (DIFF_D ** -0.5 * LOG2E)
    g_dk = g_dk_ref[...]
    row = lax.broadcasted_iota(jnp.int32, (LANES, tm), 0)
    for hd in range(DIFF_HEADS):
        dq = split_norm(proj[:, C_DQ + hd * LANES:C_DQ + (hd + 1) * LANES], DIFF_D, DIFF_D, g_dq)
        dqt = rope_b(dq).T
        dqt_ref[0, (2 * hd) * LANES:(2 * hd + 1) * LANES, :] = (
            jnp.where(row < DIFF_D, dqt, 0.0).astype(dqt_ref.dtype))
        dqt_ref[0, (2 * hd + 1) * LANES:(2 * hd + 2) * LANES, :] = (
            jnp.where(row < DIFF_D, 0.0, dqt).astype(dqt_ref.dtype))
        dk = split_norm(proj[:, C_DK + hd * LANES:C_DK + (hd + 1) * LANES], DIFF_D, DIFF_D, g_dk)
        dk_ref[0, :, hd * LANES:(hd + 1) * LANES] = rope_b(dk).astype(dk_ref.dtype)
        dvt_ref[0, 0, hd * LANES:(hd + 1) * LANES, :] = (
            proj[:, C_DV + hd * LANES:C_DV + (hd + 1) * LANES].T.astype(dvt_ref.dtype))


def _causal_mask(s_t, tk, tq):
    key = lax.broadcasted_iota(jnp.int32, (tk, tq), 0)
    qry = lax.broadcasted_iota(jnp.int32, (tk, tq), 1)
    return jnp.where(key <= qry, s_t, -jnp.inf)


def _flash_chain(s_t, vt, m_ref, acc_ref, c, rows):
    m_old = m_ref[c]
    m_new = jnp.maximum(m_old, jnp.max(s_t, axis=0, keepdims=True))
    alpha = jnp.exp2(m_old - m_new)
    p = jnp.exp2(s_t - m_new).astype(jnp.bfloat16)
    m_ref[c] = m_new
    vt_ones = jnp.concatenate([vt, jnp.ones((SUM_ROWS, vt.shape[1]), vt.dtype)], axis=0)
    acc_ref[rows, :] = alpha * acc_ref[rows, :] + jnp.dot(
        vt_ones, p, preferred_element_type=jnp.float32)


def _init_stats(m_ref, acc_ref):
    m_ref[...] = jnp.full(m_ref.shape, -jnp.inf, m_ref.dtype)
    acc_ref[...] = jnp.zeros_like(acc_ref)


def _sweep_key_tiles(tile, qi):
    def body(j, carry):
        tile(j, False)
        return carry

    lax.fori_loop(0, qi, body, 0)
    tile(qi, True)


def _mla_kernel(qt_ref, k_ref, vt_ref, o_ref, m_ref, acc_ref, *, tq):
    qi = pl.program_id(1)
    _init_stats(m_ref, acc_ref)
    acc_rows = MLA_V + SUM_ROWS

    def tile(j, masked):
        def scores(hd):
            s_t = jnp.dot(k_ref[0, j, :, hd * LANES:(hd + 1) * LANES],
                          qt_ref[0, hd * LANES:(hd + 1) * LANES, :],
                          preferred_element_type=jnp.float32)
            return _causal_mask(s_t, tq, tq) if masked else s_t

        s_all = [scores(hd) for hd in range(MLA_HEADS)]
        for hd in range(MLA_HEADS):
            _flash_chain(s_all[hd], vt_ref[0, j, hd * MLA_V:(hd + 1) * MLA_V, :], m_ref, acc_ref, hd,
                         slice(hd * acc_rows, (hd + 1) * acc_rows))

    _sweep_key_tiles(tile, qi)
    for hd in range(MLA_HEADS):
        a0 = hd * acc_rows
        o_ref[0, hd * MLA_V:(hd + 1) * MLA_V, :] = (
            acc_ref[a0:a0 + MLA_V, :] / acc_ref[a0 + MLA_V:a0 + MLA_V + 1, :]).astype(o_ref.dtype)


def _diff_kernel(lq1_ref, lk1_ref, lq2_ref, lk2_ref, g_sub_ref, dqt_ref, dk_ref, dvt_ref, o_ref,
                 m_ref, acc_ref, *, tq, lam_init):
    qi = pl.program_id(1)
    _init_stats(m_ref, acc_ref)
    acc_rows = DIFF_V + SUM_ROWS
    lam = (jnp.exp(jnp.sum(lq1_ref[...] * lk1_ref[...], axis=-1, keepdims=True))
           - jnp.exp(jnp.sum(lq2_ref[...] * lk2_ref[...], axis=-1, keepdims=True))
           + lam_init)

    def tile(j, masked):
        def scores(c):
            hd = c // 2
            s_t = jnp.dot(dk_ref[0, j, :, hd * LANES:(hd + 1) * LANES],
                          dqt_ref[0, c * LANES:(c + 1) * LANES, :],
                          preferred_element_type=jnp.float32)
            return _causal_mask(s_t, tq, tq) if masked else s_t

        s_all = [scores(c) for c in range(2 * DIFF_HEADS)]
        for c in range(2 * DIFF_HEADS):
            hd = c // 2
            vt = dvt_ref[0, j, hd * DIFF_V:(hd + 1) * DIFF_V, :]
            _flash_chain(s_all[c], vt, m_ref, acc_ref, c, slice(c * acc_rows, (c + 1) * acc_rows))

    def normalised(c):
        a0 = c * acc_rows
        return acc_ref[a0:a0 + DIFF_V, :] / acc_ref[a0 + DIFF_V:a0 + DIFF_V + 1, :]

    _sweep_key_tiles(tile, qi)
    for hd in range(DIFF_HEADS):
        o = normalised(2 * hd) - lam * normalised(2 * hd + 1)
        ms = jnp.mean(o * o, axis=0, keepdims=True)
        o = o * lax.rsqrt(ms + EPS) * g_sub_ref[...] * (1.0 - lam_init)
        o_ref[0, hd * DIFF_V:(hd + 1) * DIFF_V, :] = o.astype(o_ref.dtype)


FF_CHUNK = 512


def _ffn_kernel(x_ref, oa_ref, ob_ref, w_oa_ref, w_ob_ref, g_ffn_ref, w_gate_ref, w_up_ref,
                conv_w_ref, conv_b_ref, w_down_ref, out_ref, prev_ref, y_ref):
    si = pl.program_id(1)
    tm = x_ref.shape[1]
    contract0 = (((0,), (0,)), ((), ()))
    mix = (lax.dot_general(oa_ref[0], w_oa_ref[...], contract0, preferred_element_type=jnp.float32)
           + lax.dot_general(ob_ref[0], w_ob_ref[...], contract0,
                             preferred_element_type=jnp.float32))
    x1 = x_ref[0] + mix
    h = _rms_rows(x1, g_ffn_ref[...]).astype(jnp.bfloat16)

    @pl.when(si == 0)
    def _():
        prev_ref[...] = jnp.zeros_like(prev_ref)

    for c0 in range(0, D_FF, FF_CHUNK):
        cw = min(FF_CHUNK, D_FF - c0)
        g = jnp.dot(h, w_gate_ref[:, c0:c0 + cw], preferred_element_type=jnp.float32)
        u = jnp.dot(h, w_up_ref[:, c0:c0 + cw], preferred_element_type=jnp.float32)
        row = lax.broadcasted_iota(jnp.int32, (tm, cw), 0)
        p1 = prev_ref[7:8, c0:c0 + cw]
        p2 = prev_ref[6:7, c0:c0 + cw]
        g1 = jnp.where(row == 0, p1, pltpu.roll(g, 1, axis=0))
        g2 = jnp.where(row == 0, p2, jnp.where(row == 1, p1, pltpu.roll(g, 2, axis=0)))
        prev_ref[:, c0:c0 + cw] = g[tm - 8:tm, :]
        cg = (conv_b_ref[:, c0:c0 + cw] + g2 * conv_w_ref[0:1, c0:c0 + cw]
              + g1 * conv_w_ref[1:2, c0:c0 + cw] + g * conv_w_ref[2:3, c0:c0 + cw])
        y_ref[:, c0:c0 + cw] = (jax.nn.silu(cg) * u).astype(y_ref.dtype)

    out_ref[0] = x1 + jnp.dot(y_ref[...], w_down_ref[...], preferred_element_type=jnp.float32)


def _rope_tables(seq):
    pos = jnp.arange(seq, dtype=jnp.float32)[:, None]

    def tables(dim):
        inv = 1.0 / (ROPE_THETA ** (jnp.arange(0, dim, 2, dtype=jnp.float32) / dim))
        ang = pos * inv[None, :]
        return jnp.cos(ang), jnp.sin(ang)

    ca, sa = tables(MLA_ROPE)
    cb, sb = tables(DIFF_D)
    ones = jnp.ones((seq, MLA_NOPE), jnp.float32)
    zeros_n = jnp.zeros((seq, MLA_NOPE), jnp.float32)
    zeros_p = jnp.zeros((seq, LANES - MLA_QK), jnp.float32)
    cos_a = jnp.concatenate([ones, ca, ca, zeros_p], axis=1)
    sin_a = jnp.concatenate([zeros_n, -sa, sa, zeros_p], axis=1)
    cos_b = jnp.concatenate([cb, cb, cb, cb], axis=1)
    sin_b = jnp.concatenate([-sb, sb, -sb, sb], axis=1)
    return cos_a, sin_a, cos_b, sin_b


def _pad_heads(w, heads, width):
    r = w.shape[0]
    w = w.reshape(r, heads, width)
    return jnp.pad(w, ((0, 0), (0, 0), (0, LANES - width))).reshape(r, heads * LANES)


def _pad_lanes(v, offset=0):
    return jnp.pad(v, (offset, LANES - offset - v.shape[0])).reshape(1, LANES)


def _const_spec(shape):
    return pl.BlockSpec(shape, lambda *_: (0,) * len(shape))


def kernel(x, attn_norm_g, w_in, q_a_norm_g, w_q_up, kv_a_norm_g, w_kv_up, mla_q_norm_g,
           mla_k_norm_g, diff_q_norm_g, diff_k_norm_g, lambda_q1, lambda_k1, lambda_q2, lambda_k2,
           diff_subln_g, w_out, ffn_norm_g, w_gate, w_up, conv_w, conv_b, w_down):
    B, S, _ = x.shape
    depth = w_in.shape[0]
    bf = jnp.bfloat16
    cos_a, sin_a, cos_b, sin_b = _rope_tables(S)

    tm_proj = 256
    tq = 256
    tm_ffn = 512
    assert tq == tm_proj
    assert S % tq == 0 and S % tm_ffn == 0

    for l in range(depth):
        wi = w_in[l]
        o_kv = Q_RANK
        o_kpe = o_kv + KV_RANK
        o_dq = o_kpe + MLA_ROPE
        kpe_cols = jnp.pad(wi[:, o_kpe:o_dq], ((0, 0), (MLA_NOPE, LANES - MLA_QK)))
        w_in_p = jnp.concatenate([wi[:, :o_kpe], kpe_cols, wi[:, o_dq:]], axis=1).astype(bf)
        w_q_p = _pad_heads(w_q_up[l], MLA_HEADS, MLA_QK).astype(bf)
        wkv = w_kv_up[l].reshape(KV_RANK, MLA_HEADS, MLA_NOPE + MLA_V)
        w_k_p = _pad_heads(wkv[:, :, :MLA_NOPE].reshape(KV_RANK, -1), MLA_HEADS, MLA_NOPE).astype(bf)
        w_v_p = wkv[:, :, MLA_NOPE:].reshape(KV_RANK, MLA_HEADS * MLA_V).astype(bf)
        g_q = _pad_lanes(mla_q_norm_g[l])
        g_k = _pad_lanes(mla_k_norm_g[l])
        g_dq = jnp.tile(diff_q_norm_g[l], 2).reshape(1, LANES)
        g_dk = jnp.tile(diff_k_norm_g[l], 2).reshape(1, LANES)

        n_tok_tiles = S // tm_proj
        tok3 = lambda b, s: (b, s, 0)
        feat3 = lambda b, s: (b, 0, s)
        blk4 = lambda b, s: (b, s, 0, 0)
        rope_spec = pl.BlockSpec((tm_proj, LANES), lambda b, s: (s, 0))
        qt, k_a, vt, dqt, dk, dvt = pl.pallas_call(
            _proj_kernel,
            grid=(B, n_tok_tiles),
            in_specs=[
                pl.BlockSpec((1, tm_proj, D_MODEL), tok3),
                _const_spec((1, D_MODEL)),
                _const_spec((D_MODEL, IN_COLS_PADDED)),
                _const_spec((1, Q_RANK)),
                _const_spec((Q_RANK, MLA_HEADS * LANES)),
                _const_spec((1, KV_RANK)),
                _const_spec((KV_RANK, MLA_HEADS * LANES)),
                _const_spec((KV_RANK, MLA_HEADS * MLA_V)),
                _const_spec((1, LANES)), _const_spec((1, LANES)),
                _const_spec((1, LANES)), _const_spec((1, LANES)),
                rope_spec, rope_spec, rope_spec, rope_spec,
            ],
            out_specs=[
                pl.BlockSpec((1, MLA_HEADS * LANES, tm_proj), feat3),
                pl.BlockSpec((1, tm_proj, MLA_HEADS * LANES), tok3),
                pl.BlockSpec((1, 1, MLA_HEADS * MLA_V, tm_proj), blk4),
                pl.BlockSpec((1, DIFF_HEADS * 2 * LANES, tm_proj), feat3),
                pl.BlockSpec((1, tm_proj, DIFF_HEADS * LANES), tok3),
                pl.BlockSpec((1, 1, DIFF_HEADS * DIFF_V, tm_proj), blk4),
            ],
            out_shape=[
                jax.ShapeDtypeStruct((B, MLA_HEADS * LANES, S), bf),
                jax.ShapeDtypeStruct((B, S, MLA_HEADS * LANES), bf),
                jax.ShapeDtypeStruct((B, n_tok_tiles, MLA_HEADS * MLA_V, tm_proj), bf),
                jax.ShapeDtypeStruct((B, DIFF_HEADS * 2 * LANES, S), bf),
                jax.ShapeDtypeStruct((B, S, DIFF_HEADS * LANES), bf),
                jax.ShapeDtypeStruct((B, n_tok_tiles, DIFF_HEADS * DIFF_V, tm_proj), bf),
            ],
            compiler_params=pltpu.CompilerParams(
                dimension_semantics=("arbitrary", "arbitrary"), vmem_limit_bytes=VMEM_LIMIT),
            name="proj",
        )(x, attn_norm_g[l].reshape(1, -1), w_in_p, q_a_norm_g[l].reshape(1, -1), w_q_p,
          kv_a_norm_g[l].reshape(1, -1), w_k_p, w_v_p, g_q, g_k, g_dq, g_dk,
          cos_a, sin_a, cos_b, sin_b)

        nq = S // tq
        o_a = pl.pallas_call(
            functools.partial(_mla_kernel, tq=tq),
            grid=(B, nq),
            in_specs=[
                pl.BlockSpec((1, MLA_HEADS * LANES, tq), lambda b, i: (b, 0, i)),
                pl.BlockSpec((1, nq, tq, MLA_HEADS * LANES), lambda b, i: (b, 0, 0, 0)),
                pl.BlockSpec((1, nq, MLA_HEADS * MLA_V, tq), lambda b, i: (b, 0, 0, 0)),
            ],
            out_specs=pl.BlockSpec((1, MLA_HEADS * MLA_V, tq), lambda b, i: (b, 0, i)),
            out_shape=jax.ShapeDtypeStruct((B, MLA_HEADS * MLA_V, S), bf),
            scratch_shapes=[pltpu.VMEM((MLA_HEADS, 1, tq), jnp.float32),
                            pltpu.VMEM((MLA_HEADS * (MLA_V + SUM_ROWS), tq), jnp.float32)],
            compiler_params=pltpu.CompilerParams(
                dimension_semantics=("arbitrary", "arbitrary"), vmem_limit_bytes=VMEM_LIMIT),
            name="mla_attn",
        )(qt, k_a.reshape(B, nq, tq, MLA_HEADS * LANES), vt)

        lam_init = 0.8 - 0.6 * math.exp(-0.3 * l)
        lam_spec = _const_spec((1, DIFF_D))
        o_b = pl.pallas_call(
            functools.partial(_diff_kernel, tq=tq, lam_init=lam_init),
            grid=(B, nq),
            in_specs=[
                lam_spec, lam_spec, lam_spec, lam_spec,
                _const_spec((DIFF_V, 1)),
                pl.BlockSpec((1, DIFF_HEADS * 2 * LANES, tq), lambda b, i: (b, 0, i)),
                pl.BlockSpec((1, nq, tq, DIFF_HEADS * LANES), lambda b, i: (b, 0, 0, 0)),
                pl.BlockSpec((1, nq, DIFF_HEADS * DIFF_V, tq), lambda b, i: (b, 0, 0, 0)),
            ],
            out_specs=pl.BlockSpec((1, DIFF_HEADS * DIFF_V, tq), lambda b, i: (b, 0, i)),
            out_shape=jax.ShapeDtypeStruct((B, DIFF_HEADS * DIFF_V, S), bf),
            scratch_shapes=[pltpu.VMEM((2 * DIFF_HEADS, 1, tq), jnp.float32),
                            pltpu.VMEM((2 * DIFF_HEADS * (DIFF_V + SUM_ROWS), tq), jnp.float32)],
            compiler_params=pltpu.CompilerParams(
                dimension_semantics=("arbitrary", "arbitrary"), vmem_limit_bytes=VMEM_LIMIT),
            name="diff_attn",
        )(lambda_q1[l].reshape(1, -1), lambda_k1[l].reshape(1, -1),
          lambda_q2[l].reshape(1, -1), lambda_k2[l].reshape(1, -1),
          diff_subln_g[l].reshape(-1, 1), dqt, dk.reshape(B, nq, tq, DIFF_HEADS * LANES), dvt)

        n_a = MLA_HEADS * MLA_V
        x = pl.pallas_call(
            _ffn_kernel,
            grid=(B, S // tm_ffn),
            in_specs=[
                pl.BlockSpec((1, tm_ffn, D_MODEL), lambda b, s: (b, s, 0)),
                pl.BlockSpec((1, n_a, tm_ffn), lambda b, s: (b, 0, s)),
                pl.BlockSpec((1, D_MODEL - n_a, tm_ffn), lambda b, s: (b, 0, s)),
                _const_spec((n_a, D_MODEL)),
                _const_spec((D_MODEL - n_a, D_MODEL)),
                _const_spec((1, D_MODEL)),
                _const_spec((D_MODEL, D_FF)),
                _const_spec((D_MODEL, D_FF)),
                _const_spec((CONV_WIDTH, D_FF)),
                _const_spec((1, D_FF)),
                _const_spec((D_FF, D_MODEL)),
            ],
            out_specs=pl.BlockSpec((1, tm_ffn, D_MODEL), lambda b, s: (b, s, 0)),
            out_shape=jax.ShapeDtypeStruct((B, S, D_MODEL), x.dtype),
            scratch_shapes=[pltpu.VMEM((8, D_FF), jnp.float32),
                            pltpu.VMEM((tm_ffn, D_FF), bf)],
            compiler_params=pltpu.CompilerParams(
                dimension_semantics=("arbitrary", "arbitrary"), vmem_limit_bytes=VMEM_LIMIT),
            name="ffn",
        )(x, o_a, o_b, w_out[l][:n_a].astype(bf), w_out[l][n_a:].astype(bf),
          ffn_norm_g[l].reshape(1, -1), w_gate[l].astype(bf), w_up[l].astype(bf),
          conv_w[l], conv_b[l].reshape(1, -1), w_down[l].astype(bf))
    return x
```

```python
import functools
import math

import jax
import jax.numpy as jnp
from jax import lax
from jax.experimental import pallas as pl
from jax.experimental.pallas import tpu as pltpu

D_MODEL = 1024
MLA_HEADS = 8
MLA_NOPE = 64
MLA_ROPE = 32
MLA_V = 64
MLA_QK = MLA_NOPE + MLA_ROPE
Q_RANK = 384
KV_RANK = 256
DIFF_HEADS = 4
DIFF_D = 64
DIFF_V = 2 * DIFF_D
D_FF = 2816
CONV_WIDTH = 3
ROPE_THETA = 10000.0
EPS = 1e-6
LANES = 128
SUM_ROWS = 16
LOG2E = math.log2(math.e)

R_Q = 0
R_KV = R_Q + Q_RANK
R_KPE = R_KV + KV_RANK
R_DQ = R_KPE + MLA_ROPE
R_DK = R_DQ + DIFF_HEADS * 2 * DIFF_D
R_DV = R_DK + DIFF_HEADS * 2 * DIFF_D
IN_COLS = R_DV + DIFF_HEADS * DIFF_V

VMEM_LIMIT = 56 * 1024 * 1024


def _rms_rows(x, g):
    ms = jnp.mean(x * x, axis=-1, keepdims=True)
    return x * lax.rsqrt(ms + EPS) * g


def _rms_cols(xt, n):
    return lax.rsqrt(jnp.sum(xt * xt, axis=0, keepdims=True) * (1.0 / n) + EPS)


def _rope_cols(xt, cos, sin):
    half = xt.shape[0] // 2
    x1, x2 = xt[:half], xt[half:]
    return x1 * cos - x2 * sin, x2 * cos + x1 * sin


def _proj_kernel(x_ref, g_attn_ref, w_in_ref, g_qa_ref, w_q_ref, g_kva_ref, w_k_ref, w_v_ref,
                 g_q_ref, g_k_ref, g_dq_ref, g_dk_ref,
                 cos_a_ref, sin_a_ref, cos_b_ref, sin_b_ref,
                 qt_ref, k_ref, vt_ref, dqt_ref, dk_ref, dvt_ref):
    tm = x_ref.shape[1]
    bf = jnp.bfloat16
    h = _rms_rows(x_ref[0], g_attn_ref[...]).astype(bf)
    proj = lax.dot_general(w_in_ref[...], h, (((1,), (1,)), ((), ())),
                           preferred_element_type=jnp.float32)
    cos_a, sin_a = cos_a_ref[...], sin_a_ref[...]
    cos_b, sin_b = cos_b_ref[...], sin_b_ref[...]
    zeros_pad = jnp.zeros((LANES - MLA_QK, tm), jnp.float32)

    def head_a(nope, pe, g):
        nope = nope * _rms_cols(nope, MLA_NOPE) * g[:MLA_NOPE]
        pe = pe * _rms_cols(pe, MLA_ROPE) * g[MLA_NOPE:]
        p1, p2 = _rope_cols(pe, cos_a, sin_a)
        return jnp.concatenate([nope, p1, p2, zeros_pad], axis=0)

    def head_b(xt, g):
        xt = xt * _rms_cols(xt, DIFF_D) * g
        return jnp.concatenate(_rope_cols(xt, cos_b, sin_b), axis=0)

    cq = proj[R_Q:R_Q + Q_RANK]
    cq = (cq * _rms_cols(cq, Q_RANK) * g_qa_ref[...]).astype(bf)
    q = jnp.dot(w_q_ref[...], cq, preferred_element_type=jnp.float32)
    g_q = g_q_ref[...] * (MLA_QK ** -0.5 * LOG2E)
    for hd in range(MLA_HEADS):
        r0 = hd * MLA_QK
        qt_ref[0, hd * LANES:(hd + 1) * LANES, :] = head_a(
            q[r0:r0 + MLA_NOPE], q[r0 + MLA_NOPE:r0 + MLA_QK], g_q).astype(bf)

    ckv = proj[R_KV:R_KV + KV_RANK]
    ckv = (ckv * _rms_cols(ckv, KV_RANK) * g_kva_ref[...]).astype(bf)
    kn = jnp.dot(w_k_ref[...], ckv, preferred_element_type=jnp.float32)
    g_k = g_k_ref[...]
    kpe = proj[R_KPE:R_KPE + MLA_ROPE]
    for hd in range(MLA_HEADS):
        kt = head_a(kn[hd * MLA_NOPE:(hd + 1) * MLA_NOPE], kpe, g_k)
        k_ref[0, :, hd * LANES:(hd + 1) * LANES] = kt.T.astype(bf)
    vt_ref[0, 0] = jnp.dot(w_v_ref[...], ckv, preferred_element_type=jnp.float32).astype(bf)

    g_dq = g_dq_ref[...] * (DIFF_D ** -0.5 * LOG2E)
    g_dk = g_dk_ref[...]
    zeros_half = jnp.zeros((DIFF_D, tm), bf)
    for hd in range(DIFF_HEADS):
        r0 = R_DQ + hd * 2 * DIFF_D
        q1 = head_b(proj[r0:r0 + DIFF_D], g_dq).astype(bf)
        q2 = head_b(proj[r0 + DIFF_D:r0 + 2 * DIFF_D], g_dq).astype(bf)
        b0 = 2 * hd * LANES
        dqt_ref[0, b0:b0 + LANES, :] = jnp.concatenate([q1, zeros_half], axis=0)
        dqt_ref[0, b0 + LANES:b0 + 2 * LANES, :] = jnp.concatenate([zeros_half, q2], axis=0)
        r0 = R_DK + hd * 2 * DIFF_D
        dkt = jnp.concatenate([head_b(proj[r0:r0 + DIFF_D], g_dk),
                               head_b(proj[r0 + DIFF_D:r0 + 2 * DIFF_D], g_dk)], axis=0)
        dk_ref[0, :, hd * LANES:(hd + 1) * LANES] = dkt.T.astype(bf)
    dvt_ref[0, 0] = proj[R_DV:R_DV + DIFF_HEADS * DIFF_V].astype(bf)


def _causal_mask(s_t, tk, tq):
    key = lax.broadcasted_iota(jnp.int32, (tk, tq), 0)
    qry = lax.broadcasted_iota(jnp.int32, (tk, tq), 1)
    return jnp.where(key <= qry, s_t, -jnp.inf)


def _flash_chain(s_t, vt, m_ref, acc_ref, c, rows):
    m_old = m_ref[c]
    m_new = jnp.maximum(m_old, jnp.max(s_t, axis=0, keepdims=True))
    alpha = jnp.exp2(m_old - m_new)
    p = jnp.exp2(s_t - m_new).astype(jnp.bfloat16)
    m_ref[c] = m_new
    vt_ones = jnp.concatenate([vt, jnp.ones((SUM_ROWS, vt.shape[1]), vt.dtype)], axis=0)
    acc_ref[rows, :] = alpha * acc_ref[rows, :] + jnp.dot(
        vt_ones, p, preferred_element_type=jnp.float32)


def _init_stats(m_ref, acc_ref):
    m_ref[...] = jnp.full(m_ref.shape, -jnp.inf, m_ref.dtype)
    acc_ref[...] = jnp.zeros_like(acc_ref)


def _sweep_key_tiles(tile, qi):
    def body(j, carry):
        tile(j, False)
        return carry

    lax.fori_loop(0, qi, body, 0)
    tile(qi, True)


def _mla_kernel(qt_ref, k_ref, vt_ref, o_ref, m_ref, acc_ref, *, tq):
    qi = pl.program_id(1)
    _init_stats(m_ref, acc_ref)
    acc_rows = MLA_V + SUM_ROWS

    def tile(j, masked):
        def scores(hd):
            s_t = jnp.dot(k_ref[0, j, :, hd * LANES:(hd + 1) * LANES],
                          qt_ref[0, hd * LANES:(hd + 1) * LANES, :],
                          preferred_element_type=jnp.float32)
            return _causal_mask(s_t, tq, tq) if masked else s_t

        s_all = [scores(hd) for hd in range(MLA_HEADS)]
        for hd in range(MLA_HEADS):
            _flash_chain(s_all[hd], vt_ref[0, j, hd * MLA_V:(hd + 1) * MLA_V, :], m_ref, acc_ref, hd,
                         slice(hd * acc_rows, (hd + 1) * acc_rows))

    _sweep_key_tiles(tile, qi)
    for hd in range(MLA_HEADS):
        a0 = hd * acc_rows
        o_ref[0, hd * MLA_V:(hd + 1) * MLA_V, :] = (
            acc_ref[a0:a0 + MLA_V, :] / acc_ref[a0 + MLA_V:a0 + MLA_V + 1, :]).astype(o_ref.dtype)


def _diff_kernel(lq1_ref, lk1_ref, lq2_ref, lk2_ref, g_sub_ref, dqt_ref, dk_ref, dvt_ref, o_ref,
                 m_ref, acc_ref, *, tq, lam_init):
    qi = pl.program_id(1)
    _init_stats(m_ref, acc_ref)
    acc_rows = DIFF_V + SUM_ROWS
    lam = (jnp.exp(jnp.sum(lq1_ref[...] * lk1_ref[...], axis=-1, keepdims=True))
           - jnp.exp(jnp.sum(lq2_ref[...] * lk2_ref[...], axis=-1, keepdims=True))
           + lam_init)

    def tile(j, masked):
        def scores(c):
            hd = c // 2
            s_t = jnp.dot(dk_ref[0, j, :, hd * LANES:(hd + 1) * LANES],
                          dqt_ref[0, c * LANES:(c + 1) * LANES, :],
                          preferred_element_type=jnp.float32)
            return _causal_mask(s_t, tq, tq) if masked else s_t

        s_all = [scores(c) for c in range(2 * DIFF_HEADS)]
        for c in range(2 * DIFF_HEADS):
            hd = c // 2
            vt = dvt_ref[0, j, hd * DIFF_V:(hd + 1) * DIFF_V, :]
            _flash_chain(s_all[c], vt, m_ref, acc_ref, c, slice(c * acc_rows, (c + 1) * acc_rows))

    def normalised(c):
        a0 = c * acc_rows
        return acc_ref[a0:a0 + DIFF_V, :] / acc_ref[a0 + DIFF_V:a0 + DIFF_V + 1, :]

    _sweep_key_tiles(tile, qi)
    for hd in range(DIFF_HEADS):
        o = normalised(2 * hd) - lam * normalised(2 * hd + 1)
        ms = jnp.mean(o * o, axis=0, keepdims=True)
        o = o * lax.rsqrt(ms + EPS) * g_sub_ref[...] * (1.0 - lam_init)
        o_ref[0, hd * DIFF_V:(hd + 1) * DIFF_V, :] = o.astype(o_ref.dtype)


FF_CHUNK = 512


def _ffn_kernel(x_ref, oa_ref, ob_ref, w_oa_ref, w_ob_ref, g_ffn_ref, w_gate_ref, w_up_ref,
                conv_w_ref, conv_b_ref, w_down_ref, out_ref, prev_ref, y_ref):
    si = pl.program_id(1)
    tm = x_ref.shape[1]
    contract0 = (((0,), (0,)), ((), ()))
    mix = (lax.dot_general(oa_ref[0], w_oa_ref[...], contract0, preferred_element_type=jnp.float32)
           + lax.dot_general(ob_ref[0], w_ob_ref[...], contract0,
                             preferred_element_type=jnp.float32))
    x1 = x_ref[0] + mix
    h = _rms_rows(x1, g_ffn_ref[...]).astype(jnp.bfloat16)

    @pl.when(si == 0)
    def _():
        prev_ref[...] = jnp.zeros_like(prev_ref)

    for c0 in range(0, D_FF, FF_CHUNK):
        cw = min(FF_CHUNK, D_FF - c0)
        g = jnp.dot(h, w_gate_ref[:, c0:c0 + cw], preferred_element_type=jnp.float32)
        u = jnp.dot(h, w_up_ref[:, c0:c0 + cw], preferred_element_type=jnp.float32)
        row = lax.broadcasted_iota(jnp.int32, (tm, cw), 0)
        p1 = prev_ref[7:8, c0:c0 + cw]
        p2 = prev_ref[6:7, c0:c0 + cw]
        g1 = jnp.where(row == 0, p1, pltpu.roll(g, 1, axis=0))
        g2 = jnp.where(row == 0, p2, jnp.where(row == 1, p1, pltpu.roll(g, 2, axis=0)))
        prev_ref[:, c0:c0 + cw] = g[tm - 8:tm, :]
        cg = (conv_b_ref[:, c0:c0 + cw] + g2 * conv_w_ref[0:1, c0:c0 + cw]
              + g1 * conv_w_ref[1:2, c0:c0 + cw] + g * conv_w_ref[2:3, c0:c0 + cw])
        y_ref[:, c0:c0 + cw] = (jax.nn.silu(cg) * u).astype(y_ref.dtype)

    out_ref[0] = x1 + jnp.dot(y_ref[...], w_down_ref[...], preferred_element_type=jnp.float32)


def _rope_tables(seq):
    pos = jnp.arange(seq, dtype=jnp.float32)[:, None]

    def tables(dim):
        inv = 1.0 / (ROPE_THETA ** (jnp.arange(0, dim, 2, dtype=jnp.float32) / dim))
        ang = pos * inv[None, :]
        return jnp.cos(ang), jnp.sin(ang)

    ca, sa = tables(MLA_ROPE)
    cb, sb = tables(DIFF_D)
    return ca.T, sa.T, cb.T, sb.T


def _gain_cols(g, tokens):
    return jnp.broadcast_to(g[:, None], (g.shape[0], tokens))


def _const_spec(shape):
    return pl.BlockSpec(shape, lambda *_: (0,) * len(shape))


def kernel(x, attn_norm_g, w_in, q_a_norm_g, w_q_up, kv_a_norm_g, w_kv_up, mla_q_norm_g,
           mla_k_norm_g, diff_q_norm_g, diff_k_norm_g, lambda_q1, lambda_k1, lambda_q2, lambda_k2,
           diff_subln_g, w_out, ffn_norm_g, w_gate, w_up, conv_w, conv_b, w_down):
    B, S, _ = x.shape
    depth = w_in.shape[0]
    bf = jnp.bfloat16
    cos_a, sin_a, cos_b, sin_b = _rope_tables(S)

    tm_proj = 256
    tq = 256
    tm_ffn = 512
    assert tq == tm_proj
    assert S % tq == 0 and S % tm_ffn == 0

    for l in range(depth):
        w_in_t = w_in[l].T.astype(bf)
        w_q_t = w_q_up[l].T.astype(bf)
        wkv_t = w_kv_up[l].T.reshape(MLA_HEADS, MLA_NOPE + MLA_V, KV_RANK)
        w_k_t = wkv_t[:, :MLA_NOPE].reshape(MLA_HEADS * MLA_NOPE, KV_RANK).astype(bf)
        w_v_t = wkv_t[:, MLA_NOPE:].reshape(MLA_HEADS * MLA_V, KV_RANK).astype(bf)

        n_tok_tiles = S // tm_proj
        tok3 = lambda b, s: (b, s, 0)
        feat3 = lambda b, s: (b, 0, s)
        blk4 = lambda b, s: (b, s, 0, 0)
        rope_a_spec = pl.BlockSpec((MLA_ROPE // 2, tm_proj), lambda b, s: (0, s))
        rope_b_spec = pl.BlockSpec((DIFF_D // 2, tm_proj), lambda b, s: (0, s))
        qt, k_a, vt, dqt, dk, dvt = pl.pallas_call(
            _proj_kernel,
            grid=(B, n_tok_tiles),
            in_specs=[
                pl.BlockSpec((1, tm_proj, D_MODEL), tok3),
                _const_spec((1, D_MODEL)),
                _const_spec((IN_COLS, D_MODEL)),
                _const_spec((Q_RANK, tm_proj)),
                _const_spec((MLA_HEADS * MLA_QK, Q_RANK)),
                _const_spec((KV_RANK, tm_proj)),
                _const_spec((MLA_HEADS * MLA_NOPE, KV_RANK)),
                _const_spec((MLA_HEADS * MLA_V, KV_RANK)),
                _const_spec((MLA_QK, tm_proj)), _const_spec((MLA_QK, tm_proj)),
                _const_spec((DIFF_D, tm_proj)), _const_spec((DIFF_D, tm_proj)),
                rope_a_spec, rope_a_spec, rope_b_spec, rope_b_spec,
            ],
            out_specs=[
                pl.BlockSpec((1, MLA_HEADS * LANES, tm_proj), feat3),
                pl.BlockSpec((1, tm_proj, MLA_HEADS * LANES), tok3),
                pl.BlockSpec((1, 1, MLA_HEADS * MLA_V, tm_proj), blk4),
                pl.BlockSpec((1, DIFF_HEADS * 2 * LANES, tm_proj), feat3),
                pl.BlockSpec((1, tm_proj, DIFF_HEADS * LANES), tok3),
                pl.BlockSpec((1, 1, DIFF_HEADS * DIFF_V, tm_proj), blk4),
            ],
            out_shape=[
                jax.ShapeDtypeStruct((B, MLA_HEADS * LANES, S), bf),
                jax.ShapeDtypeStruct((B, S, MLA_HEADS * LANES), bf),
                jax.ShapeDtypeStruct((B, n_tok_tiles, MLA_HEADS * MLA_V, tm_proj), bf),
                jax.ShapeDtypeStruct((B, DIFF_HEADS * 2 * LANES, S), bf),
                jax.ShapeDtypeStruct((B, S, DIFF_HEADS * LANES), bf),
                jax.ShapeDtypeStruct((B, n_tok_tiles, DIFF_HEADS * DIFF_V, tm_proj), bf),
            ],
            compiler_params=pltpu.CompilerParams(
                dimension_semantics=("arbitrary", "arbitrary"), vmem_limit_bytes=VMEM_LIMIT),
            name="proj",
        )(x, attn_norm_g[l].reshape(1, -1), w_in_t, _gain_cols(q_a_norm_g[l], tm_proj), w_q_t,
          _gain_cols(kv_a_norm_g[l], tm_proj), w_k_t, w_v_t,
          _gain_cols(mla_q_norm_g[l], tm_proj), _gain_cols(mla_k_norm_g[l], tm_proj),
          _gain_cols(diff_q_norm_g[l], tm_proj), _gain_cols(diff_k_norm_g[l], tm_proj),
          cos_a, sin_a, cos_b, sin_b)

        nq = S // tq
        o_a = pl.pallas_call(
            functools.partial(_mla_kernel, tq=tq),
            grid=(B, nq),
            in_specs=[
                pl.BlockSpec((1, MLA_HEADS * LANES, tq), lambda b, i: (b, 0, i)),
                pl.BlockSpec((1, nq, tq, MLA_HEADS * LANES), lambda b, i: (b, 0, 0, 0)),
                pl.BlockSpec((1, nq, MLA_HEADS * MLA_V, tq), lambda b, i: (b, 0, 0, 0)),
            ],
            out_specs=pl.BlockSpec((1, MLA_HEADS * MLA_V, tq), lambda b, i: (b, 0, i)),
            out_shape=jax.ShapeDtypeStruct((B, MLA_HEADS * MLA_V, S), bf),
            scratch_shapes=[pltpu.VMEM((MLA_HEADS, 1, tq), jnp.float32),
                            pltpu.VMEM((MLA_HEADS * (MLA_V + SUM_ROWS), tq), jnp.float32)],
            compiler_params=pltpu.CompilerParams(
                dimension_semantics=("arbitrary", "arbitrary"), vmem_limit_bytes=VMEM_LIMIT),
            name="mla_attn",
        )(qt, k_a.reshape(B, nq, tq, MLA_HEADS * LANES), vt)

        lam_init = 0.8 - 0.6 * math.exp(-0.3 * l)
        lam_spec = _const_spec((1, DIFF_D))
        o_b = pl.pallas_call(
            functools.partial(_diff_kernel, tq=tq, lam_init=lam_init),
            grid=(B, nq),
            in_specs=[
                lam_spec, lam_spec, lam_spec, lam_spec,
                _const_spec((DIFF_V, 1)),
                pl.BlockSpec((1, DIFF_HEADS * 2 * LANES, tq), lambda b, i: (b, 0, i)),
                pl.BlockSpec((1, nq, tq, DIFF_HEADS * LANES), lambda b, i: (b, 0, 0, 0)),
                pl.BlockSpec((1, nq, DIFF_HEADS * DIFF_V, tq), lambda b, i: (b, 0, 0, 0)),
            ],
            out_specs=pl.BlockSpec((1, DIFF_HEADS * DIFF_V, tq), lambda b, i: (b, 0, i)),
            out_shape=jax.ShapeDtypeStruct((B, DIFF_HEADS * DIFF_V, S), bf),
            scratch_shapes=[pltpu.VMEM((2 * DIFF_HEADS, 1, tq), jnp.float32),
                            pltpu.VMEM((2 * DIFF_HEADS * (DIFF_V + SUM_ROWS), tq), jnp.float32)],
            compiler_params=pltpu.CompilerParams(
                dimension_semantics=("arbitrary", "arbitrary"), vmem_limit_bytes=VMEM_LIMIT),
            name="diff_attn",
        )(lambda_q1[l].reshape(1, -1), lambda_k1[l].reshape(1, -1),
          lambda_q2[l].reshape(1, -1), lambda_k2[l].reshape(1, -1),
          diff_subln_g[l].reshape(-1, 1), dqt, dk.reshape(B, nq, tq, DIFF_HEADS * LANES), dvt)

        n_a = MLA_HEADS * MLA_V
        x = pl.pallas_call(
            _ffn_kernel,
            grid=(B, S // tm_ffn),
            in_specs=[
                pl.BlockSpec((1, tm_ffn, D_MODEL), lambda b, s: (b, s, 0)),
                pl.BlockSpec((1, n_a, tm_ffn), lambda b, s: (b, 0, s)),
                pl.BlockSpec((1, D_MODEL - n_a, tm_ffn), lambda b, s: (b, 0, s)),
                _const_spec((n_a, D_MODEL)),
                _const_spec((D_MODEL - n_a, D_MODEL)),
                _const_spec((1, D_MODEL)),
                _const_spec((D_MODEL, D_FF)),
                _const_spec((D_MODEL, D_FF)),
                _const_spec((CONV_WIDTH, D_FF)),
                _const_spec((1, D_FF)),
                _const_spec((D_FF, D_MODEL)),
            ],
            out_specs=pl.BlockSpec((1, tm_ffn, D_MODEL), lambda b, s: (b, s, 0)),
            out_shape=jax.ShapeDtypeStruct((B, S, D_MODEL), x.dtype),
            scratch_shapes=[pltpu.VMEM((8, D_FF), jnp.float32),
                            pltpu.VMEM((tm_ffn, D_FF), bf)],
            compiler_params=pltpu.CompilerParams(
                dimension_semantics=("arbitrary", "arbitrary"), vmem_limit_bytes=VMEM_LIMIT),
            name="ffn",
        )(x, o_a, o_b, w_out[l][:n_a].astype(bf), w_out[l][n_a:].astype(bf),
          ffn_norm_g[l].reshape(1, -1), w_gate[l].astype(bf), w_up[l].astype(bf),
          conv_w[l], conv_b[l].reshape(1, -1), w_down[l].astype(bf))
    return x
```

```python
import functools
import math

import jax
import jax.numpy as jnp
from jax import lax
from jax.experimental import pallas as pl
from jax.experimental.pallas import tpu as pltpu

D_MODEL = 1024
MLA_HEADS = 8
MLA_NOPE = 64
MLA_ROPE = 32
MLA_V = 64
MLA_QK = MLA_NOPE + MLA_ROPE
Q_RANK = 384
KV_RANK = 256
DIFF_HEADS = 4
DIFF_D = 64
DIFF_V = 2 * DIFF_D
D_FF = 2816
CONV_WIDTH = 3
ROPE_THETA = 10000.0
EPS = 1e-6
LANES = 128
SUM_ROWS = 16
LOG2E = math.log2(math.e)

R_Q = 0
R_KV = R_Q + Q_RANK
R_KPE = R_KV + KV_RANK
R_DQ = R_KPE + MLA_ROPE
R_DK = R_DQ + DIFF_HEADS * 2 * DIFF_D
R_DV = R_DK + DIFF_HEADS * 2 * DIFF_D
IN_COLS = R_DV + DIFF_HEADS * DIFF_V

VMEM_LIMIT = 56 * 1024 * 1024


def _rms_rows(x, g):
    ms = jnp.mean(x * x, axis=-1, keepdims=True)
    return x * lax.rsqrt(ms + EPS) * g


def _rms_cols(xt, n):
    return lax.rsqrt(jnp.sum(xt * xt, axis=0, keepdims=True) * (1.0 / n) + EPS)


def _rope_cols(xt, cos, sin):
    half = xt.shape[0] // 2
    x1, x2 = xt[:half], xt[half:]
    return x1 * cos - x2 * sin, x2 * cos + x1 * sin


def _proj_kernel(x_ref, g_attn_ref, w_in_ref, g_qa_ref, w_q_ref, g_kva_ref, w_k_ref, w_v_ref,
                 g_q_ref, g_k_ref, g_dq_ref, g_dk_ref,
                 cos_a_ref, sin_a_ref, cos_b_ref, sin_b_ref,
                 qt_ref, k_ref, vt_ref, dqt_ref, dk_ref, dvt_ref):
    tm = x_ref.shape[1]
    bf = jnp.bfloat16
    h = _rms_rows(x_ref[0], g_attn_ref[...]).astype(bf)
    proj = lax.dot_general(w_in_ref[...], h, (((1,), (1,)), ((), ())),
                           preferred_element_type=jnp.float32)
    cos_a, sin_a = cos_a_ref[...], sin_a_ref[...]
    cos_b, sin_b = cos_b_ref[...], sin_b_ref[...]
    zeros_pad = jnp.zeros((LANES - MLA_QK, tm), jnp.float32)

    def head_a(nope, pe, g):
        nope = nope * _rms_cols(nope, MLA_NOPE) * g[:MLA_NOPE]
        pe = pe * _rms_cols(pe, MLA_ROPE) * g[MLA_NOPE:]
        p1, p2 = _rope_cols(pe, cos_a, sin_a)
        return jnp.concatenate([nope, p1, p2, zeros_pad], axis=0)

    def head_b(xt, g):
        xt = xt * _rms_cols(xt, DIFF_D) * g
        return jnp.concatenate(_rope_cols(xt, cos_b, sin_b), axis=0)

    cq = proj[R_Q:R_Q + Q_RANK]
    cq = (cq * _rms_cols(cq, Q_RANK) * g_qa_ref[...]).astype(bf)
    q = jnp.dot(w_q_ref[...], cq, preferred_element_type=jnp.float32)
    g_q = g_q_ref[...] * (MLA_QK ** -0.5 * LOG2E)
    for hd in range(MLA_HEADS):
        r0 = hd * MLA_QK
        qt_ref[0, hd * LANES:(hd + 1) * LANES, :] = head_a(
            q[r0:r0 + MLA_NOPE], q[r0 + MLA_NOPE:r0 + MLA_QK], g_q).astype(bf)

    ckv = proj[R_KV:R_KV + KV_RANK]
    ckv = (ckv * _rms_cols(ckv, KV_RANK) * g_kva_ref[...]).astype(bf)
    kn = jnp.dot(w_k_ref[...], ckv, preferred_element_type=jnp.float32)
    g_k = g_k_ref[...]
    kpe = proj[R_KPE:R_KPE + MLA_ROPE]
    for hd in range(MLA_HEADS):
        kt = head_a(kn[hd * MLA_NOPE:(hd + 1) * MLA_NOPE], kpe, g_k)
        k_ref[0, :, hd * LANES:(hd + 1) * LANES] = kt.T.astype(bf)
    vt_ref[0, 0] = jnp.dot(w_v_ref[...], ckv, preferred_element_type=jnp.float32).astype(bf)

    g_dq = g_dq_ref[...] * (DIFF_D ** -0.5 * LOG2E)
    g_dk = g_dk_ref[...]
    zeros_half = jnp.zeros((DIFF_D, tm), bf)
    for hd in range(DIFF_HEADS):
        r0 = R_DQ + hd * 2 * DIFF_D
        q1 = head_b(proj[r0:r0 + DIFF_D], g_dq).astype(bf)
        q2 = head_b(proj[r0 + DIFF_D:r0 + 2 * DIFF_D], g_dq).astype(bf)
        b0 = 2 * hd * LANES
        dqt_ref[0, b0:b0 + LANES, :] = jnp.concatenate([q1, zeros_half], axis=0)
        dqt_ref[0, b0 + LANES:b0 + 2 * LANES, :] = jnp.concatenate([zeros_half, q2], axis=0)
        r0 = R_DK + hd * 2 * DIFF_D
        dkt = jnp.concatenate([head_b(proj[r0:r0 + DIFF_D], g_dk),
                               head_b(proj[r0 + DIFF_D:r0 + 2 * DIFF_D], g_dk)], axis=0)
        dk_ref[0, :, hd * LANES:(hd + 1) * LANES] = dkt.T.astype(bf)
    dvt_ref[0, 0] = proj[R_DV:R_DV + DIFF_HEADS * DIFF_V].astype(bf)


def _causal_mask(s_t, tk, tq):
    key = lax.broadcasted_iota(jnp.int32, (tk, tq), 0)
    qry = lax.broadcasted_iota(jnp.int32, (tk, tq), 1)
    return jnp.where(key <= qry, s_t, -jnp.inf)


def _flash_chain(s_t, vt, m_ref, acc_ref, c, rows, cols):
    m_old = m_ref[c, :, cols]
    m_new = jnp.maximum(m_old, jnp.max(s_t, axis=0, keepdims=True))
    alpha = jnp.exp2(m_old - m_new)
    p = jnp.exp2(s_t - m_new).astype(jnp.bfloat16)
    m_ref[c, :, cols] = m_new
    vt_ones = jnp.concatenate([vt, jnp.ones((SUM_ROWS, vt.shape[1]), vt.dtype)], axis=0)
    acc_ref[rows, cols] = alpha * acc_ref[rows, cols] + jnp.dot(
        vt_ones, p, preferred_element_type=jnp.float32)


def _init_stats(m_ref, acc_ref):
    m_ref[...] = jnp.full(m_ref.shape, -jnp.inf, m_ref.dtype)
    acc_ref[...] = jnp.zeros_like(acc_ref)


def _sweep_key_tiles(tile, qi, tq, tk):
    per_q = tq // tk

    def body(j, carry):
        tile(j, 0, False)
        return carry

    lax.fori_loop(0, qi * per_q, body, 0)
    for d in range(per_q):
        tile(qi * per_q + d, d * tk, True)


def _mla_kernel(qt_ref, k_ref, vt_ref, o_ref, m_ref, acc_ref, *, tq, tk):
    qi = pl.program_id(1)
    _init_stats(m_ref, acc_ref)
    acc_rows = MLA_V + SUM_ROWS

    def tile(j, q_lo, masked):
        cols = slice(q_lo, tq)

        def scores(hd):
            s_t = jnp.dot(k_ref[0, j, :, hd * LANES:(hd + 1) * LANES],
                          qt_ref[0, hd * LANES:(hd + 1) * LANES, cols],
                          preferred_element_type=jnp.float32)
            return _causal_mask(s_t, tk, tq - q_lo) if masked else s_t

        s_all = [scores(hd) for hd in range(MLA_HEADS)]
        for hd in range(MLA_HEADS):
            _flash_chain(s_all[hd], vt_ref[0, j, hd * MLA_V:(hd + 1) * MLA_V, :], m_ref, acc_ref, hd,
                         slice(hd * acc_rows, (hd + 1) * acc_rows), cols)

    _sweep_key_tiles(tile, qi, tq, tk)
    for hd in range(MLA_HEADS):
        a0 = hd * acc_rows
        o_ref[0, hd * MLA_V:(hd + 1) * MLA_V, :] = (
            acc_ref[a0:a0 + MLA_V, :] / acc_ref[a0 + MLA_V:a0 + MLA_V + 1, :]).astype(o_ref.dtype)


def _diff_kernel(lq1_ref, lk1_ref, lq2_ref, lk2_ref, g_sub_ref, dqt_ref, dk_ref, dvt_ref, o_ref,
                 m_ref, acc_ref, *, tq, tk, lam_init):
    qi = pl.program_id(1)
    _init_stats(m_ref, acc_ref)
    acc_rows = DIFF_V + SUM_ROWS
    lam = (jnp.exp(jnp.sum(lq1_ref[...] * lk1_ref[...], axis=-1, keepdims=True))
           - jnp.exp(jnp.sum(lq2_ref[...] * lk2_ref[...], axis=-1, keepdims=True))
           + lam_init)

    def tile(j, q_lo, masked):
        cols = slice(q_lo, tq)

        def scores(c):
            hd = c // 2
            s_t = jnp.dot(dk_ref[0, j, :, hd * LANES:(hd + 1) * LANES],
                          dqt_ref[0, c * LANES:(c + 1) * LANES, cols],
                          preferred_element_type=jnp.float32)
            return _causal_mask(s_t, tk, tq - q_lo) if masked else s_t

        s_all = [scores(c) for c in range(2 * DIFF_HEADS)]
        for c in range(2 * DIFF_HEADS):
            hd = c // 2
            vt = dvt_ref[0, j, hd * DIFF_V:(hd + 1) * DIFF_V, :]
            _flash_chain(s_all[c], vt, m_ref, acc_ref, c, slice(c * acc_rows, (c + 1) * acc_rows), cols)

    def normalised(c):
        a0 = c * acc_rows
        return acc_ref[a0:a0 + DIFF_V, :] / acc_ref[a0 + DIFF_V:a0 + DIFF_V + 1, :]

    _sweep_key_tiles(tile, qi, tq, tk)
    for hd in range(DIFF_HEADS):
        o = normalised(2 * hd) - lam * normalised(2 * hd + 1)
        ms = jnp.mean(o * o, axis=0, keepdims=True)
        o = o * lax.rsqrt(ms + EPS) * g_sub_ref[...] * (1.0 - lam_init)
        o_ref[0, hd * DIFF_V:(hd + 1) * DIFF_V, :] = o.astype(o_ref.dtype)


FF_CHUNK = 512


def _ffn_kernel(x_ref, oa_ref, ob_ref, w_oa_ref, w_ob_ref, g_ffn_ref, w_gate_ref, w_up_ref,
                conv_w_ref, conv_b_ref, w_down_ref, out_ref, prev_ref, y_ref):
    si = pl.program_id(1)
    tm = x_ref.shape[1]
    contract0 = (((0,), (0,)), ((), ()))
    mix = (lax.dot_general(oa_ref[0], w_oa_ref[...], contract0, preferred_element_type=jnp.float32)
           + lax.dot_general(ob_ref[0], w_ob_ref[...], contract0,
                             preferred_element_type=jnp.float32))
    x1 = x_ref[0] + mix
    h = _rms_rows(x1, g_ffn_ref[...]).astype(jnp.bfloat16)

    @pl.when(si == 0)
    def _():
        prev_ref[...] = jnp.zeros_like(prev_ref)

    for c0 in range(0, D_FF, FF_CHUNK):
        cw = min(FF_CHUNK, D_FF - c0)
        g = jnp.dot(h, w_gate_ref[:, c0:c0 + cw], preferred_element_type=jnp.float32)
        u = jnp.dot(h, w_up_ref[:, c0:c0 + cw], preferred_element_type=jnp.float32)
        row = lax.broadcasted_iota(jnp.int32, (tm, cw), 0)
        p1 = prev_ref[7:8, c0:c0 + cw]
        p2 = prev_ref[6:7, c0:c0 + cw]
        g1 = jnp.where(row == 0, p1, pltpu.roll(g, 1, axis=0))
        g2 = jnp.where(row == 0, p2, jnp.where(row == 1, p1, pltpu.roll(g, 2, axis=0)))
        prev_ref[:, c0:c0 + cw] = g[tm - 8:tm, :]
        cg = (conv_b_ref[:, c0:c0 + cw] + g2 * conv_w_ref[0:1, c0:c0 + cw]
              + g1 * conv_w_ref[1:2, c0:c0 + cw] + g * conv_w_ref[2:3, c0:c0 + cw])
        y_ref[:, c0:c0 + cw] = (jax.nn.silu(cg) * u).astype(y_ref.dtype)

    out_ref[0] = x1 + jnp.dot(y_ref[...], w_down_ref[...], preferred_element_type=jnp.float32)


def _rope_tables(seq):
    pos = jnp.arange(seq, dtype=jnp.float32)[:, None]

    def tables(dim):
        inv = 1.0 / (ROPE_THETA ** (jnp.arange(0, dim, 2, dtype=jnp.float32) / dim))
        ang = pos * inv[None, :]
        return jnp.cos(ang), jnp.sin(ang)

    ca, sa = tables(MLA_ROPE)
    cb, sb = tables(DIFF_D)
    return ca.T, sa.T, cb.T, sb.T


def _gain_cols(g, tokens):
    return jnp.broadcast_to(g[:, None], (g.shape[0], tokens))


def _const_spec(shape):
    return pl.BlockSpec(shape, lambda *_: (0,) * len(shape))


def kernel(x, attn_norm_g, w_in, q_a_norm_g, w_q_up, kv_a_norm_g, w_kv_up, mla_q_norm_g,
           mla_k_norm_g, diff_q_norm_g, diff_k_norm_g, lambda_q1, lambda_k1, lambda_q2, lambda_k2,
           diff_subln_g, w_out, ffn_norm_g, w_gate, w_up, conv_w, conv_b, w_down):
    B, S, _ = x.shape
    depth = w_in.shape[0]
    bf = jnp.bfloat16
    cos_a, sin_a, cos_b, sin_b = _rope_tables(S)

    tm_proj = 256
    tk = tm_proj
    tq = 512
    tm_ffn = 512
    assert tq % tk == 0 and S % tq == 0 and S % tm_ffn == 0
    nk = S // tk

    for l in range(depth):
        w_in_t = w_in[l].T.astype(bf)
        w_q_t = w_q_up[l].T.astype(bf)
        wkv_t = w_kv_up[l].T.reshape(MLA_HEADS, MLA_NOPE + MLA_V, KV_RANK)
        w_k_t = wkv_t[:, :MLA_NOPE].reshape(MLA_HEADS * MLA_NOPE, KV_RANK).astype(bf)
        w_v_t = wkv_t[:, MLA_NOPE:].reshape(MLA_HEADS * MLA_V, KV_RANK).astype(bf)

        n_tok_tiles = S // tm_proj
        tok3 = lambda b, s: (b, s, 0)
        feat3 = lambda b, s: (b, 0, s)
        blk4 = lambda b, s: (b, s, 0, 0)
        rope_a_spec = pl.BlockSpec((MLA_ROPE // 2, tm_proj), lambda b, s: (0, s))
        rope_b_spec = pl.BlockSpec((DIFF_D // 2, tm_proj), lambda b, s: (0, s))
        qt, k_a, vt, dqt, dk, dvt = pl.pallas_call(
            _proj_kernel,
            grid=(B, n_tok_tiles),
            in_specs=[
                pl.BlockSpec((1, tm_proj, D_MODEL), tok3),
                _const_spec((1, D_MODEL)),
                _const_spec((IN_COLS, D_MODEL)),
                _const_spec((Q_RANK, tm_proj)),
                _const_spec((MLA_HEADS * MLA_QK, Q_RANK)),
                _const_spec((KV_RANK, tm_proj)),
                _const_spec((MLA_HEADS * MLA_NOPE, KV_RANK)),
                _const_spec((MLA_HEADS * MLA_V, KV_RANK)),
                _const_spec((MLA_QK, tm_proj)), _const_spec((MLA_QK, tm_proj)),
                _const_spec((DIFF_D, tm_proj)), _const_spec((DIFF_D, tm_proj)),
                rope_a_spec, rope_a_spec, rope_b_spec, rope_b_spec,
            ],
            out_specs=[
                pl.BlockSpec((1, MLA_HEADS * LANES, tm_proj), feat3),
                pl.BlockSpec((1, tm_proj, MLA_HEADS * LANES), tok3),
                pl.BlockSpec((1, 1, MLA_HEADS * MLA_V, tm_proj), blk4),
                pl.BlockSpec((1, DIFF_HEADS * 2 * LANES, tm_proj), feat3),
                pl.BlockSpec((1, tm_proj, DIFF_HEADS * LANES), tok3),
                pl.BlockSpec((1, 1, DIFF_HEADS * DIFF_V, tm_proj), blk4),
            ],
            out_shape=[
                jax.ShapeDtypeStruct((B, MLA_HEADS * LANES, S), bf),
                jax.ShapeDtypeStruct((B, S, MLA_HEADS * LANES), bf),
                jax.ShapeDtypeStruct((B, n_tok_tiles, MLA_HEADS * MLA_V, tm_proj), bf),
                jax.ShapeDtypeStruct((B, DIFF_HEADS * 2 * LANES, S), bf),
                jax.ShapeDtypeStruct((B, S, DIFF_HEADS * LANES), bf),
                jax.ShapeDtypeStruct((B, n_tok_tiles, DIFF_HEADS * DIFF_V, tm_proj), bf),
            ],
            compiler_params=pltpu.CompilerParams(
                dimension_semantics=("arbitrary", "arbitrary"), vmem_limit_bytes=VMEM_LIMIT),
            name="proj",
        )(x, attn_norm_g[l].reshape(1, -1), w_in_t, _gain_cols(q_a_norm_g[l], tm_proj), w_q_t,
          _gain_cols(kv_a_norm_g[l], tm_proj), w_k_t, w_v_t,
          _gain_cols(mla_q_norm_g[l], tm_proj), _gain_cols(mla_k_norm_g[l], tm_proj),
          _gain_cols(diff_q_norm_g[l], tm_proj), _gain_cols(diff_k_norm_g[l], tm_proj),
          cos_a, sin_a, cos_b, sin_b)

        nq = S // tq
        o_a = pl.pallas_call(
            functools.partial(_mla_kernel, tq=tq, tk=tk),
            grid=(B, nq),
            in_specs=[
                pl.BlockSpec((1, MLA_HEADS * LANES, tq), lambda b, i: (b, 0, i)),
                pl.BlockSpec((1, nk, tk, MLA_HEADS * LANES), lambda b, i: (b, 0, 0, 0)),
                pl.BlockSpec((1, nk, MLA_HEADS * MLA_V, tk), lambda b, i: (b, 0, 0, 0)),
            ],
            out_specs=pl.BlockSpec((1, MLA_HEADS * MLA_V, tq), lambda b, i: (b, 0, i)),
            out_shape=jax.ShapeDtypeStruct((B, MLA_HEADS * MLA_V, S), bf),
            scratch_shapes=[pltpu.VMEM((MLA_HEADS, 1, tq), jnp.float32),
                            pltpu.VMEM((MLA_HEADS * (MLA_V + SUM_ROWS), tq), jnp.float32)],
            compiler_params=pltpu.CompilerParams(
                dimension_semantics=("arbitrary", "arbitrary"), vmem_limit_bytes=VMEM_LIMIT),
            name="mla_attn",
        )(qt, k_a.reshape(B, nk, tk, MLA_HEADS * LANES), vt)

        lam_init = 0.8 - 0.6 * math.exp(-0.3 * l)
        lam_spec = _const_spec((1, DIFF_D))
        o_b = pl.pallas_call(
            functools.partial(_diff_kernel, tq=tq, tk=tk, lam_init=lam_init),
            grid=(B, nq),
            in_specs=[
                lam_spec, lam_spec, lam_spec, lam_spec,
                _const_spec((DIFF_V, 1)),
                pl.BlockSpec((1, DIFF_HEADS * 2 * LANES, tq), lambda b, i: (b, 0, i)),
                pl.BlockSpec((1, nk, tk, DIFF_HEADS * LANES), lambda b, i: (b, 0, 0, 0)),
                pl.BlockSpec((1, nk, DIFF_HEADS * DIFF_V, tk), lambda b, i: (b, 0, 0, 0)),
            ],
            out_specs=pl.BlockSpec((1, DIFF_HEADS * DIFF_V, tq), lambda b, i: (b, 0, i)),
            out_shape=jax.ShapeDtypeStruct((B, DIFF_HEADS * DIFF_V, S), bf),
            scratch_shapes=[pltpu.VMEM((2 * DIFF_HEADS, 1, tq), jnp.float32),
                            pltpu.VMEM((2 * DIFF_HEADS * (DIFF_V + SUM_ROWS), tq), jnp.float32)],
            compiler_params=pltpu.CompilerParams(
                dimension_semantics=("arbitrary", "arbitrary"), vmem_limit_bytes=VMEM_LIMIT),
            name="diff_attn",
        )(lambda_q1[l].reshape(1, -1), lambda_k1[l].reshape(1, -1),
          lambda_q2[l].reshape(1, -1), lambda_k2[l].reshape(1, -1),
          diff_subln_g[l].reshape(-1, 1), dqt, dk.reshape(B, nk, tk, DIFF_HEADS * LANES), dvt)

        n_a = MLA_HEADS * MLA_V
        x = pl.pallas_call(
            _ffn_kernel,
            grid=(B, S // tm_ffn),
            in_specs=[
                pl.BlockSpec((1, tm_ffn, D_MODEL), lambda b, s: (b, s, 0)),
                pl.BlockSpec((1, n_a, tm_ffn), lambda b, s: (b, 0, s)),
                pl.BlockSpec((1, D_MODEL - n_a, tm_ffn), lambda b, s: (b, 0, s)),
                _const_spec((n_a, D_MODEL)),
                _const_spec((D_MODEL - n_a, D_MODEL)),
                _const_spec((1, D_MODEL)),
                _const_spec((D_MODEL, D_FF)),
                _const_spec((D_MODEL, D_FF)),
                _const_spec((CONV_WIDTH, D_FF)),
                _const_spec((1, D_FF)),
                _const_spec((D_FF, D_MODEL)),
            ],
            out_specs=pl.BlockSpec((1, tm_ffn, D_MODEL), lambda b, s: (b, s, 0)),
            out_shape=jax.ShapeDtypeStruct((B, S, D_MODEL), x.dtype),
            scratch_shapes=[pltpu.VMEM((8, D_FF), jnp.float32),
                            pltpu.VMEM((tm_ffn, D_FF), bf)],
            compiler_params=pltpu.CompilerParams(
                dimension_semantics=("arbitrary", "arbitrary"), vmem_limit_bytes=VMEM_LIMIT),
            name="ffn",
        )(x, o_a, o_b, w_out[l][:n_a].astype(bf), w_out[l][n_a:].astype(bf),
          ffn_norm_g[l].reshape(1, -1), w_gate[l].astype(bf), w_up[l].astype(bf),
          conv_w[l], conv_b[l].reshape(1, -1), w_down[l].astype(bf))
    return x
```

```python
import functools
import math
from typing import NamedTuple

import jax
import jax.numpy as jnp
from jax import lax
from jax.experimental import pallas as pl
from jax.experimental.pallas import tpu as pltpu

D_MODEL = 1024
MLA_HEADS = 8
MLA_NOPE = 64
MLA_ROPE = 32
MLA_V = 64
MLA_QK = MLA_NOPE + MLA_ROPE
Q_RANK = 384
KV_RANK = 256
DIFF_HEADS = 4
DIFF_D = 64
DIFF_V = 2 * DIFF_D
D_FF = 2816
CONV_WIDTH = 3
ROPE_THETA = 10000.0
EPS = 1e-6
LANES = 128
SUM_ROWS = 16
LOG2E = math.log2(math.e)

R_Q = 0
R_KV = R_Q + Q_RANK
R_KPE = R_KV + KV_RANK
R_DQ = R_KPE + MLA_ROPE
R_DK = R_DQ + DIFF_HEADS * 2 * DIFF_D
R_DV = R_DK + DIFF_HEADS * 2 * DIFF_D
IN_COLS = R_DV + DIFF_HEADS * DIFF_V

VMEM_LIMIT = 56 * 1024 * 1024


def _rms_rows(x, g):
    ms = jnp.mean(x * x, axis=-1, keepdims=True)
    return x * lax.rsqrt(ms + EPS) * g


def _rms_cols(xt, n):
    return lax.rsqrt(jnp.sum(xt * xt, axis=0, keepdims=True) * (1.0 / n) + EPS)


def _rope_cols(xt, cos, sin):
    half = xt.shape[0] // 2
    x1, x2 = xt[:half], xt[half:]
    return x1 * cos - x2 * sin, x2 * cos + x1 * sin


def _proj_kernel(x_ref, g_attn_ref, w_in_ref, g_qa_ref, w_q_ref, g_kva_ref, w_k_ref, w_v_ref,
                 g_q_ref, g_k_ref, g_dq_ref, g_dk_ref,
                 cos_a_ref, sin_a_ref, cos_b_ref, sin_b_ref,
                 qt_ref, k_ref, vt_ref, dqt_ref, dk_ref, dvt_ref):
    tm = x_ref.shape[1]
    bf = jnp.bfloat16
    h = _rms_rows(x_ref[0], g_attn_ref[...]).astype(bf)
    proj = lax.dot_general(w_in_ref[...], h, (((1,), (1,)), ((), ())),
                           preferred_element_type=jnp.float32)
    cos_a, sin_a = cos_a_ref[...], sin_a_ref[...]
    cos_b, sin_b = cos_b_ref[...], sin_b_ref[...]
    zeros_pad = jnp.zeros((LANES - MLA_QK, tm), jnp.float32)

    def head_a(nope, pe, g):
        nope = nope * _rms_cols(nope, MLA_NOPE) * g[:MLA_NOPE]
        pe = pe * _rms_cols(pe, MLA_ROPE) * g[MLA_NOPE:]
        p1, p2 = _rope_cols(pe, cos_a, sin_a)
        return jnp.concatenate([nope, p1, p2, zeros_pad], axis=0)

    def head_b(xt, g):
        xt = xt * _rms_cols(xt, DIFF_D) * g
        return jnp.concatenate(_rope_cols(xt, cos_b, sin_b), axis=0)

    cq = proj[R_Q:R_Q + Q_RANK]
    cq = (cq * _rms_cols(cq, Q_RANK) * g_qa_ref[...]).astype(bf)
    q = jnp.dot(w_q_ref[...], cq, preferred_element_type=jnp.float32)
    g_q = g_q_ref[...] * (MLA_QK ** -0.5 * LOG2E)
    for hd in range(MLA_HEADS):
        r0 = hd * MLA_QK
        qt_ref[0, hd * LANES:(hd + 1) * LANES, :] = head_a(
            q[r0:r0 + MLA_NOPE], q[r0 + MLA_NOPE:r0 + MLA_QK], g_q).astype(bf)

    ckv = proj[R_KV:R_KV + KV_RANK]
    ckv = (ckv * _rms_cols(ckv, KV_RANK) * g_kva_ref[...]).astype(bf)
    kn = jnp.dot(w_k_ref[...], ckv, preferred_element_type=jnp.float32)
    g_k = g_k_ref[...]
    kpe = proj[R_KPE:R_KPE + MLA_ROPE]
    for hd in range(MLA_HEADS):
        kt = head_a(kn[hd * MLA_NOPE:(hd + 1) * MLA_NOPE], kpe, g_k)
        k_ref[0, :, hd * LANES:(hd + 1) * LANES] = kt.T.astype(bf)
    vt_ref[0, 0] = jnp.dot(w_v_ref[...], ckv, preferred_element_type=jnp.float32).astype(bf)

    g_dq = g_dq_ref[...] * (DIFF_D ** -0.5 * LOG2E)
    g_dk = g_dk_ref[...]
    zeros_half = jnp.zeros((DIFF_D, tm), bf)
    for hd in range(DIFF_HEADS):
        r0 = R_DQ + hd * 2 * DIFF_D
        q1 = head_b(proj[r0:r0 + DIFF_D], g_dq).astype(bf)
        q2 = head_b(proj[r0 + DIFF_D:r0 + 2 * DIFF_D], g_dq).astype(bf)
        b0 = 2 * hd * LANES
        dqt_ref[0, b0:b0 + LANES, :] = jnp.concatenate([q1, zeros_half], axis=0)
        dqt_ref[0, b0 + LANES:b0 + 2 * LANES, :] = jnp.concatenate([zeros_half, q2], axis=0)
        r0 = R_DK + hd * 2 * DIFF_D
        dkt = jnp.concatenate([head_b(proj[r0:r0 + DIFF_D], g_dk),
                               head_b(proj[r0 + DIFF_D:r0 + 2 * DIFF_D], g_dk)], axis=0)
        dk_ref[0, :, hd * LANES:(hd + 1) * LANES] = dkt.T.astype(bf)
    dvt_ref[0, 0] = proj[R_DV:R_DV + DIFF_HEADS * DIFF_V].astype(bf)


def _causal_mask(s_t, tk, tq):
    key = lax.broadcasted_iota(jnp.int32, (tk, tq), 0)
    qry = lax.broadcasted_iota(jnp.int32, (tk, tq), 1)
    return jnp.where(key <= qry, s_t, -jnp.inf)


class _Chain(NamedTuple):
    k_lanes: slice
    q_rows: slice
    v_rows: slice
    acc_rows: slice


def _pipelined_sweep(chains, qt_ref, k_ref, vt_ref, qi, tq, tk, p_ref, al_ref, m_ref, acc_ref):
    assert tq == 2 * tk
    full, half = slice(0, tq), slice(tk, tq)

    def stage_a(j, cols):
        return [jnp.dot(k_ref[0, j, :, ch.k_lanes], qt_ref[0, ch.q_rows, cols],
                        preferred_element_type=jnp.float32) for ch in chains]

    def stage_b(s_all, slot, cols, masked):
        for c, s_t in enumerate(s_all):
            if masked:
                s_t = _causal_mask(s_t, tk, cols.stop - cols.start)
            m_old = m_ref[c, :, cols]
            m_new = jnp.maximum(m_old, jnp.max(s_t, axis=0, keepdims=True))
            m_ref[c, :, cols] = m_new
            al_ref[slot, c, :, cols] = jnp.exp2(m_old - m_new)
            p_ref[slot, c, :, cols] = jnp.exp2(s_t - m_new).astype(p_ref.dtype)

    def stage_c(j, slot, cols):
        ones = jnp.ones((SUM_ROWS, tk), vt_ref.dtype)
        for c, ch in enumerate(chains):
            vt_ones = jnp.concatenate([vt_ref[0, j, ch.v_rows, :], ones], axis=0)
            acc_ref[ch.acc_rows, cols] = (
                al_ref[slot, c, :, cols] * acc_ref[ch.acc_rows, cols]
                + jnp.dot(vt_ones, p_ref[slot, c, :, cols], preferred_element_type=jnp.float32))

    m_ref[...] = jnp.full(m_ref.shape, -jnp.inf, m_ref.dtype)
    acc_ref[...] = jnp.zeros_like(acc_ref)
    p_ref[1] = jnp.zeros(p_ref.shape[1:], p_ref.dtype)
    al_ref[1] = jnp.ones(al_ref.shape[1:], al_ref.dtype)

    def step(j, slot, cols, masked):
        s_all = stage_a(j, cols)
        stage_c(jnp.maximum(j - 1, 0), 1 - slot, full)
        stage_b(s_all, slot, cols, masked)

    def two_steps(k, carry):
        step(2 * k, 0, full, False)
        step(2 * k + 1, 1, full, False)
        return carry

    lax.fori_loop(0, qi, two_steps, 0)
    step(2 * qi, 0, full, True)
    step(2 * qi + 1, 1, half, True)
    stage_c(2 * qi + 1, 1, half)


def _mla_kernel(qt_ref, k_ref, vt_ref, o_ref, p_ref, al_ref, m_ref, acc_ref, *, tq, tk):
    acc_rows = MLA_V + SUM_ROWS
    chains = [_Chain(slice(hd * LANES, (hd + 1) * LANES), slice(hd * LANES, (hd + 1) * LANES),
                     slice(hd * MLA_V, (hd + 1) * MLA_V), slice(hd * acc_rows, (hd + 1) * acc_rows))
              for hd in range(MLA_HEADS)]
    _pipelined_sweep(chains, qt_ref, k_ref, vt_ref, pl.program_id(1), tq, tk,
                     p_ref, al_ref, m_ref, acc_ref)
    for hd in range(MLA_HEADS):
        a0 = hd * acc_rows
        o_ref[0, hd * MLA_V:(hd + 1) * MLA_V, :] = (
            acc_ref[a0:a0 + MLA_V, :] / acc_ref[a0 + MLA_V:a0 + MLA_V + 1, :]).astype(o_ref.dtype)


def _diff_kernel(lq1_ref, lk1_ref, lq2_ref, lk2_ref, g_sub_ref, dqt_ref, dk_ref, dvt_ref, o_ref,
                 p_ref, al_ref, m_ref, acc_ref, *, tq, tk, lam_init):
    acc_rows = DIFF_V + SUM_ROWS
    lam = (jnp.exp(jnp.sum(lq1_ref[...] * lk1_ref[...], axis=-1, keepdims=True))
           - jnp.exp(jnp.sum(lq2_ref[...] * lk2_ref[...], axis=-1, keepdims=True))
           + lam_init)
    chains = [_Chain(slice((c // 2) * LANES, (c // 2 + 1) * LANES), slice(c * LANES, (c + 1) * LANES),
                     slice((c // 2) * DIFF_V, (c // 2 + 1) * DIFF_V),
                     slice(c * acc_rows, (c + 1) * acc_rows))
              for c in range(2 * DIFF_HEADS)]
    _pipelined_sweep(chains, dqt_ref, dk_ref, dvt_ref, pl.program_id(1), tq, tk,
                     p_ref, al_ref, m_ref, acc_ref)

    def normalised(c):
        a0 = c * acc_rows
        return acc_ref[a0:a0 + DIFF_V, :] / acc_ref[a0 + DIFF_V:a0 + DIFF_V + 1, :]

    for hd in range(DIFF_HEADS):
        o = normalised(2 * hd) - lam * normalised(2 * hd + 1)
        ms = jnp.mean(o * o, axis=0, keepdims=True)
        o = o * lax.rsqrt(ms + EPS) * g_sub_ref[...] * (1.0 - lam_init)
        o_ref[0, hd * DIFF_V:(hd + 1) * DIFF_V, :] = o.astype(o_ref.dtype)


FF_CHUNK = 512


def _ffn_kernel(x_ref, oa_ref, ob_ref, w_oa_ref, w_ob_ref, g_ffn_ref, w_gate_ref, w_up_ref,
                conv_w_ref, conv_b_ref, w_down_ref, out_ref, prev_ref, y_ref):
    si = pl.program_id(1)
    tm = x_ref.shape[1]
    contract0 = (((0,), (0,)), ((), ()))
    mix = (lax.dot_general(oa_ref[0], w_oa_ref[...], contract0, preferred_element_type=jnp.float32)
           + lax.dot_general(ob_ref[0], w_ob_ref[...], contract0,
                             preferred_element_type=jnp.float32))
    x1 = x_ref[0] + mix
    h = _rms_rows(x1, g_ffn_ref[...]).astype(jnp.bfloat16)

    @pl.when(si == 0)
    def _():
        prev_ref[...] = jnp.zeros_like(prev_ref)

    for c0 in range(0, D_FF, FF_CHUNK):
        cw = min(FF_CHUNK, D_FF - c0)
        g = jnp.dot(h, w_gate_ref[:, c0:c0 + cw], preferred_element_type=jnp.float32)
        u = jnp.dot(h, w_up_ref[:, c0:c0 + cw], preferred_element_type=jnp.float32)
        row = lax.broadcasted_iota(jnp.int32, (tm, cw), 0)
        p1 = prev_ref[7:8, c0:c0 + cw]
        p2 = prev_ref[6:7, c0:c0 + cw]
        g1 = jnp.where(row == 0, p1, pltpu.roll(g, 1, axis=0))
        g2 = jnp.where(row == 0, p2, jnp.where(row == 1, p1, pltpu.roll(g, 2, axis=0)))
        prev_ref[:, c0:c0 + cw] = g[tm - 8:tm, :]
        cg = (conv_b_ref[:, c0:c0 + cw] + g2 * conv_w_ref[0:1, c0:c0 + cw]
              + g1 * conv_w_ref[1:2, c0:c0 + cw] + g * conv_w_ref[2:3, c0:c0 + cw])
        y_ref[:, c0:c0 + cw] = (jax.nn.silu(cg) * u).astype(y_ref.dtype)

    out_ref[0] = x1 + jnp.dot(y_ref[...], w_down_ref[...], preferred_element_type=jnp.float32)


def _rope_tables(seq):
    pos = jnp.arange(seq, dtype=jnp.float32)[:, None]

    def tables(dim):
        inv = 1.0 / (ROPE_THETA ** (jnp.arange(0, dim, 2, dtype=jnp.float32) / dim))
        ang = pos * inv[None, :]
        return jnp.cos(ang), jnp.sin(ang)

    ca, sa = tables(MLA_ROPE)
    cb, sb = tables(DIFF_D)
    return ca.T, sa.T, cb.T, sb.T


def _gain_cols(g, tokens):
    return jnp.broadcast_to(g[:, None], (g.shape[0], tokens))


def _attn_scratch(n_chains, dv, tq, tk):
    return [pltpu.VMEM((2, n_chains, tk, tq), jnp.bfloat16),
            pltpu.VMEM((2, n_chains, 1, tq), jnp.float32),
            pltpu.VMEM((n_chains, 1, tq), jnp.float32),
            pltpu.VMEM((n_chains * (dv + SUM_ROWS), tq), jnp.float32)]


def _const_spec(shape):
    return pl.BlockSpec(shape, lambda *_: (0,) * len(shape))


def kernel(x, attn_norm_g, w_in, q_a_norm_g, w_q_up, kv_a_norm_g, w_kv_up, mla_q_norm_g,
           mla_k_norm_g, diff_q_norm_g, diff_k_norm_g, lambda_q1, lambda_k1, lambda_q2, lambda_k2,
           diff_subln_g, w_out, ffn_norm_g, w_gate, w_up, conv_w, conv_b, w_down):
    B, S, _ = x.shape
    depth = w_in.shape[0]
    bf = jnp.bfloat16
    cos_a, sin_a, cos_b, sin_b = _rope_tables(S)

    tm_proj = 256
    tk = tm_proj
    tq = 2 * tk
    tm_ffn = 512
    assert S % tq == 0 and S % tm_ffn == 0
    nk = S // tk

    for l in range(depth):
        w_in_t = w_in[l].T.astype(bf)
        w_q_t = w_q_up[l].T.astype(bf)
        wkv_t = w_kv_up[l].T.reshape(MLA_HEADS, MLA_NOPE + MLA_V, KV_RANK)
        w_k_t = wkv_t[:, :MLA_NOPE].reshape(MLA_HEADS * MLA_NOPE, KV_RANK).astype(bf)
        w_v_t = wkv_t[:, MLA_NOPE:].reshape(MLA_HEADS * MLA_V, KV_RANK).astype(bf)

        n_tok_tiles = S // tm_proj
        tok3 = lambda b, s: (b, s, 0)
        feat3 = lambda b, s: (b, 0, s)
        blk4 = lambda b, s: (b, s, 0, 0)
        rope_a_spec = pl.BlockSpec((MLA_ROPE // 2, tm_proj), lambda b, s: (0, s))
        rope_b_spec = pl.BlockSpec((DIFF_D // 2, tm_proj), lambda b, s: (0, s))
        qt, k_a, vt, dqt, dk, dvt = pl.pallas_call(
            _proj_kernel,
            grid=(B, n_tok_tiles),
            in_specs=[
                pl.BlockSpec((1, tm_proj, D_MODEL), tok3),
                _const_spec((1, D_MODEL)),
                _const_spec((IN_COLS, D_MODEL)),
                _const_spec((Q_RANK, tm_proj)),
                _const_spec((MLA_HEADS * MLA_QK, Q_RANK)),
                _const_spec((KV_RANK, tm_proj)),
                _const_spec((MLA_HEADS * MLA_NOPE, KV_RANK)),
                _const_spec((MLA_HEADS * MLA_V, KV_RANK)),
                _const_spec((MLA_QK, tm_proj)), _const_spec((MLA_QK, tm_proj)),
                _const_spec((DIFF_D, tm_proj)), _const_spec((DIFF_D, tm_proj)),
                rope_a_spec, rope_a_spec, rope_b_spec, rope_b_spec,
            ],
            out_specs=[
                pl.BlockSpec((1, MLA_HEADS * LANES, tm_proj), feat3),
                pl.BlockSpec((1, tm_proj, MLA_HEADS * LANES), tok3),
                pl.BlockSpec((1, 1, MLA_HEADS * MLA_V, tm_proj), blk4),
                pl.BlockSpec((1, DIFF_HEADS * 2 * LANES, tm_proj), feat3),
                pl.BlockSpec((1, tm_proj, DIFF_HEADS * LANES), tok3),
                pl.BlockSpec((1, 1, DIFF_HEADS * DIFF_V, tm_proj), blk4),
            ],
            out_shape=[
                jax.ShapeDtypeStruct((B, MLA_HEADS * LANES, S), bf),
                jax.ShapeDtypeStruct((B, S, MLA_HEADS * LANES), bf),
                jax.ShapeDtypeStruct((B, n_tok_tiles, MLA_HEADS * MLA_V, tm_proj), bf),
                jax.ShapeDtypeStruct((B, DIFF_HEADS * 2 * LANES, S), bf),
                jax.ShapeDtypeStruct((B, S, DIFF_HEADS * LANES), bf),
                jax.ShapeDtypeStruct((B, n_tok_tiles, DIFF_HEADS * DIFF_V, tm_proj), bf),
            ],
            compiler_params=pltpu.CompilerParams(
                dimension_semantics=("arbitrary", "arbitrary"), vmem_limit_bytes=VMEM_LIMIT),
            name="proj",
        )(x, attn_norm_g[l].reshape(1, -1), w_in_t, _gain_cols(q_a_norm_g[l], tm_proj), w_q_t,
          _gain_cols(kv_a_norm_g[l], tm_proj), w_k_t, w_v_t,
          _gain_cols(mla_q_norm_g[l], tm_proj), _gain_cols(mla_k_norm_g[l], tm_proj),
          _gain_cols(diff_q_norm_g[l], tm_proj), _gain_cols(diff_k_norm_g[l], tm_proj),
          cos_a, sin_a, cos_b, sin_b)

        nq = S // tq
        o_a = pl.pallas_call(
            functools.partial(_mla_kernel, tq=tq, tk=tk),
            grid=(B, nq),
            in_specs=[
                pl.BlockSpec((1, MLA_HEADS * LANES, tq), lambda b, i: (b, 0, i)),
                pl.BlockSpec((1, nk, tk, MLA_HEADS * LANES), lambda b, i: (b, 0, 0, 0)),
                pl.BlockSpec((1, nk, MLA_HEADS * MLA_V, tk), lambda b, i: (b, 0, 0, 0)),
            ],
            out_specs=pl.BlockSpec((1, MLA_HEADS * MLA_V, tq), lambda b, i: (b, 0, i)),
            out_shape=jax.ShapeDtypeStruct((B, MLA_HEADS * MLA_V, S), bf),
            scratch_shapes=_attn_scratch(MLA_HEADS, MLA_V, tq, tk),
            compiler_params=pltpu.CompilerParams(
                dimension_semantics=("arbitrary", "arbitrary"), vmem_limit_bytes=VMEM_LIMIT),
            name="mla_attn",
        )(qt, k_a.reshape(B, nk, tk, MLA_HEADS * LANES), vt)

        lam_init = 0.8 - 0.6 * math.exp(-0.3 * l)
        lam_spec = _const_spec((1, DIFF_D))
        o_b = pl.pallas_call(
            functools.partial(_diff_kernel, tq=tq, tk=tk, lam_init=lam_init),
            grid=(B, nq),
            in_specs=[
                lam_spec, lam_spec, lam_spec, lam_spec,
                _const_spec((DIFF_V, 1)),
                pl.BlockSpec((1, DIFF_HEADS * 2 * LANES, tq), lambda b, i: (b, 0, i)),
                pl.BlockSpec((1, nk, tk, DIFF_HEADS * LANES), lambda b, i: (b, 0, 0, 0)),
                pl.BlockSpec((1, nk, DIFF_HEADS * DIFF_V, tk), lambda b, i: (b, 0, 0, 0)),
            ],
            out_specs=pl.BlockSpec((1, DIFF_HEADS * DIFF_V, tq), lambda b, i: (b, 0, i)),
            out_shape=jax.ShapeDtypeStruct((B, DIFF_HEADS * DIFF_V, S), bf),
            scratch_shapes=_attn_scratch(2 * DIFF_HEADS, DIFF_V, tq, tk),
            compiler_params=pltpu.CompilerParams(
                dimension_semantics=("arbitrary", "arbitrary"), vmem_limit_bytes=VMEM_LIMIT),
            name="diff_attn",
        )(lambda_q1[l].reshape(1, -1), lambda_k1[l].reshape(1, -1),
          lambda_q2[l].reshape(1, -1), lambda_k2[l].reshape(1, -1),
          diff_subln_g[l].reshape(-1, 1), dqt, dk.reshape(B, nk, tk, DIFF_HEADS * LANES), dvt)

        n_a = MLA_HEADS * MLA_V
        x = pl.pallas_call(
            _ffn_kernel,
            grid=(B, S // tm_ffn),
            in_specs=[
                pl.BlockSpec((1, tm_ffn, D_MODEL), lambda b, s: (b, s, 0)),
                pl.BlockSpec((1, n_a, tm_ffn), lambda b, s: (b, 0, s)),
                pl.BlockSpec((1, D_MODEL - n_a, tm_ffn), lambda b, s: (b, 0, s)),
                _const_spec((n_a, D_MODEL)),
                _const_spec((D_MODEL - n_a, D_MODEL)),
                _const_spec((1, D_MODEL)),
                _const_spec((D_MODEL, D_FF)),
                _const_spec((D_MODEL, D_FF)),
                _const_spec((CONV_WIDTH, D_FF)),
                _const_spec((1, D_FF)),
                _const_spec((D_FF, D_MODEL)),
            ],
            out_specs=pl.BlockSpec((1, tm_ffn, D_MODEL), lambda b, s: (b, s, 0)),
            out_shape=jax.ShapeDtypeStruct((B, S, D_MODEL), x.dtype),
            scratch_shapes=[pltpu.VMEM((8, D_FF), jnp.float32),
                            pltpu.VMEM((tm_ffn, D_FF), bf)],
            compiler_params=pltpu.CompilerParams(
                dimension_semantics=("arbitrary", "arbitrary"), vmem_limit_bytes=VMEM_LIMIT),
            name="ffn",
        )(x, o_a, o_b, w_out[l][:n_a].astype(bf), w_out[l][n_a:].astype(bf),
          ffn_norm_g[l].reshape(1, -1), w_gate[l].astype(bf), w_up[l].astype(bf),
          conv_w[l], conv_b[l].reshape(1, -1), w_down[l].astype(bf))
    return x
```

```python
import functools
import math
from typing import NamedTuple

import jax
import jax.numpy as jnp
from jax import lax
from jax.experimental import pallas as pl
from jax.experimental.pallas import tpu as pltpu

D_MODEL = 1024
MLA_HEADS = 8
MLA_NOPE = 64
MLA_ROPE = 32
MLA_V = 64
MLA_QK = MLA_NOPE + MLA_ROPE
Q_RANK = 384
KV_RANK = 256
DIFF_HEADS = 4
DIFF_D = 64
DIFF_V = 2 * DIFF_D
D_FF = 2816
CONV_WIDTH = 3
ROPE_THETA = 10000.0
EPS = 1e-6
LANES = 128
SUM_ROWS = 16
LOG2E = math.log2(math.e)

R_Q = 0
R_KV = R_Q + Q_RANK
R_KPE = R_KV + KV_RANK
R_DQ = R_KPE + MLA_ROPE
R_DK = R_DQ + DIFF_HEADS * 2 * DIFF_D
R_DV = R_DK + DIFF_HEADS * 2 * DIFF_D
IN_COLS = R_DV + DIFF_HEADS * DIFF_V

VMEM_LIMIT = 56 * 1024 * 1024


def _rms_rows(x, g):
    ms = jnp.mean(x * x, axis=-1, keepdims=True)
    return x * lax.rsqrt(ms + EPS) * g


def _rms_cols(xt, n, scale):
    ms = jnp.sum(xt * xt, axis=0, keepdims=True) * (1.0 / n)
    return scale * lax.rsqrt(scale * scale * ms + EPS)


def _rope_cols(xt, cos, sin):
    half = xt.shape[0] // 2
    x1, x2 = xt[:half], xt[half:]
    return x1 * cos - x2 * sin, x2 * cos + x1 * sin


def _proj_kernel(x_ref, g_attn_ref, w_in_ref, g_qa_ref, w_q_ref, g_kva_ref, w_k_ref, w_v_ref,
                 g_q_ref, g_k_ref, g_dq_ref, g_dk_ref,
                 cos_a_ref, sin_a_ref, cos_b_ref, sin_b_ref,
                 qt_ref, k_ref, vt_ref, dqt_ref, dk_ref, dvt_ref):
    tm = x_ref.shape[1]
    bf = jnp.bfloat16
    x = x_ref[0]
    r_tok = lax.rsqrt(jnp.mean(x * x, axis=-1, keepdims=True) + EPS)
    r_tok = jnp.transpose(jnp.broadcast_to(r_tok, (tm, LANES)))[0:1, :]
    h = (x * g_attn_ref[...]).astype(bf)

    def in_proj(r0, r1):
        return lax.dot_general(w_in_ref[r0:r1, :], h, (((1,), (1,)), ((), ())),
                               preferred_element_type=jnp.float32)

    cos_a, sin_a = cos_a_ref[...], sin_a_ref[...]
    cos_b, sin_b = cos_b_ref[...], sin_b_ref[...]
    zeros_pad = jnp.zeros((LANES - MLA_QK, tm), jnp.float32)
    one = jnp.ones((1, tm), jnp.float32)

    def head_a(nope, pe_roped, g):
        nope = nope * _rms_cols(nope, MLA_NOPE, one) * g[:MLA_NOPE]
        return jnp.concatenate([nope, *pe_roped, zeros_pad], axis=0)

    def rope_a(pe, g, scale):
        pe = pe * _rms_cols(pe, MLA_ROPE, scale) * g[MLA_NOPE:]
        return _rope_cols(pe, cos_a, sin_a)

    def head_b(xt, g):
        xt = xt * _rms_cols(xt, DIFF_D, r_tok) * g
        return jnp.concatenate(_rope_cols(xt, cos_b, sin_b), axis=0)

    lat = in_proj(R_Q, R_DQ)
    dq = in_proj(R_DQ, R_DK)
    dk = in_proj(R_DK, R_DV)

    cq = lat[R_Q:R_Q + Q_RANK]
    cq = (cq * _rms_cols(cq, Q_RANK, r_tok) * g_qa_ref[...]).astype(bf)
    q = jnp.dot(w_q_ref[...], cq, preferred_element_type=jnp.float32)
    ckv = lat[R_KV:R_KV + KV_RANK]
    ckv = (ckv * _rms_cols(ckv, KV_RANK, r_tok) * g_kva_ref[...]).astype(bf)
    kn = jnp.dot(w_k_ref[...], ckv, preferred_element_type=jnp.float32)
    vt_ref[0, 0] = jnp.dot(w_v_ref[...], ckv, preferred_element_type=jnp.float32).astype(bf)
    dvt_ref[0, 0] = (in_proj(R_DV, IN_COLS) * r_tok).astype(bf)

    g_q = g_q_ref[...] * (MLA_QK ** -0.5 * LOG2E)
    for hd in range(MLA_HEADS):
        r0 = hd * MLA_QK
        qt_ref[0, hd * LANES:(hd + 1) * LANES, :] = head_a(
            q[r0:r0 + MLA_NOPE], rope_a(q[r0 + MLA_NOPE:r0 + MLA_QK], g_q, one), g_q).astype(bf)

    g_k = g_k_ref[...]
    kpe = rope_a(lat[R_KPE:R_KPE + MLA_ROPE], g_k, r_tok)
    for hd in range(MLA_HEADS):
        kt = head_a(kn[hd * MLA_NOPE:(hd + 1) * MLA_NOPE], kpe, g_k)
        k_ref[0, :, hd * LANES:(hd + 1) * LANES] = kt.T.astype(bf)

    g_dq = g_dq_ref[...] * (DIFF_D ** -0.5 * LOG2E)
    g_dk = g_dk_ref[...]
    zeros_half = jnp.zeros((DIFF_D, tm), bf)
    for hd in range(DIFF_HEADS):
        r0 = hd * 2 * DIFF_D
        q1 = head_b(dq[r0:r0 + DIFF_D], g_dq).astype(bf)
        q2 = head_b(dq[r0 + DIFF_D:r0 + 2 * DIFF_D], g_dq).astype(bf)
        b0 = 2 * hd * LANES
        dqt_ref[0, b0:b0 + LANES, :] = jnp.concatenate([q1, zeros_half], axis=0)
        dqt_ref[0, b0 + LANES:b0 + 2 * LANES, :] = jnp.concatenate([zeros_half, q2], axis=0)
        dkt = jnp.concatenate([head_b(dk[r0:r0 + DIFF_D], g_dk),
                               head_b(dk[r0 + DIFF_D:r0 + 2 * DIFF_D], g_dk)], axis=0)
        dk_ref[0, :, hd * LANES:(hd + 1) * LANES] = dkt.T.astype(bf)


def _causal_mask(s_t, tk, tq):
    key = lax.broadcasted_iota(jnp.int32, (tk, tq), 0)
    qry = lax.broadcasted_iota(jnp.int32, (tk, tq), 1)
    return jnp.where(key <= qry, s_t, -jnp.inf)


class _Chain(NamedTuple):
    k_lanes: slice
    q_rows: slice
    v_rows: slice
    acc_rows: slice


def _pipelined_sweep(chains, qt_ref, k_ref, vt_ref, qi, tq, tk, p_ref, al_ref, m_ref, acc_ref):
    assert tq == 2 * tk
    full, half = slice(0, tq), slice(tk, tq)

    def stage_a(j, cols):
        return [jnp.dot(k_ref[0, j, :, ch.k_lanes], qt_ref[0, ch.q_rows, cols],
                        preferred_element_type=jnp.float32) for ch in chains]

    def stage_b(s_all, slot, cols, masked):
        for c, s_t in enumerate(s_all):
            if masked:
                s_t = _causal_mask(s_t, tk, cols.stop - cols.start)
            m_old = m_ref[c, :, cols]
            m_new = jnp.maximum(m_old, jnp.max(s_t, axis=0, keepdims=True))
            m_ref[c, :, cols] = m_new
            al_ref[slot, c, :, cols] = jnp.exp2(m_old - m_new)
            p_ref[slot, c, :, cols] = jnp.exp2(s_t - m_new).astype(p_ref.dtype)

    def stage_c(j, slot, cols):
        ones = jnp.ones((SUM_ROWS, tk), vt_ref.dtype)
        for c, ch in enumerate(chains):
            vt_ones = jnp.concatenate([vt_ref[0, j, ch.v_rows, :], ones], axis=0)
            acc_ref[ch.acc_rows, cols] = (
                al_ref[slot, c, :, cols] * acc_ref[ch.acc_rows, cols]
                + jnp.dot(vt_ones, p_ref[slot, c, :, cols], preferred_element_type=jnp.float32))

    m_ref[...] = jnp.full(m_ref.shape, -jnp.inf, m_ref.dtype)
    acc_ref[...] = jnp.zeros_like(acc_ref)
    p_ref[1] = jnp.zeros(p_ref.shape[1:], p_ref.dtype)
    al_ref[1] = jnp.ones(al_ref.shape[1:], al_ref.dtype)

    def step(j, slot, cols, masked):
        s_all = stage_a(j, cols)
        stage_c(jnp.maximum(j - 1, 0), 1 - slot, full)
        stage_b(s_all, slot, cols, masked)

    def two_steps(k, carry):
        step(2 * k, 0, full, False)
        step(2 * k + 1, 1, full, False)
        return carry

    lax.fori_loop(0, qi, two_steps, 0)
    step(2 * qi, 0, full, True)
    step(2 * qi + 1, 1, half, True)
    stage_c(2 * qi + 1, 1, half)


def _mla_kernel(qt_ref, k_ref, vt_ref, o_ref, p_ref, al_ref, m_ref, acc_ref, *, tq, tk):
    acc_rows = MLA_V + SUM_ROWS
    chains = [_Chain(slice(hd * LANES, (hd + 1) * LANES), slice(hd * LANES, (hd + 1) * LANES),
                     slice(hd * MLA_V, (hd + 1) * MLA_V), slice(hd * acc_rows, (hd + 1) * acc_rows))
              for hd in range(MLA_HEADS)]
    _pipelined_sweep(chains, qt_ref, k_ref, vt_ref, pl.program_id(1), tq, tk,
                     p_ref, al_ref, m_ref, acc_ref)
    for hd in range(MLA_HEADS):
        a0 = hd * acc_rows
        o_ref[0, hd * MLA_V:(hd + 1) * MLA_V, :] = (
            acc_ref[a0:a0 + MLA_V, :] / acc_ref[a0 + MLA_V:a0 + MLA_V + 1, :]).astype(o_ref.dtype)


def _diff_kernel(lq1_ref, lk1_ref, lq2_ref, lk2_ref, g_sub_ref, dqt_ref, dk_ref, dvt_ref, o_ref,
                 p_ref, al_ref, m_ref, acc_ref, *, tq, tk, lam_init):
    acc_rows = DIFF_V + SUM_ROWS
    lam = (jnp.exp(jnp.sum(lq1_ref[...] * lk1_ref[...], axis=-1, keepdims=True))
           - jnp.exp(jnp.sum(lq2_ref[...] * lk2_ref[...], axis=-1, keepdims=True))
           + lam_init)
    chains = [_Chain(slice((c // 2) * LANES, (c // 2 + 1) * LANES), slice(c * LANES, (c + 1) * LANES),
                     slice((c // 2) * DIFF_V, (c // 2 + 1) * DIFF_V),
                     slice(c * acc_rows, (c + 1) * acc_rows))
              for c in range(2 * DIFF_HEADS)]
    _pipelined_sweep(chains, dqt_ref, dk_ref, dvt_ref, pl.program_id(1), tq, tk,
                     p_ref, al_ref, m_ref, acc_ref)

    def normalised(c):
        a0 = c * acc_rows
        return acc_ref[a0:a0 + DIFF_V, :] / acc_ref[a0 + DIFF_V:a0 + DIFF_V + 1, :]

    for hd in range(DIFF_HEADS):
        o = normalised(2 * hd) - lam * normalised(2 * hd + 1)
        ms = jnp.mean(o * o, axis=0, keepdims=True)
        o = o * lax.rsqrt(ms + EPS) * g_sub_ref[...] * (1.0 - lam_init)
        o_ref[0, hd * DIFF_V:(hd + 1) * DIFF_V, :] = o.astype(o_ref.dtype)


FF_CHUNK = 512


def _ffn_kernel(x_ref, oa_ref, ob_ref, w_oa_ref, w_ob_ref, g_ffn_ref, w_gate_ref, w_up_ref,
                conv_w_ref, conv_b_ref, w_down_ref, out_ref, prev_ref, y_ref):
    si = pl.program_id(1)
    tm = x_ref.shape[1]
    contract0 = (((0,), (0,)), ((), ()))
    mix = (lax.dot_general(oa_ref[0], w_oa_ref[...], contract0, preferred_element_type=jnp.float32)
           + lax.dot_general(ob_ref[0], w_ob_ref[...], contract0,
                             preferred_element_type=jnp.float32))
    x1 = x_ref[0] + mix
    h = _rms_rows(x1, g_ffn_ref[...]).astype(jnp.bfloat16)

    @pl.when(si == 0)
    def _():
        prev_ref[...] = jnp.zeros_like(prev_ref)

    for c0 in range(0, D_FF, FF_CHUNK):
        cw = min(FF_CHUNK, D_FF - c0)
        g = jnp.dot(h, w_gate_ref[:, c0:c0 + cw], preferred_element_type=jnp.float32)
        u = jnp.dot(h, w_up_ref[:, c0:c0 + cw], preferred_element_type=jnp.float32)
        row = lax.broadcasted_iota(jnp.int32, (tm, cw), 0)
        p1 = prev_ref[7:8, c0:c0 + cw]
        p2 = prev_ref[6:7, c0:c0 + cw]
        g1 = jnp.where(row == 0, p1, pltpu.roll(g, 1, axis=0))
        g2 = jnp.where(row == 0, p2, jnp.where(row == 1, p1, pltpu.roll(g, 2, axis=0)))
        prev_ref[:, c0:c0 + cw] = g[tm - 8:tm, :]
        cg = (conv_b_ref[:, c0:c0 + cw] + g2 * conv_w_ref[0:1, c0:c0 + cw]
              + g1 * conv_w_ref[1:2, c0:c0 + cw] + g * conv_w_ref[2:3, c0:c0 + cw])
        y_ref[:, c0:c0 + cw] = (jax.nn.silu(cg) * u).astype(y_ref.dtype)

    out_ref[0] = x1 + jnp.dot(y_ref[...], w_down_ref[...], preferred_element_type=jnp.float32)


def _rope_tables(seq):
    pos = jnp.arange(seq, dtype=jnp.float32)[:, None]

    def tables(dim):
        inv = 1.0 / (ROPE_THETA ** (jnp.arange(0, dim, 2, dtype=jnp.float32) / dim))
        ang = pos * inv[None, :]
        return jnp.cos(ang), jnp.sin(ang)

    ca, sa = tables(MLA_ROPE)
    cb, sb = tables(DIFF_D)
    return ca.T, sa.T, cb.T, sb.T


def _gain_cols(g, tokens):
    return jnp.broadcast_to(g[:, None], (g.shape[0], tokens))


def _attn_scratch(n_chains, dv, tq, tk):
    return [pltpu.VMEM((2, n_chains, tk, tq), jnp.bfloat16),
            pltpu.VMEM((2, n_chains, 1, tq), jnp.float32),
            pltpu.VMEM((n_chains, 1, tq), jnp.float32),
            pltpu.VMEM((n_chains * (dv + SUM_ROWS), tq), jnp.float32)]


def _const_spec(shape):
    return pl.BlockSpec(shape, lambda *_: (0,) * len(shape))


def kernel(x, attn_norm_g, w_in, q_a_norm_g, w_q_up, kv_a_norm_g, w_kv_up, mla_q_norm_g,
           mla_k_norm_g, diff_q_norm_g, diff_k_norm_g, lambda_q1, lambda_k1, lambda_q2, lambda_k2,
           diff_subln_g, w_out, ffn_norm_g, w_gate, w_up, conv_w, conv_b, w_down):
    B, S, _ = x.shape
    depth = w_in.shape[0]
    bf = jnp.bfloat16
    cos_a, sin_a, cos_b, sin_b = _rope_tables(S)

    tm_proj = 256
    tk = tm_proj
    tq = 2 * tk
    tm_ffn = 512
    assert S % tq == 0 and S % tm_ffn == 0
    nk = S // tk

    for l in range(depth):
        w_in_t = w_in[l].T.astype(bf)
        w_q_t = w_q_up[l].T.astype(bf)
        wkv_t = w_kv_up[l].T.reshape(MLA_HEADS, MLA_NOPE + MLA_V, KV_RANK)
        w_k_t = wkv_t[:, :MLA_NOPE].reshape(MLA_HEADS * MLA_NOPE, KV_RANK).astype(bf)
        w_v_t = wkv_t[:, MLA_NOPE:].reshape(MLA_HEADS * MLA_V, KV_RANK).astype(bf)

        n_tok_tiles = S // tm_proj
        tok3 = lambda b, s: (b, s, 0)
        feat3 = lambda b, s: (b, 0, s)
        blk4 = lambda b, s: (b, s, 0, 0)
        rope_a_spec = pl.BlockSpec((MLA_ROPE // 2, tm_proj), lambda b, s: (0, s))
        rope_b_spec = pl.BlockSpec((DIFF_D // 2, tm_proj), lambda b, s: (0, s))
        qt, k_a, vt, dqt, dk, dvt = pl.pallas_call(
            _proj_kernel,
            grid=(B, n_tok_tiles),
            in_specs=[
                pl.BlockSpec((1, tm_proj, D_MODEL), tok3),
                _const_spec((1, D_MODEL)),
                _const_spec((IN_COLS, D_MODEL)),
                _const_spec((Q_RANK, tm_proj)),
                _const_spec((MLA_HEADS * MLA_QK, Q_RANK)),
                _const_spec((KV_RANK, tm_proj)),
                _const_spec((MLA_HEADS * MLA_NOPE, KV_RANK)),
                _const_spec((MLA_HEADS * MLA_V, KV_RANK)),
                _const_spec((MLA_QK, tm_proj)), _const_spec((MLA_QK, tm_proj)),
                _const_spec((DIFF_D, tm_proj)), _const_spec((DIFF_D, tm_proj)),
                rope_a_spec, rope_a_spec, rope_b_spec, rope_b_spec,
            ],
            out_specs=[
                pl.BlockSpec((1, MLA_HEADS * LANES, tm_proj), feat3),
                pl.BlockSpec((1, tm_proj, MLA_HEADS * LANES), tok3),
                pl.BlockSpec((1, 1, MLA_HEADS * MLA_V, tm_proj), blk4),
                pl.BlockSpec((1, DIFF_HEADS * 2 * LANES, tm_proj), feat3),
                pl.BlockSpec((1, tm_proj, DIFF_HEADS * LANES), tok3),
                pl.BlockSpec((1, 1, DIFF_HEADS * DIFF_V, tm_proj), blk4),
            ],
            out_shape=[
                jax.ShapeDtypeStruct((B, MLA_HEADS * LANES, S), bf),
                jax.ShapeDtypeStruct((B, S, MLA_HEADS * LANES), bf),
                jax.ShapeDtypeStruct((B, n_tok_tiles, MLA_HEADS * MLA_V, tm_proj), bf),
                jax.ShapeDtypeStruct((B, DIFF_HEADS * 2 * LANES, S), bf),
                jax.ShapeDtypeStruct((B, S, DIFF_HEADS * LANES), bf),
                jax.ShapeDtypeStruct((B, n_tok_tiles, DIFF_HEADS * DIFF_V, tm_proj), bf),
            ],
            compiler_params=pltpu.CompilerParams(
                dimension_semantics=("arbitrary", "arbitrary"), vmem_limit_bytes=VMEM_LIMIT),
            name="proj",
        )(x, attn_norm_g[l].reshape(1, -1), w_in_t, _gain_cols(q_a_norm_g[l], tm_proj), w_q_t,
          _gain_cols(kv_a_norm_g[l], tm_proj), w_k_t, w_v_t,
          _gain_cols(mla_q_norm_g[l], tm_proj), _gain_cols(mla_k_norm_g[l], tm_proj),
          _gain_cols(diff_q_norm_g[l], tm_proj), _gain_cols(diff_k_norm_g[l], tm_proj),
          cos_a, sin_a, cos_b, sin_b)

        nq = S // tq
        o_a = pl.pallas_call(
            functools.partial(_mla_kernel, tq=tq, tk=tk),
            grid=(B, nq),
            in_specs=[
                pl.BlockSpec((1, MLA_HEADS * LANES, tq), lambda b, i: (b, 0, i)),
                pl.BlockSpec((1, nk, tk, MLA_HEADS * LANES), lambda b, i: (b, 0, 0, 0)),
                pl.BlockSpec((1, nk, MLA_HEADS * MLA_V, tk), lambda b, i: (b, 0, 0, 0)),
            ],
            out_specs=pl.BlockSpec((1, MLA_HEADS * MLA_V, tq), lambda b, i: (b, 0, i)),
            out_shape=jax.ShapeDtypeStruct((B, MLA_HEADS * MLA_V, S), bf),
            scratch_shapes=_attn_scratch(MLA_HEADS, MLA_V, tq, tk),
            compiler_params=pltpu.CompilerParams(
                dimension_semantics=("arbitrary", "arbitrary"), vmem_limit_bytes=VMEM_LIMIT),
            name="mla_attn",
        )(qt, k_a.reshape(B, nk, tk, MLA_HEADS * LANES), vt)

        lam_init = 0.8 - 0.6 * math.exp(-0.3 * l)
        lam_spec = _const_spec((1, DIFF_D))
        o_b = pl.pallas_call(
            functools.partial(_diff_kernel, tq=tq, tk=tk, lam_init=lam_init),
            grid=(B, nq),
            in_specs=[
                lam_spec, lam_spec, lam_spec, lam_spec,
                _const_spec((DIFF_V, 1)),
                pl.BlockSpec((1, DIFF_HEADS * 2 * LANES, tq), lambda b, i: (b, 0, i)),
                pl.BlockSpec((1, nk, tk, DIFF_HEADS * LANES), lambda b, i: (b, 0, 0, 0)),
                pl.BlockSpec((1, nk, DIFF_HEADS * DIFF_V, tk), lambda b, i: (b, 0, 0, 0)),
            ],
            out_specs=pl.BlockSpec((1, DIFF_HEADS * DIFF_V, tq), lambda b, i: (b, 0, i)),
            out_shape=jax.ShapeDtypeStruct((B, DIFF_HEADS * DIFF_V, S), bf),
            scratch_shapes=_attn_scratch(2 * DIFF_HEADS, DIFF_V, tq, tk),
            compiler_params=pltpu.CompilerParams(
                dimension_semantics=("arbitrary", "arbitrary"), vmem_limit_bytes=VMEM_LIMIT),
            name="diff_attn",
        )(lambda_q1[l].reshape(1, -1), lambda_k1[l].reshape(1, -1),
          lambda_q2[l].reshape(1, -1), lambda_k2[l].reshape(1, -1),
          diff_subln_g[l].reshape(-1, 1), dqt, dk.reshape(B, nk, tk, DIFF_HEADS * LANES), dvt)

        n_a = MLA_HEADS * MLA_V
        x = pl.pallas_call(
            _ffn_kernel,
            grid=(B, S // tm_ffn),
            in_specs=[
                pl.BlockSpec((1, tm_ffn, D_MODEL), lambda b, s: (b, s, 0)),
                pl.BlockSpec((1, n_a, tm_ffn), lambda b, s: (b, 0, s)),
                pl.BlockSpec((1, D_MODEL - n_a, tm_ffn), lambda b, s: (b, 0, s)),
                _const_spec((n_a, D_MODEL)),
                _const_spec((D_MODEL - n_a, D_MODEL)),
                _const_spec((1, D_MODEL)),
                _const_spec((D_MODEL, D_FF)),
                _const_spec((D_MODEL, D_FF)),
                _const_spec((CONV_WIDTH, D_FF)),
                _const_spec((1, D_FF)),
                _const_spec((D_FF, D_MODEL)),
            ],
            out_specs=pl.BlockSpec((1, tm_ffn, D_MODEL), lambda b, s: (b, s, 0)),
            out_shape=jax.ShapeDtypeStruct((B, S, D_MODEL), x.dtype),
            scratch_shapes=[pltpu.VMEM((8, D_FF), jnp.float32),
                            pltpu.VMEM((tm_ffn, D_FF), bf)],
            compiler_params=pltpu.CompilerParams(
                dimension_semantics=("arbitrary", "arbitrary"), vmem_limit_bytes=VMEM_LIMIT),
            name="ffn",
        )(x, o_a, o_b, w_out[l][:n_a].astype(bf), w_out[l][n_a:].astype(bf),
          ffn_norm_g[l].reshape(1, -1), w_gate[l].astype(bf), w_up[l].astype(bf),
          conv_w[l], conv_b[l].reshape(1, -1), w_down[l].astype(bf))
    return x
```

```python
import functools
import math
from typing import NamedTuple

import jax
import jax.numpy as jnp
from jax import lax
from jax.experimental import pallas as pl
from jax.experimental.pallas import tpu as pltpu

D_MODEL = 1024
MLA_HEADS = 8
MLA_NOPE = 64
MLA_ROPE = 32
MLA_V = 64
MLA_QK = MLA_NOPE + MLA_ROPE
Q_RANK = 384
KV_RANK = 256
DIFF_HEADS = 4
DIFF_D = 64
DIFF_V = 2 * DIFF_D
D_FF = 2816
CONV_WIDTH = 3
ROPE_THETA = 10000.0
EPS = 1e-6
LANES = 128
SUM_ROWS = 16
LOG2E = math.log2(math.e)

R_Q = 0
R_KV = R_Q + Q_RANK
R_KPE = R_KV + KV_RANK
R_DQ = R_KPE + MLA_ROPE
R_DK = R_DQ + DIFF_HEADS * 2 * DIFF_D
R_DV = R_DK + DIFF_HEADS * 2 * DIFF_D
IN_COLS = R_DV + DIFF_HEADS * DIFF_V

VMEM_LIMIT = 56 * 1024 * 1024


def _rms_rows(x, g):
    ms = jnp.mean(x * x, axis=-1, keepdims=True)
    return x * lax.rsqrt(ms + EPS) * g


def _rms_cols(xt, n, scale):
    ms = jnp.sum(xt * xt, axis=0, keepdims=True) * (1.0 / n)
    return scale * lax.rsqrt(scale * scale * ms + EPS)


def _rope_cols(xt, cos, sin):
    half = xt.shape[0] // 2
    x1, x2 = xt[:half], xt[half:]
    return x1 * cos - x2 * sin, x2 * cos + x1 * sin


def _proj_kernel(x_ref, g_attn_ref, w_in_ref, g_qa_ref, w_q_ref, g_kva_ref, w_k_ref, w_v_ref,
                 g_q_ref, g_k_ref, g_dq_ref, g_dk_ref,
                 cos_a_ref, sin_a_ref, cos_b_ref, sin_b_ref,
                 qt_ref, k_ref, vt_ref, dqt_ref, dk_ref, dvt_ref):
    tm = x_ref.shape[1]
    bf = jnp.bfloat16
    x = x_ref[0]
    r_tok = lax.rsqrt(jnp.mean(x * x, axis=-1, keepdims=True) + EPS)
    r_tok = jnp.transpose(jnp.broadcast_to(r_tok, (tm, LANES)))[0:1, :]
    h = (x * g_attn_ref[...]).astype(bf)

    def in_proj(r0, r1):
        return lax.dot_general(w_in_ref[r0:r1, :], h, (((1,), (1,)), ((), ())),
                               preferred_element_type=jnp.float32)

    cos_a, sin_a = cos_a_ref[...], sin_a_ref[...]
    cos_b, sin_b = cos_b_ref[...], sin_b_ref[...]
    zeros_pad = jnp.zeros((LANES - MLA_QK, tm), jnp.float32)
    one = jnp.ones((1, tm), jnp.float32)

    def head_a(nope, pe_roped, g):
        nope = nope * _rms_cols(nope, MLA_NOPE, one) * g[:MLA_NOPE]
        return jnp.concatenate([nope, *pe_roped, zeros_pad], axis=0)

    def rope_a(pe, g, scale):
        pe = pe * _rms_cols(pe, MLA_ROPE, scale) * g[MLA_NOPE:]
        return _rope_cols(pe, cos_a, sin_a)

    def head_b(xt, g):
        xt = xt * _rms_cols(xt, DIFF_D, r_tok) * g
        return jnp.concatenate(_rope_cols(xt, cos_b, sin_b), axis=0)

    lat = in_proj(R_Q, R_DQ)
    dq = in_proj(R_DQ, R_DK)
    dk = in_proj(R_DK, R_DV)

    cq = lat[R_Q:R_Q + Q_RANK]
    cq = (cq * _rms_cols(cq, Q_RANK, r_tok) * g_qa_ref[...]).astype(bf)
    q = jnp.dot(w_q_ref[...], cq, preferred_element_type=jnp.float32)
    ckv = lat[R_KV:R_KV + KV_RANK]
    ckv = (ckv * _rms_cols(ckv, KV_RANK, r_tok) * g_kva_ref[...]).astype(bf)
    kn = jnp.dot(w_k_ref[...], ckv, preferred_element_type=jnp.float32)
    vt_ref[0, 0] = jnp.dot(w_v_ref[...], ckv, preferred_element_type=jnp.float32).astype(bf)
    dvt_ref[0, 0] = (in_proj(R_DV, IN_COLS) * r_tok).astype(bf)

    g_q = g_q_ref[...] * (MLA_QK ** -0.5 * LOG2E)
    for hd in range(MLA_HEADS):
        r0 = hd * MLA_QK
        qt_ref[0, hd * LANES:(hd + 1) * LANES, :] = head_a(
            q[r0:r0 + MLA_NOPE], rope_a(q[r0 + MLA_NOPE:r0 + MLA_QK], g_q, one), g_q).astype(bf)

    g_k = g_k_ref[...]
    kpe = rope_a(lat[R_KPE:R_KPE + MLA_ROPE], g_k, r_tok)
    for hd in range(MLA_HEADS):
        kt = head_a(kn[hd * MLA_NOPE:(hd + 1) * MLA_NOPE], kpe, g_k)
        k_ref[0, :, hd * LANES:(hd + 1) * LANES] = kt.T.astype(bf)

    g_dq = g_dq_ref[...] * (DIFF_D ** -0.5 * LOG2E)
    g_dk = g_dk_ref[...]
    zeros_half = jnp.zeros((DIFF_D, tm), bf)
    for hd in range(DIFF_HEADS):
        r0 = hd * 2 * DIFF_D
        q1 = head_b(dq[r0:r0 + DIFF_D], g_dq).astype(bf)
        q2 = head_b(dq[r0 + DIFF_D:r0 + 2 * DIFF_D], g_dq).astype(bf)
        b0 = 2 * hd * LANES
        dqt_ref[0, b0:b0 + LANES, :] = jnp.concatenate([q1, zeros_half], axis=0)
        dqt_ref[0, b0 + LANES:b0 + 2 * LANES, :] = jnp.concatenate([zeros_half, q2], axis=0)
        dkt = jnp.concatenate([head_b(dk[r0:r0 + DIFF_D], g_dk),
                               head_b(dk[r0 + DIFF_D:r0 + 2 * DIFF_D], g_dk)], axis=0)
        dk_ref[0, :, hd * LANES:(hd + 1) * LANES] = dkt.T.astype(bf)


def _causal_mask(s_t, tk, tq):
    key = lax.broadcasted_iota(jnp.int32, (tk, tq), 0)
    qry = lax.broadcasted_iota(jnp.int32, (tk, tq), 1)
    return jnp.where(key <= qry, s_t, -jnp.inf)


class _Chain(NamedTuple):
    k_lanes: slice
    q_rows: slice
    v_rows: slice
    acc_rows: slice


def _pipelined_sweep(chains, qt_ref, k_ref, vt_ref, qi, tq, tk, p_ref, al_ref, m_ref, acc_ref):
    assert tq == 2 * tk
    lower, upper = slice(0, tk), slice(tk, tq)

    def stage_a(j, cols):
        return [jnp.dot(k_ref[0, j, :, ch.k_lanes], qt_ref[0, ch.q_rows, cols],
                        preferred_element_type=jnp.float32) for ch in chains]

    def stage_b(s_all, slot, cols, first=False):
        for c, s_t in enumerate(s_all):
            if first:
                m_new = jnp.max(s_t, axis=0, keepdims=True)
            else:
                m_old = m_ref[c, :, cols]
                m_new = jnp.maximum(m_old, jnp.max(s_t, axis=0, keepdims=True))
                al_ref[slot, c, :, cols] = jnp.exp2(m_old - m_new)
            m_ref[c, :, cols] = m_new
            p_ref[slot, c, :, cols] = jnp.exp2(s_t - m_new).astype(p_ref.dtype)

    def stage_c(j, slot, first=False):
        ones = jnp.ones((SUM_ROWS, tk), vt_ref.dtype)
        for c, ch in enumerate(chains):
            vt_ones = jnp.concatenate([vt_ref[0, j, ch.v_rows, :], ones], axis=0)
            pv = jnp.dot(vt_ones, p_ref[slot, c], preferred_element_type=jnp.float32)
            if first:
                acc_ref[ch.acc_rows, :] = pv
            else:
                acc_ref[ch.acc_rows, :] = al_ref[slot, c] * acc_ref[ch.acc_rows, :] + pv

    d0, d1 = 2 * qi, 2 * qi + 1
    s_all = stage_a(d0, slice(0, tq))
    stage_b([jnp.concatenate([_causal_mask(s_t[:, lower], tk, tk), s_t[:, upper]], axis=1)
             for s_t in s_all], 0, slice(0, tq), first=True)

    s_all = stage_a(d1, upper)
    stage_c(d0, 0, first=True)
    stage_b([_causal_mask(s_t, tk, tk) for s_t in s_all], 1, upper)
    p_ref[1, :, :, lower] = jnp.zeros((len(chains), tk, tk), p_ref.dtype)
    al_ref[1, :, :, lower] = jnp.ones((len(chains), 1, tk), al_ref.dtype)

    def two_steps(k, carry):
        s_all = stage_a(2 * k, slice(0, tq))
        stage_c(jnp.where(k == 0, d1, 2 * k - 1), 1)
        stage_b(s_all, 0, slice(0, tq))
        s_all = stage_a(2 * k + 1, slice(0, tq))
        stage_c(2 * k, 0)
        stage_b(s_all, 1, slice(0, tq))
        return carry

    lax.fori_loop(0, qi, two_steps, 0)
    stage_c(jnp.where(qi == 0, d1, 2 * qi - 1), 1)


def _mla_kernel(qt_ref, k_ref, vt_ref, o_ref, p_ref, al_ref, m_ref, acc_ref, *, tq, tk):
    acc_rows = MLA_V + SUM_ROWS
    chains = [_Chain(slice(hd * LANES, (hd + 1) * LANES), slice(hd * LANES, (hd + 1) * LANES),
                     slice(hd * MLA_V, (hd + 1) * MLA_V), slice(hd * acc_rows, (hd + 1) * acc_rows))
              for hd in range(MLA_HEADS)]
    _pipelined_sweep(chains, qt_ref, k_ref, vt_ref, pl.program_id(1), tq, tk,
                     p_ref, al_ref, m_ref, acc_ref)
    for hd in range(MLA_HEADS):
        a0 = hd * acc_rows
        o_ref[0, hd * MLA_V:(hd + 1) * MLA_V, :] = (
            acc_ref[a0:a0 + MLA_V, :] / acc_ref[a0 + MLA_V:a0 + MLA_V + 1, :]).astype(o_ref.dtype)


def _diff_kernel(lq1_ref, lk1_ref, lq2_ref, lk2_ref, g_sub_ref, dqt_ref, dk_ref, dvt_ref, o_ref,
                 p_ref, al_ref, m_ref, acc_ref, *, tq, tk, lam_init):
    acc_rows = DIFF_V + SUM_ROWS
    lam = (jnp.exp(jnp.sum(lq1_ref[...] * lk1_ref[...], axis=-1, keepdims=True))
           - jnp.exp(jnp.sum(lq2_ref[...] * lk2_ref[...], axis=-1, keepdims=True))
           + lam_init)
    chains = [_Chain(slice((c // 2) * LANES, (c // 2 + 1) * LANES), slice(c * LANES, (c + 1) * LANES),
                     slice((c // 2) * DIFF_V, (c // 2 + 1) * DIFF_V),
                     slice(c * acc_rows, (c + 1) * acc_rows))
              for c in range(2 * DIFF_HEADS)]
    _pipelined_sweep(chains, dqt_ref, dk_ref, dvt_ref, pl.program_id(1), tq, tk,
                     p_ref, al_ref, m_ref, acc_ref)

    def normalised(c):
        a0 = c * acc_rows
        return acc_ref[a0:a0 + DIFF_V, :] / acc_ref[a0 + DIFF_V:a0 + DIFF_V + 1, :]

    for hd in range(DIFF_HEADS):
        o = normalised(2 * hd) - lam * normalised(2 * hd + 1)
        ms = jnp.mean(o * o, axis=0, keepdims=True)
        o = o * lax.rsqrt(ms + EPS) * g_sub_ref[...] * (1.0 - lam_init)
        o_ref[0, hd * DIFF_V:(hd + 1) * DIFF_V, :] = o.astype(o_ref.dtype)


FF_CHUNK = 512


def _ffn_kernel(x_ref, oa_ref, ob_ref, w_oa_ref, w_ob_ref, g_ffn_ref, w_gate_ref, w_up_ref,
                conv_w_ref, conv_b_ref, w_down_ref, out_ref, prev_ref, y_ref):
    si = pl.program_id(1)
    tm = x_ref.shape[1]
    contract0 = (((0,), (0,)), ((), ()))
    mix = (lax.dot_general(oa_ref[0], w_oa_ref[...], contract0, preferred_element_type=jnp.float32)
           + lax.dot_general(ob_ref[0], w_ob_ref[...], contract0,
                             preferred_element_type=jnp.float32))
    x1 = x_ref[0] + mix
    h = _rms_rows(x1, g_ffn_ref[...]).astype(jnp.bfloat16)

    @pl.when(si == 0)
    def _():
        prev_ref[...] = jnp.zeros_like(prev_ref)

    for c0 in range(0, D_FF, FF_CHUNK):
        cw = min(FF_CHUNK, D_FF - c0)
        g = jnp.dot(h, w_gate_ref[:, c0:c0 + cw], preferred_element_type=jnp.float32)
        u = jnp.dot(h, w_up_ref[:, c0:c0 + cw], preferred_element_type=jnp.float32)
        row = lax.broadcasted_iota(jnp.int32, (tm, cw), 0)
        p1 = prev_ref[7:8, c0:c0 + cw]
        p2 = prev_ref[6:7, c0:c0 + cw]
        g1 = jnp.where(row == 0, p1, pltpu.roll(g, 1, axis=0))
        g2 = jnp.where(row == 0, p2, jnp.where(row == 1, p1, pltpu.roll(g, 2, axis=0)))
        prev_ref[:, c0:c0 + cw] = g[tm - 8:tm, :]
        cg = (conv_b_ref[:, c0:c0 + cw] + g2 * conv_w_ref[0:1, c0:c0 + cw]
              + g1 * conv_w_ref[1:2, c0:c0 + cw] + g * conv_w_ref[2:3, c0:c0 + cw])
        y_ref[:, c0:c0 + cw] = (jax.nn.silu(cg) * u).astype(y_ref.dtype)

    out_ref[0] = x1 + jnp.dot(y_ref[...], w_down_ref[...], preferred_element_type=jnp.float32)


def _rope_tables(seq):
    pos = jnp.arange(seq, dtype=jnp.float32)[:, None]

    def tables(dim):
        inv = 1.0 / (ROPE_THETA ** (jnp.arange(0, dim, 2, dtype=jnp.float32) / dim))
        ang = pos * inv[None, :]
        return jnp.cos(ang), jnp.sin(ang)

    ca, sa = tables(MLA_ROPE)
    cb, sb = tables(DIFF_D)
    return ca.T, sa.T, cb.T, sb.T


def _gain_cols(g, tokens):
    return jnp.broadcast_to(g[:, None], (g.shape[0], tokens))


def _attn_scratch(n_chains, dv, tq, tk):
    return [pltpu.VMEM((2, n_chains, tk, tq), jnp.bfloat16),
            pltpu.VMEM((2, n_chains, 1, tq), jnp.float32),
            pltpu.VMEM((n_chains, 1, tq), jnp.float32),
            pltpu.VMEM((n_chains * (dv + SUM_ROWS), tq), jnp.float32)]


def _const_spec(shape):
    return pl.BlockSpec(shape, lambda *_: (0,) * len(shape))


def kernel(x, attn_norm_g, w_in, q_a_norm_g, w_q_up, kv_a_norm_g, w_kv_up, mla_q_norm_g,
           mla_k_norm_g, diff_q_norm_g, diff_k_norm_g, lambda_q1, lambda_k1, lambda_q2, lambda_k2,
           diff_subln_g, w_out, ffn_norm_g, w_gate, w_up, conv_w, conv_b, w_down):
    B, S, _ = x.shape
    depth = w_in.shape[0]
    bf = jnp.bfloat16
    cos_a, sin_a, cos_b, sin_b = _rope_tables(S)

    tm_proj = 256
    tk = tm_proj
    tq = 2 * tk
    tm_ffn = 512
    assert S % tq == 0 and S % tm_ffn == 0
    nk = S // tk

    for l in range(depth):
        w_in_t = w_in[l].T.astype(bf)
        w_q_t = w_q_up[l].T.astype(bf)
        wkv_t = w_kv_up[l].T.reshape(MLA_HEADS, MLA_NOPE + MLA_V, KV_RANK)
        w_k_t = wkv_t[:, :MLA_NOPE].reshape(MLA_HEADS * MLA_NOPE, KV_RANK).astype(bf)
        w_v_t = wkv_t[:, MLA_NOPE:].reshape(MLA_HEADS * MLA_V, KV_RANK).astype(bf)

        n_tok_tiles = S // tm_proj
        tok3 = lambda b, s: (b, s, 0)
        feat3 = lambda b, s: (b, 0, s)
        blk4 = lambda b, s: (b, s, 0, 0)
        rope_a_spec = pl.BlockSpec((MLA_ROPE // 2, tm_proj), lambda b, s: (0, s))
        rope_b_spec = pl.BlockSpec((DIFF_D // 2, tm_proj), lambda b, s: (0, s))
        qt, k_a, vt, dqt, dk, dvt = pl.pallas_call(
            _proj_kernel,
            grid=(B, n_tok_tiles),
            in_specs=[
                pl.BlockSpec((1, tm_proj, D_MODEL), tok3),
                _const_spec((1, D_MODEL)),
                _const_spec((IN_COLS, D_MODEL)),
                _const_spec((Q_RANK, tm_proj)),
                _const_spec((MLA_HEADS * MLA_QK, Q_RANK)),
                _const_spec((KV_RANK, tm_proj)),
                _const_spec((MLA_HEADS * MLA_NOPE, KV_RANK)),
                _const_spec((MLA_HEADS * MLA_V, KV_RANK)),
                _const_spec((MLA_QK, tm_proj)), _const_spec((MLA_QK, tm_proj)),
                _const_spec((DIFF_D, tm_proj)), _const_spec((DIFF_D, tm_proj)),
                rope_a_spec, rope_a_spec, rope_b_spec, rope_b_spec,
            ],
            out_specs=[
                pl.BlockSpec((1, MLA_HEADS * LANES, tm_proj), feat3),
                pl.BlockSpec((1, tm_proj, MLA_HEADS * LANES), tok3),
                pl.BlockSpec((1, 1, MLA_HEADS * MLA_V, tm_proj), blk4),
                pl.BlockSpec((1, DIFF_HEADS * 2 * LANES, tm_proj), feat3),
                pl.BlockSpec((1, tm_proj, DIFF_HEADS * LANES), tok3),
                pl.BlockSpec((1, 1, DIFF_HEADS * DIFF_V, tm_proj), blk4),
            ],
            out_shape=[
                jax.ShapeDtypeStruct((B, MLA_HEADS * LANES, S), bf),
                jax.ShapeDtypeStruct((B, S, MLA_HEADS * LANES), bf),
                jax.ShapeDtypeStruct((B, n_tok_tiles, MLA_HEADS * MLA_V, tm_proj), bf),
                jax.ShapeDtypeStruct((B, DIFF_HEADS * 2 * LANES, S), bf),
                jax.ShapeDtypeStruct((B, S, DIFF_HEADS * LANES), bf),
                jax.ShapeDtypeStruct((B, n_tok_tiles, DIFF_HEADS * DIFF_V, tm_proj), bf),
            ],
            compiler_params=pltpu.CompilerParams(
                dimension_semantics=("arbitrary", "arbitrary"), vmem_limit_bytes=VMEM_LIMIT),
            name="proj",
        )(x, attn_norm_g[l].reshape(1, -1), w_in_t, _gain_cols(q_a_norm_g[l], tm_proj), w_q_t,
          _gain_cols(kv_a_norm_g[l], tm_proj), w_k_t, w_v_t,
          _gain_cols(mla_q_norm_g[l], tm_proj), _gain_cols(mla_k_norm_g[l], tm_proj),
          _gain_cols(diff_q_norm_g[l], tm_proj), _gain_cols(diff_k_norm_g[l], tm_proj),
          cos_a, sin_a, cos_b, sin_b)

        nq = S // tq
        o_a = pl.pallas_call(
            functools.partial(_mla_kernel, tq=tq, tk=tk),
            grid=(B, nq),
            in_specs=[
                pl.BlockSpec((1, MLA_HEADS * LANES, tq), lambda b, i: (b, 0, i)),
                pl.BlockSpec((1, nk, tk, MLA_HEADS * LANES), lambda b, i: (b, 0, 0, 0)),
                pl.BlockSpec((1, nk, MLA_HEADS * MLA_V, tk), lambda b, i: (b, 0, 0, 0)),
            ],
            out_specs=pl.BlockSpec((1, MLA_HEADS * MLA_V, tq), lambda b, i: (b, 0, i)),
            out_shape=jax.ShapeDtypeStruct((B, MLA_HEADS * MLA_V, S), bf),
            scratch_shapes=_attn_scratch(MLA_HEADS, MLA_V, tq, tk),
            compiler_params=pltpu.CompilerParams(
                dimension_semantics=("arbitrary", "arbitrary"), vmem_limit_bytes=VMEM_LIMIT),
            name="mla_attn",
        )(qt, k_a.reshape(B, nk, tk, MLA_HEADS * LANES), vt)

        lam_init = 0.8 - 0.6 * math.exp(-0.3 * l)
        lam_spec = _const_spec((1, DIFF_D))
        o_b = pl.pallas_call(
            functools.partial(_diff_kernel, tq=tq, tk=tk, lam_init=lam_init),
            grid=(B, nq),
            in_specs=[
                lam_spec, lam_spec, lam_spec, lam_spec,
                _const_spec((DIFF_V, 1)),
                pl.BlockSpec((1, DIFF_HEADS * 2 * LANES, tq), lambda b, i: (b, 0, i)),
                pl.BlockSpec((1, nk, tk, DIFF_HEADS * LANES), lambda b, i: (b, 0, 0, 0)),
                pl.BlockSpec((1, nk, DIFF_HEADS * DIFF_V, tk), lambda b, i: (b, 0, 0, 0)),
            ],
            out_specs=pl.BlockSpec((1, DIFF_HEADS * DIFF_V, tq), lambda b, i: (b, 0, i)),
            out_shape=jax.ShapeDtypeStruct((B, DIFF_HEADS * DIFF_V, S), bf),
            scratch_shapes=_attn_scratch(2 * DIFF_HEADS, DIFF_V, tq, tk),
            compiler_params=pltpu.CompilerParams(
                dimension_semantics=("arbitrary", "arbitrary"), vmem_limit_bytes=VMEM_LIMIT),
            name="diff_attn",
        )(lambda_q1[l].reshape(1, -1), lambda_k1[l].reshape(1, -1),
          lambda_q2[l].reshape(1, -1), lambda_k2[l].reshape(1, -1),
          diff_subln_g[l].reshape(-1, 1), dqt, dk.reshape(B, nk, tk, DIFF_HEADS * LANES), dvt)

        n_a = MLA_HEADS * MLA_V
        x = pl.pallas_call(
            _ffn_kernel,
            grid=(B, S // tm_ffn),
            in_specs=[
                pl.BlockSpec((1, tm_ffn, D_MODEL), lambda b, s: (b, s, 0)),
                pl.BlockSpec((1, n_a, tm_ffn), lambda b, s: (b, 0, s)),
                pl.BlockSpec((1, D_MODEL - n_a, tm_ffn), lambda b, s: (b, 0, s)),
                _const_spec((n_a, D_MODEL)),
                _const_spec((D_MODEL - n_a, D_MODEL)),
                _const_spec((1, D_MODEL)),
                _const_spec((D_MODEL, D_FF)),
                _const_spec((D_MODEL, D_FF)),
                _const_spec((CONV_WIDTH, D_FF)),
                _const_spec((1, D_FF)),
                _const_spec((D_FF, D_MODEL)),
            ],
            out_specs=pl.BlockSpec((1, tm_ffn, D_MODEL), lambda b, s: (b, s, 0)),
            out_shape=jax.ShapeDtypeStruct((B, S, D_MODEL), x.dtype),
            scratch_shapes=[pltpu.VMEM((8, D_FF), jnp.float32),
                            pltpu.VMEM((tm_ffn, D_FF), bf)],
            compiler_params=pltpu.CompilerParams(
                dimension_semantics=("arbitrary", "arbitrary"), vmem_limit_bytes=VMEM_LIMIT),
            name="ffn",
        )(x, o_a, o_b, w_out[l][:n_a].astype(bf), w_out[l][n_a:].astype(bf),
          ffn_norm_g[l].reshape(1, -1), w_gate[l].astype(bf), w_up[l].astype(bf),
          conv_w[l], conv_b[l].reshape(1, -1), w_down[l].astype(bf))
    return x
```

```python
import functools
import math
from typing import NamedTuple

import jax
import jax.numpy as jnp
from jax import lax
from jax.experimental import pallas as pl
from jax.experimental.pallas import tpu as pltpu

D_MODEL = 1024
MLA_HEADS = 8
MLA_NOPE = 64
MLA_ROPE = 32
MLA_V = 64
MLA_QK = MLA_NOPE + MLA_ROPE
Q_RANK = 384
KV_RANK = 256
DIFF_HEADS = 4
DIFF_D = 64
DIFF_V = 2 * DIFF_D
D_FF = 2816
CONV_WIDTH = 3
ROPE_THETA = 10000.0
EPS = 1e-6
LANES = 128
SUM_ROWS = 16
LOG2E = math.log2(math.e)

R_Q = 0
R_KV = R_Q + Q_RANK
R_KPE = R_KV + KV_RANK
R_DQ = R_KPE + MLA_ROPE
R_DK = R_DQ + DIFF_HEADS * 2 * DIFF_D
R_DV = R_DK + DIFF_HEADS * 2 * DIFF_D
IN_COLS = R_DV + DIFF_HEADS * DIFF_V

VMEM_LIMIT = 56 * 1024 * 1024


def _rms_rows(x, g):
    ms = jnp.mean(x * x, axis=-1, keepdims=True)
    return x * lax.rsqrt(ms + EPS) * g


def _rms_cols(xt, n, scale):
    ms = jnp.sum(xt * xt, axis=0, keepdims=True) * (1.0 / n)
    return scale * lax.rsqrt(scale * scale * ms + EPS)


def _rope_cols(xt, cos, sin):
    half = xt.shape[0] // 2
    x1, x2 = xt[:half], xt[half:]
    return x1 * cos - x2 * sin, x2 * cos + x1 * sin


def _proj_kernel(x_ref, g_attn_ref, w_in_ref, g_qa_ref, w_q_ref, g_kva_ref, w_k_ref, w_v_ref,
                 g_q_ref, g_k_ref, g_dq_ref, g_dk_ref,
                 cos_a_ref, sin_a_ref, cos_b_ref, sin_b_ref,
                 qt_ref, k_ref, vt_ref, dqt_ref, dk_ref, dvt_ref):
    tm = x_ref.shape[1]
    bf = jnp.bfloat16
    x = x_ref[0]
    r_tok = lax.rsqrt(jnp.mean(x * x, axis=-1, keepdims=True) + EPS)
    r_tok = jnp.transpose(jnp.broadcast_to(r_tok, (tm, LANES)))[0:1, :]
    h = (x * g_attn_ref[...]).astype(bf)

    def in_proj(r0, r1):
        return lax.dot_general(w_in_ref[r0:r1, :], h, (((1,), (1,)), ((), ())),
                               preferred_element_type=jnp.float32)

    cos_a, sin_a = cos_a_ref[...], sin_a_ref[...]
    cos_b, sin_b = cos_b_ref[...], sin_b_ref[...]
    zeros_pad = jnp.zeros((LANES - MLA_QK, tm), jnp.float32)
    one = jnp.ones((1, tm), jnp.float32)

    def head_a(nope, pe_roped, g):
        nope = nope * _rms_cols(nope, MLA_NOPE, one) * g[:MLA_NOPE]
        return jnp.concatenate([nope, *pe_roped, zeros_pad], axis=0)

    def rope_a(pe, g, scale):
        pe = pe * _rms_cols(pe, MLA_ROPE, scale) * g[MLA_NOPE:]
        return _rope_cols(pe, cos_a, sin_a)

    def head_b(xt, g):
        xt = xt * _rms_cols(xt, DIFF_D, r_tok) * g
        return jnp.concatenate(_rope_cols(xt, cos_b, sin_b), axis=0)

    lat = in_proj(R_Q, R_DQ)
    dq = in_proj(R_DQ, R_DK)
    dk = in_proj(R_DK, R_DV)

    cq = lat[R_Q:R_Q + Q_RANK]
    cq = (cq * _rms_cols(cq, Q_RANK, r_tok) * g_qa_ref[...]).astype(bf)
    q = jnp.dot(w_q_ref[...], cq, preferred_element_type=jnp.float32)
    ckv = lat[R_KV:R_KV + KV_RANK]
    ckv = (ckv * _rms_cols(ckv, KV_RANK, r_tok) * g_kva_ref[...]).astype(bf)
    kn = jnp.dot(w_k_ref[...], ckv, preferred_element_type=jnp.float32)
    vt_ref[0, 0] = jnp.dot(w_v_ref[...], ckv, preferred_element_type=jnp.float32).astype(bf)
    dvt_ref[0, 0] = (in_proj(R_DV, IN_COLS) * r_tok).astype(bf)

    g_q = g_q_ref[...] * (MLA_QK ** -0.5 * LOG2E)
    for hd in range(MLA_HEADS):
        r0 = hd * MLA_QK
        qt_ref[0, hd * LANES:(hd + 1) * LANES, :] = head_a(
            q[r0:r0 + MLA_NOPE], rope_a(q[r0 + MLA_NOPE:r0 + MLA_QK], g_q, one), g_q).astype(bf)

    g_k = g_k_ref[...]
    kpe = rope_a(lat[R_KPE:R_KPE + MLA_ROPE], g_k, r_tok)
    for hd in range(MLA_HEADS):
        kt = head_a(kn[hd * MLA_NOPE:(hd + 1) * MLA_NOPE], kpe, g_k)
        k_ref[0, :, hd * LANES:(hd + 1) * LANES] = kt.T.astype(bf)

    g_dq = g_dq_ref[...] * (DIFF_D ** -0.5 * LOG2E)
    g_dk = g_dk_ref[...]
    zeros_half = jnp.zeros((DIFF_D, tm), bf)
    for hd in range(DIFF_HEADS):
        r0 = hd * 2 * DIFF_D
        q1 = head_b(dq[r0:r0 + DIFF_D], g_dq).astype(bf)
        q2 = head_b(dq[r0 + DIFF_D:r0 + 2 * DIFF_D], g_dq).astype(bf)
        b0 = 2 * hd * LANES
        dqt_ref[0, b0:b0 + LANES, :] = jnp.concatenate([q1, zeros_half], axis=0)
        dqt_ref[0, b0 + LANES:b0 + 2 * LANES, :] = jnp.concatenate([zeros_half, q2], axis=0)
        dkt = jnp.concatenate([head_b(dk[r0:r0 + DIFF_D], g_dk),
                               head_b(dk[r0 + DIFF_D:r0 + 2 * DIFF_D], g_dk)], axis=0)
        dk_ref[0, :, hd * LANES:(hd + 1) * LANES] = dkt.T.astype(bf)


def _causal_mask(s_t, tk, tq):
    key = lax.broadcasted_iota(jnp.int32, (tk, tq), 0)
    qry = lax.broadcasted_iota(jnp.int32, (tk, tq), 1)
    return jnp.where(key <= qry, s_t, -jnp.inf)


class _Chain(NamedTuple):
    k_lanes: slice
    q_rows: slice
    v_rows: slice
    acc_rows: slice


def _pipelined_sweep(chains, qt_ref, k_ref, vt_ref, qi, tq, tk, p_ref, al_ref, m_ref, acc_ref):
    assert tq == 2 * tk
    lower, upper = slice(0, tk), slice(tk, tq)

    def stage_a(j, cols):
        return [jnp.dot(k_ref[0, j, :, ch.k_lanes], qt_ref[0, ch.q_rows, cols],
                        preferred_element_type=jnp.float32) for ch in chains]

    def stage_b(s_all, slot, cols, first=False):
        for c, s_t in enumerate(s_all):
            if first:
                m_new = jnp.max(s_t, axis=0, keepdims=True)
            else:
                m_old = m_ref[c, :, cols]
                m_new = jnp.maximum(m_old, jnp.max(s_t, axis=0, keepdims=True))
                al_ref[slot, c, :, cols] = jnp.exp2(m_old - m_new)
            m_ref[c, :, cols] = m_new
            p_ref[slot, c, :, cols] = jnp.exp2(s_t - m_new).astype(p_ref.dtype)

    def stage_c(j, slot, first=False):
        ones = jnp.ones((SUM_ROWS, tk), vt_ref.dtype)
        for c, ch in enumerate(chains):
            vt_ones = jnp.concatenate([vt_ref[0, j, ch.v_rows, :], ones], axis=0)
            pv = jnp.dot(vt_ones, p_ref[slot, c], preferred_element_type=jnp.float32)
            if first:
                acc_ref[ch.acc_rows, :] = pv
            else:
                acc_ref[ch.acc_rows, :] = al_ref[slot, c] * acc_ref[ch.acc_rows, :] + pv

    d0, d1 = 2 * qi, 2 * qi + 1
    s_all = stage_a(d0, slice(0, tq))
    stage_b([jnp.concatenate([_causal_mask(s_t[:, lower], tk, tk), s_t[:, upper]], axis=1)
             for s_t in s_all], 0, slice(0, tq), first=True)

    s_all = stage_a(d1, upper)
    stage_c(d0, 0, first=True)
    stage_b([_causal_mask(s_t, tk, tk) for s_t in s_all], 1, upper)
    p_ref[1, :, :, lower] = jnp.zeros((len(chains), tk, tk), p_ref.dtype)
    al_ref[1, :, :, lower] = jnp.ones((len(chains), 1, tk), al_ref.dtype)

    def two_steps(k, carry):
        s_all = stage_a(2 * k, slice(0, tq))
        stage_c(jnp.where(k == 0, d1, 2 * k - 1), 1)
        stage_b(s_all, 0, slice(0, tq))
        s_all = stage_a(2 * k + 1, slice(0, tq))
        stage_c(2 * k, 0)
        stage_b(s_all, 1, slice(0, tq))
        return carry

    lax.fori_loop(0, qi, two_steps, 0)
    stage_c(jnp.where(qi == 0, d1, 2 * qi - 1), 1)


def _mla_kernel(qt_ref, k_ref, vt_ref, o_ref, p_ref, al_ref, m_ref, acc_ref, *, tq, tk):
    acc_rows = MLA_V + SUM_ROWS
    chains = [_Chain(slice(hd * LANES, (hd + 1) * LANES), slice(hd * LANES, (hd + 1) * LANES),
                     slice(hd * MLA_V, (hd + 1) * MLA_V), slice(hd * acc_rows, (hd + 1) * acc_rows))
              for hd in range(MLA_HEADS)]
    _pipelined_sweep(chains, qt_ref, k_ref, vt_ref, pl.program_id(1), tq, tk,
                     p_ref, al_ref, m_ref, acc_ref)
    for hd in range(MLA_HEADS):
        a0 = hd * acc_rows
        o_ref[0, hd * MLA_V:(hd + 1) * MLA_V, :] = (
            acc_ref[a0:a0 + MLA_V, :] / acc_ref[a0 + MLA_V:a0 + MLA_V + 1, :]).astype(o_ref.dtype)


def _diff_kernel(lq1_ref, lk1_ref, lq2_ref, lk2_ref, g_sub_ref, dqt_ref, dk_ref, dvt_ref, o_ref,
                 p_ref, al_ref, m_ref, acc_ref, *, tq, tk, lam_init):
    acc_rows = DIFF_V + SUM_ROWS
    lam = (jnp.exp(jnp.sum(lq1_ref[...] * lk1_ref[...], axis=-1, keepdims=True))
           - jnp.exp(jnp.sum(lq2_ref[...] * lk2_ref[...], axis=-1, keepdims=True))
           + lam_init)
    chains = [_Chain(slice((c // 2) * LANES, (c // 2 + 1) * LANES), slice(c * LANES, (c + 1) * LANES),
                     slice((c // 2) * DIFF_V, (c // 2 + 1) * DIFF_V),
                     slice(c * acc_rows, (c + 1) * acc_rows))
              for c in range(2 * DIFF_HEADS)]
    _pipelined_sweep(chains, dqt_ref, dk_ref, dvt_ref, pl.program_id(1), tq, tk,
                     p_ref, al_ref, m_ref, acc_ref)

    def normalised(c):
        a0 = c * acc_rows
        return acc_ref[a0:a0 + DIFF_V, :] / acc_ref[a0 + DIFF_V:a0 + DIFF_V + 1, :]

    for hd in range(DIFF_HEADS):
        o = normalised(2 * hd) - lam * normalised(2 * hd + 1)
        ms = jnp.mean(o * o, axis=0, keepdims=True)
        o = o * lax.rsqrt(ms + EPS) * g_sub_ref[...] * (1.0 - lam_init)
        o_ref[0, hd * DIFF_V:(hd + 1) * DIFF_V, :] = o.astype(o_ref.dtype)


FF_CHUNK = 512


def _ffn_kernel(x_ref, oa_ref, ob_ref, w_oa_ref, w_ob_ref, g_ffn_ref, w_gate_ref, w_up_ref,
                conv_w_ref, conv_b_ref, w_down_ref, out_ref, prev_ref, y_ref):
    si = pl.program_id(1)
    tm = x_ref.shape[1]
    contract0 = (((0,), (0,)), ((), ()))
    mix = (lax.dot_general(oa_ref[0], w_oa_ref[...], contract0, preferred_element_type=jnp.float32)
           + lax.dot_general(ob_ref[0], w_ob_ref[...], contract0,
                             preferred_element_type=jnp.float32))
    x1 = x_ref[0] + mix
    h = _rms_rows(x1, g_ffn_ref[...]).astype(jnp.bfloat16)

    @pl.when(si == 0)
    def _():
        prev_ref[...] = jnp.zeros_like(prev_ref)

    for c0 in range(0, D_FF, FF_CHUNK):
        cw = min(FF_CHUNK, D_FF - c0)
        g = jnp.dot(h, w_gate_ref[:, c0:c0 + cw], preferred_element_type=jnp.float32)
        u = jnp.dot(h, w_up_ref[:, c0:c0 + cw], preferred_element_type=jnp.float32)
        row = lax.broadcasted_iota(jnp.int32, (tm, cw), 0)
        p1 = prev_ref[7:8, c0:c0 + cw]
        p2 = prev_ref[6:7, c0:c0 + cw]
        g1 = jnp.where(row == 0, p1, pltpu.roll(g, 1, axis=0))
        g2 = jnp.where(row == 0, p2, jnp.where(row == 1, p1, pltpu.roll(g, 2, axis=0)))
        prev_ref[:, c0:c0 + cw] = g[tm - 8:tm, :]
        cg = (conv_b_ref[:, c0:c0 + cw] + g2 * conv_w_ref[0:1, c0:c0 + cw]
              + g1 * conv_w_ref[1:2, c0:c0 + cw] + g * conv_w_ref[2:3, c0:c0 + cw])
        y_ref[:, c0:c0 + cw] = (jax.nn.silu(cg) * u).astype(y_ref.dtype)

    out_ref[0] = x1 + jnp.dot(y_ref[...], w_down_ref[...], preferred_element_type=jnp.float32)


def _rope_tables(seq):
    pos = jnp.arange(seq, dtype=jnp.float32)[:, None]

    def tables(dim):
        inv = 1.0 / (ROPE_THETA ** (jnp.arange(0, dim, 2, dtype=jnp.float32) / dim))
        ang = pos * inv[None, :]
        return jnp.cos(ang), jnp.sin(ang)

    ca, sa = tables(MLA_ROPE)
    cb, sb = tables(DIFF_D)
    return ca.T, sa.T, cb.T, sb.T


def _gain_cols(g, tokens):
    return jnp.broadcast_to(g[:, None], (g.shape[0], tokens))


def _attn_scratch(n_chains, dv, tq, tk):
    return [pltpu.VMEM((2, n_chains, tk, tq), jnp.bfloat16),
            pltpu.VMEM((2, n_chains, 1, tq), jnp.float32),
            pltpu.VMEM((n_chains, 1, tq), jnp.float32),
            pltpu.VMEM((n_chains * (dv + SUM_ROWS), tq), jnp.float32)]


def _const_spec(shape):
    return pl.BlockSpec(shape, lambda *_: (0,) * len(shape))


def kernel(x, attn_norm_g, w_in, q_a_norm_g, w_q_up, kv_a_norm_g, w_kv_up, mla_q_norm_g,
           mla_k_norm_g, diff_q_norm_g, diff_k_norm_g, lambda_q1, lambda_k1, lambda_q2, lambda_k2,
           diff_subln_g, w_out, ffn_norm_g, w_gate, w_up, conv_w, conv_b, w_down):
    B, S, _ = x.shape
    depth = w_in.shape[0]
    bf = jnp.bfloat16
    cos_a, sin_a, cos_b, sin_b = _rope_tables(S)

    tm_proj = 256
    tk = tm_proj
    tq = 2 * tk
    tm_ffn = 1024
    assert S % tq == 0 and S % tm_ffn == 0
    nk = S // tk

    for l in range(depth):
        w_in_t = w_in[l].T.astype(bf)
        w_q_t = w_q_up[l].T.astype(bf)
        wkv_t = w_kv_up[l].T.reshape(MLA_HEADS, MLA_NOPE + MLA_V, KV_RANK)
        w_k_t = wkv_t[:, :MLA_NOPE].reshape(MLA_HEADS * MLA_NOPE, KV_RANK).astype(bf)
        w_v_t = wkv_t[:, MLA_NOPE:].reshape(MLA_HEADS * MLA_V, KV_RANK).astype(bf)

        n_tok_tiles = S // tm_proj
        tok3 = lambda b, s: (b, s, 0)
        feat3 = lambda b, s: (b, 0, s)
        blk4 = lambda b, s: (b, s, 0, 0)
        rope_a_spec = pl.BlockSpec((MLA_ROPE // 2, tm_proj), lambda b, s: (0, s))
        rope_b_spec = pl.BlockSpec((DIFF_D // 2, tm_proj), lambda b, s: (0, s))
        qt, k_a, vt, dqt, dk, dvt = pl.pallas_call(
            _proj_kernel,
            grid=(B, n_tok_tiles),
            in_specs=[
                pl.BlockSpec((1, tm_proj, D_MODEL), tok3),
                _const_spec((1, D_MODEL)),
                _const_spec((IN_COLS, D_MODEL)),
                _const_spec((Q_RANK, tm_proj)),
                _const_spec((MLA_HEADS * MLA_QK, Q_RANK)),
                _const_spec((KV_RANK, tm_proj)),
                _const_spec((MLA_HEADS * MLA_NOPE, KV_RANK)),
                _const_spec((MLA_HEADS * MLA_V, KV_RANK)),
                _const_spec((MLA_QK, tm_proj)), _const_spec((MLA_QK, tm_proj)),
                _const_spec((DIFF_D, tm_proj)), _const_spec((DIFF_D, tm_proj)),
                rope_a_spec, rope_a_spec, rope_b_spec, rope_b_spec,
            ],
            out_specs=[
                pl.BlockSpec((1, MLA_HEADS * LANES, tm_proj), feat3),
                pl.BlockSpec((1, tm_proj, MLA_HEADS * LANES), tok3),
                pl.BlockSpec((1, 1, MLA_HEADS * MLA_V, tm_proj), blk4),
                pl.BlockSpec((1, DIFF_HEADS * 2 * LANES, tm_proj), feat3),
                pl.BlockSpec((1, tm_proj, DIFF_HEADS * LANES), tok3),
                pl.BlockSpec((1, 1, DIFF_HEADS * DIFF_V, tm_proj), blk4),
            ],
            out_shape=[
                jax.ShapeDtypeStruct((B, MLA_HEADS * LANES, S), bf),
                jax.ShapeDtypeStruct((B, S, MLA_HEADS * LANES), bf),
                jax.ShapeDtypeStruct((B, n_tok_tiles, MLA_HEADS * MLA_V, tm_proj), bf),
                jax.ShapeDtypeStruct((B, DIFF_HEADS * 2 * LANES, S), bf),
                jax.ShapeDtypeStruct((B, S, DIFF_HEADS * LANES), bf),
                jax.ShapeDtypeStruct((B, n_tok_tiles, DIFF_HEADS * DIFF_V, tm_proj), bf),
            ],
            compiler_params=pltpu.CompilerParams(
                dimension_semantics=("arbitrary", "arbitrary"), vmem_limit_bytes=VMEM_LIMIT),
            name="proj",
        )(x, attn_norm_g[l].reshape(1, -1), w_in_t, _gain_cols(q_a_norm_g[l], tm_proj), w_q_t,
          _gain_cols(kv_a_norm_g[l], tm_proj), w_k_t, w_v_t,
          _gain_cols(mla_q_norm_g[l], tm_proj), _gain_cols(mla_k_norm_g[l], tm_proj),
          _gain_cols(diff_q_norm_g[l], tm_proj), _gain_cols(diff_k_norm_g[l], tm_proj),
          cos_a, sin_a, cos_b, sin_b)

        nq = S // tq
        o_a = pl.pallas_call(
            functools.partial(_mla_kernel, tq=tq, tk=tk),
            grid=(B, nq),
            in_specs=[
                pl.BlockSpec((1, MLA_HEADS * LANES, tq), lambda b, i: (b, 0, i)),
                pl.BlockSpec((1, nk, tk, MLA_HEADS * LANES), lambda b, i: (b, 0, 0, 0)),
                pl.BlockSpec((1, nk, MLA_HEADS * MLA_V, tk), lambda b, i: (b, 0, 0, 0)),
            ],
            out_specs=pl.BlockSpec((1, MLA_HEADS * MLA_V, tq), lambda b, i: (b, 0, i)),
            out_shape=jax.ShapeDtypeStruct((B, MLA_HEADS * MLA_V, S), bf),
            scratch_shapes=_attn_scratch(MLA_HEADS, MLA_V, tq, tk),
            compiler_params=pltpu.CompilerParams(
                dimension_semantics=("arbitrary", "arbitrary"), vmem_limit_bytes=VMEM_LIMIT),
            name="mla_attn",
        )(qt, k_a.reshape(B, nk, tk, MLA_HEADS * LANES), vt)

        lam_init = 0.8 - 0.6 * math.exp(-0.3 * l)
        lam_spec = _const_spec((1, DIFF_D))
        o_b = pl.pallas_call(
            functools.partial(_diff_kernel, tq=tq, tk=tk, lam_init=lam_init),
            grid=(B, nq),
            in_specs=[
                lam_spec, lam_spec, lam_spec, lam_spec,
                _const_spec((DIFF_V, 1)),
                pl.BlockSpec((1, DIFF_HEADS * 2 * LANES, tq), lambda b, i: (b, 0, i)),
                pl.BlockSpec((1, nk, tk, DIFF_HEADS * LANES), lambda b, i: (b, 0, 0, 0)),
                pl.BlockSpec((1, nk, DIFF_HEADS * DIFF_V, tk), lambda b, i: (b, 0, 0, 0)),
            ],
            out_specs=pl.BlockSpec((1, DIFF_HEADS * DIFF_V, tq), lambda b, i: (b, 0, i)),
            out_shape=jax.ShapeDtypeStruct((B, DIFF_HEADS * DIFF_V, S), bf),
            scratch_shapes=_attn_scratch(2 * DIFF_HEADS, DIFF_V, tq, tk),
            compiler_params=pltpu.CompilerParams(
                dimension_semantics=("arbitrary", "arbitrary"), vmem_limit_bytes=VMEM_LIMIT),
            name="diff_attn",
        )(lambda_q1[l].reshape(1, -1), lambda_k1[l].reshape(1, -1),
          lambda_q2[l].reshape(1, -1), lambda_k2[l].reshape(1, -1),
          diff_subln_g[l].reshape(-1, 1), dqt, dk.reshape(B, nk, tk, DIFF_HEADS * LANES), dvt)

        n_a = MLA_HEADS * MLA_V
        x = pl.pallas_call(
            _ffn_kernel,
            grid=(B, S // tm_ffn),
            in_specs=[
                pl.BlockSpec((1, tm_ffn, D_MODEL), lambda b, s: (b, s, 0)),
                pl.BlockSpec((1, n_a, tm_ffn), lambda b, s: (b, 0, s)),
                pl.BlockSpec((1, D_MODEL - n_a, tm_ffn), lambda b, s: (b, 0, s)),
                _const_spec((n_a, D_MODEL)),
                _const_spec((D_MODEL - n_a, D_MODEL)),
                _const_spec((1, D_MODEL)),
                _const_spec((D_MODEL, D_FF)),
                _const_spec((D_MODEL, D_FF)),
                _const_spec((CONV_WIDTH, D_FF)),
                _const_spec((1, D_FF)),
                _const_spec((D_FF, D_MODEL)),
            ],
            out_specs=pl.BlockSpec((1, tm_ffn, D_MODEL), lambda b, s: (b, s, 0)),
            out_shape=jax.ShapeDtypeStruct((B, S, D_MODEL), x.dtype),
            scratch_shapes=[pltpu.VMEM((8, D_FF), jnp.float32),
                            pltpu.VMEM((tm_ffn, D_FF), bf)],
            compiler_params=pltpu.CompilerParams(
                dimension_semantics=("arbitrary", "arbitrary"), vmem_limit_bytes=VMEM_LIMIT),
            name="ffn",
        )(x, o_a, o_b, w_out[l][:n_a].astype(bf), w_out[l][n_a:].astype(bf),
          ffn_norm_g[l].reshape(1, -1), w_gate[l].astype(bf), w_up[l].astype(bf),
          conv_w[l], conv_b[l].reshape(1, -1), w_down[l].astype(bf))
    return x
```

```python
import functools
import math
from typing import NamedTuple

import jax
import jax.numpy as jnp
from jax import lax
from jax.experimental import pallas as pl
from jax.experimental.pallas import tpu as pltpu

D_MODEL = 1024
MLA_HEADS = 8
MLA_NOPE = 64
MLA_ROPE = 32
MLA_V = 64
MLA_QK = MLA_NOPE + MLA_ROPE
Q_RANK = 384
KV_RANK = 256
DIFF_HEADS = 4
DIFF_D = 64
DIFF_V = 2 * DIFF_D
D_FF = 2816
CONV_WIDTH = 3
ROPE_THETA = 10000.0
EPS = 1e-6
LANES = 128
SUM_ROWS = 16
LOG2E = math.log2(math.e)

R_Q = 0
R_KV = R_Q + Q_RANK
R_KPE = R_KV + KV_RANK
R_DQ = R_KPE + MLA_ROPE
R_DK = R_DQ + DIFF_HEADS * 2 * DIFF_D
R_DV = R_DK + DIFF_HEADS * 2 * DIFF_D
IN_COLS = R_DV + DIFF_HEADS * DIFF_V

VMEM_LIMIT = 56 * 1024 * 1024


def _rms_rows(x, g):
    ms = jnp.mean(x * x, axis=-1, keepdims=True)
    return x * lax.rsqrt(ms + EPS) * g


def _rms_cols(xt, n, scale):
    ms = jnp.sum(xt * xt, axis=0, keepdims=True) * (1.0 / n)
    return scale * lax.rsqrt(scale * scale * ms + EPS)


def _rope_cols(xt, cos, sin):
    half = xt.shape[0] // 2
    x1, x2 = xt[:half], xt[half:]
    return x1 * cos - x2 * sin, x2 * cos + x1 * sin


def _proj_kernel(x_ref, g_attn_ref, w_in_ref, g_qa_ref, w_q_ref, g_kva_ref, w_k_ref, w_v_ref,
                 g_q_ref, g_k_ref, g_dq_ref, g_dk_ref,
                 cos_a_ref, sin_a_ref, cos_b_ref, sin_b_ref,
                 qt_ref, k_ref, vt_ref, dqt_ref, dk_ref, dvt_ref):
    tm = x_ref.shape[1]
    bf = jnp.bfloat16
    x = x_ref[0]
    r_tok = lax.rsqrt(jnp.mean(x * x, axis=-1, keepdims=True) + EPS)
    r_tok = jnp.transpose(jnp.broadcast_to(r_tok, (tm, LANES)))[0:1, :]
    h = (x * g_attn_ref[...]).astype(bf)

    def in_proj(r0, r1):
        return lax.dot_general(w_in_ref[r0:r1, :], h, (((1,), (1,)), ((), ())),
                               preferred_element_type=jnp.float32)

    cos_a, sin_a = cos_a_ref[...], sin_a_ref[...]
    cos_b, sin_b = cos_b_ref[...], sin_b_ref[...]
    zeros_pad = jnp.zeros((LANES - MLA_QK, tm), jnp.float32)
    one = jnp.ones((1, tm), jnp.float32)

    def head_a(nope, pe_roped, g):
        nope = nope * _rms_cols(nope, MLA_NOPE, one) * g[:MLA_NOPE]
        return jnp.concatenate([nope, *pe_roped, zeros_pad], axis=0)

    def rope_a(pe, g, scale):
        pe = pe * _rms_cols(pe, MLA_ROPE, scale) * g[MLA_NOPE:]
        return _rope_cols(pe, cos_a, sin_a)

    def head_b(xt, g):
        xt = xt * _rms_cols(xt, DIFF_D, r_tok) * g
        return jnp.concatenate(_rope_cols(xt, cos_b, sin_b), axis=0)

    lat = in_proj(R_Q, R_DQ)
    dq = in_proj(R_DQ, R_DK)
    dk = in_proj(R_DK, R_DV)

    cq = lat[R_Q:R_Q + Q_RANK]
    cq = (cq * _rms_cols(cq, Q_RANK, r_tok) * g_qa_ref[...]).astype(bf)
    q = jnp.dot(w_q_ref[...], cq, preferred_element_type=jnp.float32)
    ckv = lat[R_KV:R_KV + KV_RANK]
    ckv = (ckv * _rms_cols(ckv, KV_RANK, r_tok) * g_kva_ref[...]).astype(bf)
    kn = jnp.dot(w_k_ref[...], ckv, preferred_element_type=jnp.float32)
    v = jnp.dot(w_v_ref[...], ckv, preferred_element_type=jnp.float32).astype(bf)
    dv = (in_proj(R_DV, IN_COLS) * r_tok).astype(bf)
    tk = vt_ref.shape[3]
    for t in range(tm // tk):
        vt_ref[0, t] = v[:, t * tk:(t + 1) * tk]
        dvt_ref[0, t] = dv[:, t * tk:(t + 1) * tk]

    g_q = g_q_ref[...] * (MLA_QK ** -0.5 * LOG2E)
    for hd in range(MLA_HEADS):
        r0 = hd * MLA_QK
        qt_ref[0, hd * LANES:(hd + 1) * LANES, :] = head_a(
            q[r0:r0 + MLA_NOPE], rope_a(q[r0 + MLA_NOPE:r0 + MLA_QK], g_q, one), g_q).astype(bf)

    g_k = g_k_ref[...]
    kpe = rope_a(lat[R_KPE:R_KPE + MLA_ROPE], g_k, r_tok)
    for hd in range(MLA_HEADS):
        kt = head_a(kn[hd * MLA_NOPE:(hd + 1) * MLA_NOPE], kpe, g_k)
        k_ref[0, :, hd * LANES:(hd + 1) * LANES] = kt.T.astype(bf)

    g_dq = g_dq_ref[...] * (DIFF_D ** -0.5 * LOG2E)
    g_dk = g_dk_ref[...]
    zeros_half = jnp.zeros((DIFF_D, tm), bf)
    for hd in range(DIFF_HEADS):
        r0 = hd * 2 * DIFF_D
        q1 = head_b(dq[r0:r0 + DIFF_D], g_dq).astype(bf)
        q2 = head_b(dq[r0 + DIFF_D:r0 + 2 * DIFF_D], g_dq).astype(bf)
        b0 = 2 * hd * LANES
        dqt_ref[0, b0:b0 + LANES, :] = jnp.concatenate([q1, zeros_half], axis=0)
        dqt_ref[0, b0 + LANES:b0 + 2 * LANES, :] = jnp.concatenate([zeros_half, q2], axis=0)
        dkt = jnp.concatenate([head_b(dk[r0:r0 + DIFF_D], g_dk),
                               head_b(dk[r0 + DIFF_D:r0 + 2 * DIFF_D], g_dk)], axis=0)
        dk_ref[0, :, hd * LANES:(hd + 1) * LANES] = dkt.T.astype(bf)


def _causal_mask(s_t, tk, tq):
    key = lax.broadcasted_iota(jnp.int32, (tk, tq), 0)
    qry = lax.broadcasted_iota(jnp.int32, (tk, tq), 1)
    return jnp.where(key <= qry, s_t, -jnp.inf)


class _Chain(NamedTuple):
    k_lanes: slice
    q_rows: slice
    v_rows: slice
    acc_rows: slice


def _pipelined_sweep(chains, qt_ref, k_ref, vt_ref, qi, tq, tk, p_ref, al_ref, m_ref, acc_ref):
    assert tq == 2 * tk
    lower, upper = slice(0, tk), slice(tk, tq)

    def stage_a(j, cols):
        return [jnp.dot(k_ref[0, j, :, ch.k_lanes], qt_ref[0, ch.q_rows, cols],
                        preferred_element_type=jnp.float32) for ch in chains]

    def stage_b(s_all, slot, cols, first=False):
        for c, s_t in enumerate(s_all):
            if first:
                m_new = jnp.max(s_t, axis=0, keepdims=True)
            else:
                m_old = m_ref[c, :, cols]
                m_new = jnp.maximum(m_old, jnp.max(s_t, axis=0, keepdims=True))
                al_ref[slot, c, :, cols] = jnp.exp2(m_old - m_new)
            m_ref[c, :, cols] = m_new
            p_ref[slot, c, :, cols] = jnp.exp2(s_t - m_new).astype(p_ref.dtype)

    def stage_c(j, slot, first=False):
        ones = jnp.ones((SUM_ROWS, tk), vt_ref.dtype)
        for c, ch in enumerate(chains):
            vt_ones = jnp.concatenate([vt_ref[0, j, ch.v_rows, :], ones], axis=0)
            pv = jnp.dot(vt_ones, p_ref[slot, c], preferred_element_type=jnp.float32)
            if first:
                acc_ref[ch.acc_rows, :] = pv
            else:
                acc_ref[ch.acc_rows, :] = al_ref[slot, c] * acc_ref[ch.acc_rows, :] + pv

    d0, d1 = 2 * qi, 2 * qi + 1
    s_all = stage_a(d0, slice(0, tq))
    stage_b([jnp.concatenate([_causal_mask(s_t[:, lower], tk, tk), s_t[:, upper]], axis=1)
             for s_t in s_all], 0, slice(0, tq), first=True)

    s_all = stage_a(d1, upper)
    stage_c(d0, 0, first=True)
    stage_b([_causal_mask(s_t, tk, tk) for s_t in s_all], 1, upper)
    p_ref[1, :, :, lower] = jnp.zeros((len(chains), tk, tk), p_ref.dtype)
    al_ref[1, :, :, lower] = jnp.ones((len(chains), 1, tk), al_ref.dtype)

    def two_steps(k, carry):
        s_all = stage_a(2 * k, slice(0, tq))
        stage_c(jnp.where(k == 0, d1, 2 * k - 1), 1)
        stage_b(s_all, 0, slice(0, tq))
        s_all = stage_a(2 * k + 1, slice(0, tq))
        stage_c(2 * k, 0)
        stage_b(s_all, 1, slice(0, tq))
        return carry

    lax.fori_loop(0, qi, two_steps, 0)
    stage_c(jnp.where(qi == 0, d1, 2 * qi - 1), 1)


def _mla_kernel(qt_ref, k_ref, vt_ref, o_ref, p_ref, al_ref, m_ref, acc_ref, *, tq, tk):
    acc_rows = MLA_V + SUM_ROWS
    chains = [_Chain(slice(hd * LANES, (hd + 1) * LANES), slice(hd * LANES, (hd + 1) * LANES),
                     slice(hd * MLA_V, (hd + 1) * MLA_V), slice(hd * acc_rows, (hd + 1) * acc_rows))
              for hd in range(MLA_HEADS)]
    _pipelined_sweep(chains, qt_ref, k_ref, vt_ref, pl.program_id(1), tq, tk,
                     p_ref, al_ref, m_ref, acc_ref)
    for hd in range(MLA_HEADS):
        a0 = hd * acc_rows
        o_ref[0, hd * MLA_V:(hd + 1) * MLA_V, :] = (
            acc_ref[a0:a0 + MLA_V, :] / acc_ref[a0 + MLA_V:a0 + MLA_V + 1, :]).astype(o_ref.dtype)


def _diff_kernel(lq1_ref, lk1_ref, lq2_ref, lk2_ref, g_sub_ref, dqt_ref, dk_ref, dvt_ref, o_ref,
                 p_ref, al_ref, m_ref, acc_ref, *, tq, tk, lam_init):
    acc_rows = DIFF_V + SUM_ROWS
    lam = (jnp.exp(jnp.sum(lq1_ref[...] * lk1_ref[...], axis=-1, keepdims=True))
           - jnp.exp(jnp.sum(lq2_ref[...] * lk2_ref[...], axis=-1, keepdims=True))
           + lam_init)
    chains = [_Chain(slice((c // 2) * LANES, (c // 2 + 1) * LANES), slice(c * LANES, (c + 1) * LANES),
                     slice((c // 2) * DIFF_V, (c // 2 + 1) * DIFF_V),
                     slice(c * acc_rows, (c + 1) * acc_rows))
              for c in range(2 * DIFF_HEADS)]
    _pipelined_sweep(chains, dqt_ref, dk_ref, dvt_ref, pl.program_id(1), tq, tk,
                     p_ref, al_ref, m_ref, acc_ref)

    def normalised(c):
        a0 = c * acc_rows
        return acc_ref[a0:a0 + DIFF_V, :] / acc_ref[a0 + DIFF_V:a0 + DIFF_V + 1, :]

    for hd in range(DIFF_HEADS):
        o = normalised(2 * hd) - lam * normalised(2 * hd + 1)
        ms = jnp.mean(o * o, axis=0, keepdims=True)
        o = o * lax.rsqrt(ms + EPS) * g_sub_ref[...] * (1.0 - lam_init)
        o_ref[0, hd * DIFF_V:(hd + 1) * DIFF_V, :] = o.astype(o_ref.dtype)


FF_CHUNK = 512


def _ffn_kernel(x_ref, oa_ref, ob_ref, w_oa_ref, w_ob_ref, g_ffn_ref, w_gate_ref, w_up_ref,
                conv_w_ref, conv_b_ref, w_down_ref, out_ref, prev_ref, y_ref):
    si = pl.program_id(1)
    tm = x_ref.shape[1]
    contract0 = (((0,), (0,)), ((), ()))
    mix = (lax.dot_general(oa_ref[0], w_oa_ref[...], contract0, preferred_element_type=jnp.float32)
           + lax.dot_general(ob_ref[0], w_ob_ref[...], contract0,
                             preferred_element_type=jnp.float32))
    x1 = x_ref[0] + mix
    h = _rms_rows(x1, g_ffn_ref[...]).astype(jnp.bfloat16)

    @pl.when(si == 0)
    def _():
        prev_ref[...] = jnp.zeros_like(prev_ref)

    for c0 in range(0, D_FF, FF_CHUNK):
        cw = min(FF_CHUNK, D_FF - c0)
        g = jnp.dot(h, w_gate_ref[:, c0:c0 + cw], preferred_element_type=jnp.float32)
        u = jnp.dot(h, w_up_ref[:, c0:c0 + cw], preferred_element_type=jnp.float32)
        row = lax.broadcasted_iota(jnp.int32, (tm, cw), 0)
        p1 = prev_ref[7:8, c0:c0 + cw]
        p2 = prev_ref[6:7, c0:c0 + cw]
        g1 = jnp.where(row == 0, p1, pltpu.roll(g, 1, axis=0))
        g2 = jnp.where(row == 0, p2, jnp.where(row == 1, p1, pltpu.roll(g, 2, axis=0)))
        prev_ref[:, c0:c0 + cw] = g[tm - 8:tm, :]
        cg = (conv_b_ref[:, c0:c0 + cw] + g2 * conv_w_ref[0:1, c0:c0 + cw]
              + g1 * conv_w_ref[1:2, c0:c0 + cw] + g * conv_w_ref[2:3, c0:c0 + cw])
        y_ref[:, c0:c0 + cw] = (jax.nn.silu(cg) * u).astype(y_ref.dtype)

    out_ref[0] = x1 + jnp.dot(y_ref[...], w_down_ref[...], preferred_element_type=jnp.float32)


def _rope_tables(seq):
    pos = jnp.arange(seq, dtype=jnp.float32)[:, None]

    def tables(dim):
        inv = 1.0 / (ROPE_THETA ** (jnp.arange(0, dim, 2, dtype=jnp.float32) / dim))
        ang = pos * inv[None, :]
        return jnp.cos(ang), jnp.sin(ang)

    ca, sa = tables(MLA_ROPE)
    cb, sb = tables(DIFF_D)
    return ca.T, sa.T, cb.T, sb.T


def _gain_cols(g, tokens):
    return jnp.broadcast_to(g[:, None], (g.shape[0], tokens))


def _attn_scratch(n_chains, dv, tq, tk):
    return [pltpu.VMEM((2, n_chains, tk, tq), jnp.bfloat16),
            pltpu.VMEM((2, n_chains, 1, tq), jnp.float32),
            pltpu.VMEM((n_chains, 1, tq), jnp.float32),
            pltpu.VMEM((n_chains * (dv + SUM_ROWS), tq), jnp.float32)]


def _const_spec(shape):
    return pl.BlockSpec(shape, lambda *_: (0,) * len(shape))


def kernel(x, attn_norm_g, w_in, q_a_norm_g, w_q_up, kv_a_norm_g, w_kv_up, mla_q_norm_g,
           mla_k_norm_g, diff_q_norm_g, diff_k_norm_g, lambda_q1, lambda_k1, lambda_q2, lambda_k2,
           diff_subln_g, w_out, ffn_norm_g, w_gate, w_up, conv_w, conv_b, w_down):
    B, S, _ = x.shape
    depth = w_in.shape[0]
    bf = jnp.bfloat16
    cos_a, sin_a, cos_b, sin_b = _rope_tables(S)

    tk = 256
    tq = 2 * tk
    tm_proj = 1024
    tm_ffn = 1024
    assert tm_proj % tk == 0 and S % tq == 0 and S % tm_proj == 0 and S % tm_ffn == 0
    nk = S // tk

    for l in range(depth):
        w_in_t = w_in[l].T.astype(bf)
        w_q_t = w_q_up[l].T.astype(bf)
        wkv_t = w_kv_up[l].T.reshape(MLA_HEADS, MLA_NOPE + MLA_V, KV_RANK)
        w_k_t = wkv_t[:, :MLA_NOPE].reshape(MLA_HEADS * MLA_NOPE, KV_RANK).astype(bf)
        w_v_t = wkv_t[:, MLA_NOPE:].reshape(MLA_HEADS * MLA_V, KV_RANK).astype(bf)

        n_tok_tiles = S // tm_proj
        tok3 = lambda b, s: (b, s, 0)
        feat3 = lambda b, s: (b, 0, s)
        blk4 = lambda b, s: (b, s, 0, 0)
        rope_a_spec = pl.BlockSpec((MLA_ROPE // 2, tm_proj), lambda b, s: (0, s))
        rope_b_spec = pl.BlockSpec((DIFF_D // 2, tm_proj), lambda b, s: (0, s))
        qt, k_a, vt, dqt, dk, dvt = pl.pallas_call(
            _proj_kernel,
            grid=(B, n_tok_tiles),
            in_specs=[
                pl.BlockSpec((1, tm_proj, D_MODEL), tok3),
                _const_spec((1, D_MODEL)),
                _const_spec((IN_COLS, D_MODEL)),
                _const_spec((Q_RANK, tm_proj)),
                _const_spec((MLA_HEADS * MLA_QK, Q_RANK)),
                _const_spec((KV_RANK, tm_proj)),
                _const_spec((MLA_HEADS * MLA_NOPE, KV_RANK)),
                _const_spec((MLA_HEADS * MLA_V, KV_RANK)),
                _const_spec((MLA_QK, tm_proj)), _const_spec((MLA_QK, tm_proj)),
                _const_spec((DIFF_D, tm_proj)), _const_spec((DIFF_D, tm_proj)),
                rope_a_spec, rope_a_spec, rope_b_spec, rope_b_spec,
            ],
            out_specs=[
                pl.BlockSpec((1, MLA_HEADS * LANES, tm_proj), feat3),
                pl.BlockSpec((1, tm_proj, MLA_HEADS * LANES), tok3),
                pl.BlockSpec((1, tm_proj // tk, MLA_HEADS * MLA_V, tk), blk4),
                pl.BlockSpec((1, DIFF_HEADS * 2 * LANES, tm_proj), feat3),
                pl.BlockSpec((1, tm_proj, DIFF_HEADS * LANES), tok3),
                pl.BlockSpec((1, tm_proj // tk, DIFF_HEADS * DIFF_V, tk), blk4),
            ],
            out_shape=[
                jax.ShapeDtypeStruct((B, MLA_HEADS * LANES, S), bf),
                jax.ShapeDtypeStruct((B, S, MLA_HEADS * LANES), bf),
                jax.ShapeDtypeStruct((B, nk, MLA_HEADS * MLA_V, tk), bf),
                jax.ShapeDtypeStruct((B, DIFF_HEADS * 2 * LANES, S), bf),
                jax.ShapeDtypeStruct((B, S, DIFF_HEADS * LANES), bf),
                jax.ShapeDtypeStruct((B, nk, DIFF_HEADS * DIFF_V, tk), bf),
            ],
            compiler_params=pltpu.CompilerParams(
                dimension_semantics=("arbitrary", "arbitrary"), vmem_limit_bytes=VMEM_LIMIT),
            name="proj",
        )(x, attn_norm_g[l].reshape(1, -1), w_in_t, _gain_cols(q_a_norm_g[l], tm_proj), w_q_t,
          _gain_cols(kv_a_norm_g[l], tm_proj), w_k_t, w_v_t,
          _gain_cols(mla_q_norm_g[l], tm_proj), _gain_cols(mla_k_norm_g[l], tm_proj),
          _gain_cols(diff_q_norm_g[l], tm_proj), _gain_cols(diff_k_norm_g[l], tm_proj),
          cos_a, sin_a, cos_b, sin_b)

        nq = S // tq
        o_a = pl.pallas_call(
            functools.partial(_mla_kernel, tq=tq, tk=tk),
            grid=(B, nq),
            in_specs=[
                pl.BlockSpec((1, MLA_HEADS * LANES, tq), lambda b, i: (b, 0, i)),
                pl.BlockSpec((1, nk, tk, MLA_HEADS * LANES), lambda b, i: (b, 0, 0, 0)),
                pl.BlockSpec((1, nk, MLA_HEADS * MLA_V, tk), lambda b, i: (b, 0, 0, 0)),
            ],
            out_specs=pl.BlockSpec((1, MLA_HEADS * MLA_V, tq), lambda b, i: (b, 0, i)),
            out_shape=jax.ShapeDtypeStruct((B, MLA_HEADS * MLA_V, S), bf),
            scratch_shapes=_attn_scratch(MLA_HEADS, MLA_V, tq, tk),
            compiler_params=pltpu.CompilerParams(
                dimension_semantics=("arbitrary", "arbitrary"), vmem_limit_bytes=VMEM_LIMIT),
            name="mla_attn",
        )(qt, k_a.reshape(B, nk, tk, MLA_HEADS * LANES), vt)

        lam_init = 0.8 - 0.6 * math.exp(-0.3 * l)
        lam_spec = _const_spec((1, DIFF_D))
        o_b = pl.pallas_call(
            functools.partial(_diff_kernel, tq=tq, tk=tk, lam_init=lam_init),
            grid=(B, nq),
            in_specs=[
                lam_spec, lam_spec, lam_spec, lam_spec,
                _const_spec((DIFF_V, 1)),
                pl.BlockSpec((1, DIFF_HEADS * 2 * LANES, tq), lambda b, i: (b, 0, i)),
                pl.BlockSpec((1, nk, tk, DIFF_HEADS * LANES), lambda b, i: (b, 0, 0, 0)),
                pl.BlockSpec((1, nk, DIFF_HEADS * DIFF_V, tk), lambda b, i: (b, 0, 0, 0)),
            ],
            out_specs=pl.BlockSpec((1, DIFF_HEADS * DIFF_V, tq), lambda b, i: (b, 0, i)),
            out_shape=jax.ShapeDtypeStruct((B, DIFF_HEADS * DIFF_V, S), bf),
            scratch_shapes=_attn_scratch(2 * DIFF_HEADS, DIFF_V, tq, tk),
            compiler_params=pltpu.CompilerParams(
                dimension_semantics=("arbitrary", "arbitrary"), vmem_limit_bytes=VMEM_LIMIT),
            name="diff_attn",
        )(lambda_q1[l].reshape(1, -1), lambda_k1[l].reshape(1, -1),
          lambda_q2[l].reshape(1, -1), lambda_k2[l].reshape(1, -1),
          diff_subln_g[l].reshape(-1, 1), dqt, dk.reshape(B, nk, tk, DIFF_HEADS * LANES), dvt)

        n_a = MLA_HEADS * MLA_V
        x = pl.pallas_call(
            _ffn_kernel,
            grid=(B, S // tm_ffn),
            in_specs=[
                pl.BlockSpec((1, tm_ffn, D_MODEL), lambda b, s: (b, s, 0)),
                pl.BlockSpec((1, n_a, tm_ffn), lambda b, s: (b, 0, s)),
                pl.BlockSpec((1, D_MODEL - n_a, tm_ffn), lambda b, s: (b, 0, s)),
                _const_spec((n_a, D_MODEL)),
                _const_spec((D_MODEL - n_a, D_MODEL)),
                _const_spec((1, D_MODEL)),
                _const_spec((D_MODEL, D_FF)),
                _const_spec((D_MODEL, D_FF)),
                _const_spec((CONV_WIDTH, D_FF)),
                _const_spec((1, D_FF)),
                _const_spec((D_FF, D_MODEL)),
            ],
            out_specs=pl.BlockSpec((1, tm_ffn, D_MODEL), lambda b, s: (b, s, 0)),
            out_shape=jax.ShapeDtypeStruct((B, S, D_MODEL), x.dtype),
            scratch_shapes=[pltpu.VMEM((8, D_FF), jnp.float32),
                            pltpu.VMEM((tm_ffn, D_FF), bf)],
            compiler_params=pltpu.CompilerParams(
                dimension_semantics=("arbitrary", "arbitrary"), vmem_limit_bytes=VMEM_LIMIT),
            name="ffn",
        )(x, o_a, o_b, w_out[l][:n_a].astype(bf), w_out[l][n_a:].astype(bf),
          ffn_norm_g[l].reshape(1, -1), w_gate[l].astype(bf), w_up[l].astype(bf),
          conv_w[l], conv_b[l].reshape(1, -1), w_down[l].astype(bf))
    return x
```

```python
import functools
import math
from typing import NamedTuple

import jax
import jax.numpy as jnp
from jax import lax
from jax.experimental import pallas as pl
from jax.experimental.pallas import tpu as pltpu

D_MODEL = 1024
MLA_HEADS = 8
MLA_NOPE = 64
MLA_ROPE = 32
MLA_V = 64
MLA_QK = MLA_NOPE + MLA_ROPE
Q_RANK = 384
KV_RANK = 256
DIFF_HEADS = 4
DIFF_D = 64
DIFF_V = 2 * DIFF_D
D_FF = 2816
CONV_WIDTH = 3
ROPE_THETA = 10000.0
EPS = 1e-6
LANES = 128
SUM_ROWS = 16
LOG2E = math.log2(math.e)

R_Q = 0
R_KV = R_Q + Q_RANK
R_KPE = R_KV + KV_RANK
R_DQ = R_KPE + MLA_ROPE
R_DK = R_DQ + DIFF_HEADS * 2 * DIFF_D
R_DV = R_DK + DIFF_HEADS * 2 * DIFF_D
IN_COLS = R_DV + DIFF_HEADS * DIFF_V

VMEM_LIMIT = 56 * 1024 * 1024


def _rms_rows(x, g):
    ms = jnp.mean(x * x, axis=-1, keepdims=True)
    return x * lax.rsqrt(ms + EPS) * g


def _rms_cols(xt, n, scale):
    ms = jnp.sum(xt * xt, axis=0, keepdims=True) * (1.0 / n)
    return scale * lax.rsqrt(scale * scale * ms + EPS)


def _rope_cols(xt, cos, sin):
    half = xt.shape[0] // 2
    x1, x2 = xt[:half], xt[half:]
    return x1 * cos - x2 * sin, x2 * cos + x1 * sin


def _proj_kernel(x_ref, g_attn_ref, w_in_ref, g_qa_ref, w_q_ref, g_kva_ref, w_k_ref, w_v_ref,
                 g_q_ref, g_k_ref, g_dq_ref, g_dk_ref,
                 cos_a_ref, sin_a_ref, cos_b_ref, sin_b_ref,
                 qt_ref, k_ref, vt_ref, dqt_ref, dk_ref, dvt_ref):
    tm = x_ref.shape[1]
    bf = jnp.bfloat16
    x = x_ref[0]
    r_tok = lax.rsqrt(jnp.mean(x * x, axis=-1, keepdims=True) + EPS)
    r_tok = jnp.transpose(jnp.broadcast_to(r_tok, (tm, LANES)))[0:1, :]
    h = (x * g_attn_ref[...]).astype(bf)

    def in_proj(r0, r1):
        return lax.dot_general(w_in_ref[r0:r1, :], h, (((1,), (1,)), ((), ())),
                               preferred_element_type=jnp.float32)

    cos_a, sin_a = cos_a_ref[...], sin_a_ref[...]
    cos_b, sin_b = cos_b_ref[...], sin_b_ref[...]
    zeros_pad = jnp.zeros((LANES - MLA_QK, tm), jnp.float32)
    one = jnp.ones((1, tm), jnp.float32)

    def head_a(nope, pe_roped, g):
        nope = nope * _rms_cols(nope, MLA_NOPE, one) * g[:MLA_NOPE]
        return jnp.concatenate([nope, *pe_roped, zeros_pad], axis=0)

    def rope_a(pe, g, scale):
        pe = pe * _rms_cols(pe, MLA_ROPE, scale) * g[MLA_NOPE:]
        return _rope_cols(pe, cos_a, sin_a)

    def head_b(xt, g):
        xt = xt * _rms_cols(xt, DIFF_D, r_tok) * g
        return jnp.concatenate(_rope_cols(xt, cos_b, sin_b), axis=0)

    lat = in_proj(R_Q, R_DQ)
    dq = in_proj(R_DQ, R_DK)
    dk = in_proj(R_DK, R_DV)

    cq = lat[R_Q:R_Q + Q_RANK]
    cq = (cq * _rms_cols(cq, Q_RANK, r_tok) * g_qa_ref[...]).astype(bf)
    q = jnp.dot(w_q_ref[...], cq, preferred_element_type=jnp.float32)
    ckv = lat[R_KV:R_KV + KV_RANK]
    ckv = (ckv * _rms_cols(ckv, KV_RANK, r_tok) * g_kva_ref[...]).astype(bf)
    kn = jnp.dot(w_k_ref[...], ckv, preferred_element_type=jnp.float32)
    v = jnp.dot(w_v_ref[...], ckv, preferred_element_type=jnp.float32).astype(bf)
    dv = (in_proj(R_DV, IN_COLS) * r_tok).astype(bf)
    tk = vt_ref.shape[3]
    for t in range(tm // tk):
        vt_ref[0, t] = v[:, t * tk:(t + 1) * tk]
        dvt_ref[0, t] = dv[:, t * tk:(t + 1) * tk]

    g_q = g_q_ref[...] * (MLA_QK ** -0.5 * LOG2E)
    for hd in range(MLA_HEADS):
        r0 = hd * MLA_QK
        qt_ref[0, hd * LANES:(hd + 1) * LANES, :] = head_a(
            q[r0:r0 + MLA_NOPE], rope_a(q[r0 + MLA_NOPE:r0 + MLA_QK], g_q, one), g_q).astype(bf)

    g_k = g_k_ref[...]
    kpe = rope_a(lat[R_KPE:R_KPE + MLA_ROPE], g_k, r_tok)
    for hd in range(MLA_HEADS):
        kt = head_a(kn[hd * MLA_NOPE:(hd + 1) * MLA_NOPE], kpe, g_k)
        k_ref[0, :, hd * LANES:(hd + 1) * LANES] = kt.T.astype(bf)

    g_dq = g_dq_ref[...] * (DIFF_D ** -0.5 * LOG2E)
    g_dk = g_dk_ref[...]
    zeros_half = jnp.zeros((DIFF_D, tm), bf)
    for hd in range(DIFF_HEADS):
        r0 = hd * 2 * DIFF_D
        q1 = head_b(dq[r0:r0 + DIFF_D], g_dq).astype(bf)
        q2 = head_b(dq[r0 + DIFF_D:r0 + 2 * DIFF_D], g_dq).astype(bf)
        b0 = 2 * hd * LANES
        dqt_ref[0, b0:b0 + LANES, :] = jnp.concatenate([q1, zeros_half], axis=0)
        dqt_ref[0, b0 + LANES:b0 + 2 * LANES, :] = jnp.concatenate([zeros_half, q2], axis=0)
        dkt = jnp.concatenate([head_b(dk[r0:r0 + DIFF_D], g_dk),
                               head_b(dk[r0 + DIFF_D:r0 + 2 * DIFF_D], g_dk)], axis=0)
        dk_ref[0, :, hd * LANES:(hd + 1) * LANES] = dkt.T.astype(bf)


def _causal_mask(s_t, tk, tq):
    key = lax.broadcasted_iota(jnp.int32, (tk, tq), 0)
    qry = lax.broadcasted_iota(jnp.int32, (tk, tq), 1)
    return jnp.where(key <= qry, s_t, -jnp.inf)


class _Chain(NamedTuple):
    k_lanes: slice
    q_rows: slice
    v_rows: slice
    acc_rows: slice


def _pipelined_sweep(chains, qt_ref, k_ref, vt_ref, qi, tq, tk, p_ref, al_ref, m_ref, acc_ref):
    assert tq == 2 * tk
    lower, upper = slice(0, tk), slice(tk, tq)

    def stage_a(j, cols):
        return [jnp.dot(k_ref[0, j, :, ch.k_lanes], qt_ref[0, ch.q_rows, cols],
                        preferred_element_type=jnp.float32) for ch in chains]

    def stage_b(s_all, slot, cols, first=False):
        for c, s_t in enumerate(s_all):
            if first:
                m_new = jnp.max(s_t, axis=0, keepdims=True)
            else:
                m_old = m_ref[c, :, cols]
                m_new = jnp.maximum(m_old, jnp.max(s_t, axis=0, keepdims=True))
                al_ref[slot, c, :, cols] = jnp.exp2(m_old - m_new)
            m_ref[c, :, cols] = m_new
            p_ref[slot, c, :, cols] = jnp.exp2(s_t - m_new).astype(p_ref.dtype)

    def stage_c(j, slot, first=False):
        ones = jnp.ones((SUM_ROWS, tk), vt_ref.dtype)
        for c, ch in enumerate(chains):
            vt_ones = jnp.concatenate([vt_ref[0, j, ch.v_rows, :], ones], axis=0)
            pv = jnp.dot(vt_ones, p_ref[slot, c], preferred_element_type=jnp.float32)
            if first:
                acc_ref[ch.acc_rows, :] = pv
            else:
                acc_ref[ch.acc_rows, :] = al_ref[slot, c] * acc_ref[ch.acc_rows, :] + pv

    d0, d1 = 2 * qi, 2 * qi + 1
    s_all = stage_a(d0, slice(0, tq))
    stage_b([jnp.concatenate([_causal_mask(s_t[:, lower], tk, tk), s_t[:, upper]], axis=1)
             for s_t in s_all], 0, slice(0, tq), first=True)

    s_all = stage_a(d1, upper)
    stage_c(d0, 0, first=True)
    stage_b([_causal_mask(s_t, tk, tk) for s_t in s_all], 1, upper)
    p_ref[1, :, :, lower] = jnp.zeros((len(chains), tk, tk), p_ref.dtype)
    al_ref[1, :, :, lower] = jnp.ones((len(chains), 1, tk), al_ref.dtype)

    def two_steps(k, carry):
        s_all = stage_a(2 * k, slice(0, tq))
        stage_c(jnp.where(k == 0, d1, 2 * k - 1), 1)
        stage_b(s_all, 0, slice(0, tq))
        s_all = stage_a(2 * k + 1, slice(0, tq))
        stage_c(2 * k, 0)
        stage_b(s_all, 1, slice(0, tq))
        return carry

    lax.fori_loop(0, qi, two_steps, 0)
    stage_c(jnp.where(qi == 0, d1, 2 * qi - 1), 1)


def _mla_kernel(qt_ref, k_ref, vt_ref, o_ref, p_ref, al_ref, m_ref, acc_ref, *, tq, tk):
    acc_rows = MLA_V + SUM_ROWS
    chains = [_Chain(slice(hd * LANES, (hd + 1) * LANES), slice(hd * LANES, (hd + 1) * LANES),
                     slice(hd * MLA_V, (hd + 1) * MLA_V), slice(hd * acc_rows, (hd + 1) * acc_rows))
              for hd in range(MLA_HEADS)]
    _pipelined_sweep(chains, qt_ref, k_ref, vt_ref, pl.program_id(1), tq, tk,
                     p_ref, al_ref, m_ref, acc_ref)
    for hd in range(MLA_HEADS):
        a0 = hd * acc_rows
        o_ref[0, hd * MLA_V:(hd + 1) * MLA_V, :] = (
            acc_ref[a0:a0 + MLA_V, :] / acc_ref[a0 + MLA_V:a0 + MLA_V + 1, :]).astype(o_ref.dtype)


def _diff_kernel(lq1_ref, lk1_ref, lq2_ref, lk2_ref, g_sub_ref, dqt_ref, dk_ref, dvt_ref, o_ref,
                 p_ref, al_ref, m_ref, acc_ref, *, tq, tk, lam_init):
    acc_rows = DIFF_V + SUM_ROWS
    lam = (jnp.exp(jnp.sum(lq1_ref[...] * lk1_ref[...], axis=-1, keepdims=True))
           - jnp.exp(jnp.sum(lq2_ref[...] * lk2_ref[...], axis=-1, keepdims=True))
           + lam_init)
    chains = [_Chain(slice((c // 2) * LANES, (c // 2 + 1) * LANES), slice(c * LANES, (c + 1) * LANES),
                     slice((c // 2) * DIFF_V, (c // 2 + 1) * DIFF_V),
                     slice(c * acc_rows, (c + 1) * acc_rows))
              for c in range(2 * DIFF_HEADS)]
    _pipelined_sweep(chains, dqt_ref, dk_ref, dvt_ref, pl.program_id(1), tq, tk,
                     p_ref, al_ref, m_ref, acc_ref)

    def normalised(c):
        a0 = c * acc_rows
        return acc_ref[a0:a0 + DIFF_V, :] / acc_ref[a0 + DIFF_V:a0 + DIFF_V + 1, :]

    for hd in range(DIFF_HEADS):
        o = normalised(2 * hd) - lam * normalised(2 * hd + 1)
        ms = jnp.mean(o * o, axis=0, keepdims=True)
        o = o * lax.rsqrt(ms + EPS) * g_sub_ref[...] * (1.0 - lam_init)
        o_ref[0, hd * DIFF_V:(hd + 1) * DIFF_V, :] = o.astype(o_ref.dtype)


FF_CHUNK = 512


def _ffn_kernel(x_ref, oa_ref, ob_ref, w_out_ref, g_ffn_ref, w_gate_ref, w_up_ref,
                conv_w_ref, conv_b_ref, w_down_ref, out_ref, prev_ref, y_ref):
    si = pl.program_id(1)
    tm = x_ref.shape[1]
    contract0 = (((0,), (0,)), ((), ()))
    n_a = oa_ref.shape[1]
    mix = (lax.dot_general(oa_ref[0], w_out_ref[:n_a, :], contract0,
                           preferred_element_type=jnp.float32)
           + lax.dot_general(ob_ref[0], w_out_ref[n_a:, :], contract0,
                             preferred_element_type=jnp.float32))
    x1 = x_ref[0] + mix
    h = _rms_rows(x1, g_ffn_ref[...]).astype(jnp.bfloat16)

    @pl.when(si == 0)
    def _():
        prev_ref[...] = jnp.zeros_like(prev_ref)

    for c0 in range(0, D_FF, FF_CHUNK):
        cw = min(FF_CHUNK, D_FF - c0)
        g = jnp.dot(h, w_gate_ref[:, c0:c0 + cw], preferred_element_type=jnp.float32)
        u = jnp.dot(h, w_up_ref[:, c0:c0 + cw], preferred_element_type=jnp.float32)
        row = lax.broadcasted_iota(jnp.int32, (tm, cw), 0)
        p1 = prev_ref[7:8, c0:c0 + cw]
        p2 = prev_ref[6:7, c0:c0 + cw]
        g1 = jnp.where(row == 0, p1, pltpu.roll(g, 1, axis=0))
        g2 = jnp.where(row == 0, p2, jnp.where(row == 1, p1, pltpu.roll(g, 2, axis=0)))
        prev_ref[:, c0:c0 + cw] = g[tm - 8:tm, :]
        cg = (conv_b_ref[:, c0:c0 + cw] + g2 * conv_w_ref[0:1, c0:c0 + cw]
              + g1 * conv_w_ref[1:2, c0:c0 + cw] + g * conv_w_ref[2:3, c0:c0 + cw])
        y_ref[:, c0:c0 + cw] = (jax.nn.silu(cg) * u).astype(y_ref.dtype)

    out_ref[0] = x1 + jnp.dot(y_ref[...], w_down_ref[...], preferred_element_type=jnp.float32)


def _rope_tables(seq):
    pos = jnp.arange(seq, dtype=jnp.float32)[:, None]

    def tables(dim):
        inv = 1.0 / (ROPE_THETA ** (jnp.arange(0, dim, 2, dtype=jnp.float32) / dim))
        ang = pos * inv[None, :]
        return jnp.cos(ang), jnp.sin(ang)

    ca, sa = tables(MLA_ROPE)
    cb, sb = tables(DIFF_D)
    return ca.T, sa.T, cb.T, sb.T


def _attn_scratch(n_chains, dv, tq, tk):
    return [pltpu.VMEM((2, n_chains, tk, tq), jnp.bfloat16),
            pltpu.VMEM((2, n_chains, 1, tq), jnp.float32),
            pltpu.VMEM((n_chains, 1, tq), jnp.float32),
            pltpu.VMEM((n_chains * (dv + SUM_ROWS), tq), jnp.float32)]


def _const_spec(shape):
    return pl.BlockSpec(shape, lambda *_: (0,) * len(shape))


def kernel(x, attn_norm_g, w_in, q_a_norm_g, w_q_up, kv_a_norm_g, w_kv_up, mla_q_norm_g,
           mla_k_norm_g, diff_q_norm_g, diff_k_norm_g, lambda_q1, lambda_k1, lambda_q2, lambda_k2,
           diff_subln_g, w_out, ffn_norm_g, w_gate, w_up, conv_w, conv_b, w_down):
    B, S, _ = x.shape
    depth = w_in.shape[0]
    bf = jnp.bfloat16
    cos_a, sin_a, cos_b, sin_b = _rope_tables(S)

    tk = 256
    tq = 2 * tk
    tm_proj = 1024
    tm_ffn = 1024
    assert tm_proj % tk == 0 and S % tq == 0 and S % tm_proj == 0 and S % tm_ffn == 0
    nk = S // tk

    for l in range(depth):
        w_in_t = w_in[l].T.astype(bf)
        w_q_t = w_q_up[l].T.astype(bf)
        wkv_t = w_kv_up[l].T.reshape(MLA_HEADS, MLA_NOPE + MLA_V, KV_RANK)
        w_k_t = wkv_t[:, :MLA_NOPE].reshape(MLA_HEADS * MLA_NOPE, KV_RANK).astype(bf)
        w_v_t = wkv_t[:, MLA_NOPE:].reshape(MLA_HEADS * MLA_V, KV_RANK).astype(bf)

        n_tok_tiles = S // tm_proj
        tok3 = lambda b, s: (b, s, 0)
        feat3 = lambda b, s: (b, 0, s)
        blk4 = lambda b, s: (b, s, 0, 0)
        rope_a_spec = pl.BlockSpec((MLA_ROPE // 2, tm_proj), lambda b, s: (0, s))
        rope_b_spec = pl.BlockSpec((DIFF_D // 2, tm_proj), lambda b, s: (0, s))
        qt, k_a, vt, dqt, dk, dvt = pl.pallas_call(
            _proj_kernel,
            grid=(B, n_tok_tiles),
            in_specs=[
                pl.BlockSpec((1, tm_proj, D_MODEL), tok3),
                _const_spec((1, D_MODEL)),
                _const_spec((IN_COLS, D_MODEL)),
                _const_spec((Q_RANK, 1)),
                _const_spec((MLA_HEADS * MLA_QK, Q_RANK)),
                _const_spec((KV_RANK, 1)),
                _const_spec((MLA_HEADS * MLA_NOPE, KV_RANK)),
                _const_spec((MLA_HEADS * MLA_V, KV_RANK)),
                _const_spec((MLA_QK, 1)), _const_spec((MLA_QK, 1)),
                _const_spec((DIFF_D, 1)), _const_spec((DIFF_D, 1)),
                rope_a_spec, rope_a_spec, rope_b_spec, rope_b_spec,
            ],
            out_specs=[
                pl.BlockSpec((1, MLA_HEADS * LANES, tm_proj), feat3),
                pl.BlockSpec((1, tm_proj, MLA_HEADS * LANES), tok3),
                pl.BlockSpec((1, tm_proj // tk, MLA_HEADS * MLA_V, tk), blk4),
                pl.BlockSpec((1, DIFF_HEADS * 2 * LANES, tm_proj), feat3),
                pl.BlockSpec((1, tm_proj, DIFF_HEADS * LANES), tok3),
                pl.BlockSpec((1, tm_proj // tk, DIFF_HEADS * DIFF_V, tk), blk4),
            ],
            out_shape=[
                jax.ShapeDtypeStruct((B, MLA_HEADS * LANES, S), bf),
                jax.ShapeDtypeStruct((B, S, MLA_HEADS * LANES), bf),
                jax.ShapeDtypeStruct((B, nk, MLA_HEADS * MLA_V, tk), bf),
                jax.ShapeDtypeStruct((B, DIFF_HEADS * 2 * LANES, S), bf),
                jax.ShapeDtypeStruct((B, S, DIFF_HEADS * LANES), bf),
                jax.ShapeDtypeStruct((B, nk, DIFF_HEADS * DIFF_V, tk), bf),
            ],
            compiler_params=pltpu.CompilerParams(
                dimension_semantics=("arbitrary", "arbitrary"), vmem_limit_bytes=VMEM_LIMIT),
            name="proj",
        )(x, attn_norm_g[l].reshape(1, -1), w_in_t, q_a_norm_g[l].reshape(-1, 1), w_q_t,
          kv_a_norm_g[l].reshape(-1, 1), w_k_t, w_v_t,
          mla_q_norm_g[l].reshape(-1, 1), mla_k_norm_g[l].reshape(-1, 1),
          diff_q_norm_g[l].reshape(-1, 1), diff_k_norm_g[l].reshape(-1, 1),
          cos_a, sin_a, cos_b, sin_b)

        nq = S // tq
        o_a = pl.pallas_call(
            functools.partial(_mla_kernel, tq=tq, tk=tk),
            grid=(B, nq),
            in_specs=[
                pl.BlockSpec((1, MLA_HEADS * LANES, tq), lambda b, i: (b, 0, i)),
                pl.BlockSpec((1, nk, tk, MLA_HEADS * LANES), lambda b, i: (b, 0, 0, 0)),
                pl.BlockSpec((1, nk, MLA_HEADS * MLA_V, tk), lambda b, i: (b, 0, 0, 0)),
            ],
            out_specs=pl.BlockSpec((1, MLA_HEADS * MLA_V, tq), lambda b, i: (b, 0, i)),
            out_shape=jax.ShapeDtypeStruct((B, MLA_HEADS * MLA_V, S), bf),
            scratch_shapes=_attn_scratch(MLA_HEADS, MLA_V, tq, tk),
            compiler_params=pltpu.CompilerParams(
                dimension_semantics=("arbitrary", "arbitrary"), vmem_limit_bytes=VMEM_LIMIT),
            name="mla_attn",
        )(qt, k_a.reshape(B, nk, tk, MLA_HEADS * LANES), vt)

        lam_init = 0.8 - 0.6 * math.exp(-0.3 * l)
        lam_spec = _const_spec((1, DIFF_D))
        o_b = pl.pallas_call(
            functools.partial(_diff_kernel, tq=tq, tk=tk, lam_init=lam_init),
            grid=(B, nq),
            in_specs=[
                lam_spec, lam_spec, lam_spec, lam_spec,
                _const_spec((DIFF_V, 1)),
                pl.BlockSpec((1, DIFF_HEADS * 2 * LANES, tq), lambda b, i: (b, 0, i)),
                pl.BlockSpec((1, nk, tk, DIFF_HEADS * LANES), lambda b, i: (b, 0, 0, 0)),
                pl.BlockSpec((1, nk, DIFF_HEADS * DIFF_V, tk), lambda b, i: (b, 0, 0, 0)),
            ],
            out_specs=pl.BlockSpec((1, DIFF_HEADS * DIFF_V, tq), lambda b, i: (b, 0, i)),
            out_shape=jax.ShapeDtypeStruct((B, DIFF_HEADS * DIFF_V, S), bf),
            scratch_shapes=_attn_scratch(2 * DIFF_HEADS, DIFF_V, tq, tk),
            compiler_params=pltpu.CompilerParams(
                dimension_semantics=("arbitrary", "arbitrary"), vmem_limit_bytes=VMEM_LIMIT),
            name="diff_attn",
        )(lambda_q1[l].reshape(1, -1), lambda_k1[l].reshape(1, -1),
          lambda_q2[l].reshape(1, -1), lambda_k2[l].reshape(1, -1),
          diff_subln_g[l].reshape(-1, 1), dqt, dk.reshape(B, nk, tk, DIFF_HEADS * LANES), dvt)

        n_a = MLA_HEADS * MLA_V
        x = pl.pallas_call(
            _ffn_kernel,
            grid=(B, S // tm_ffn),
            in_specs=[
                pl.BlockSpec((1, tm_ffn, D_MODEL), lambda b, s: (b, s, 0)),
                pl.BlockSpec((1, n_a, tm_ffn), lambda b, s: (b, 0, s)),
                pl.BlockSpec((1, D_MODEL - n_a, tm_ffn), lambda b, s: (b, 0, s)),
                _const_spec((D_MODEL, D_MODEL)),
                _const_spec((1, D_MODEL)),
                _const_spec((D_MODEL, D_FF)),
                _const_spec((D_MODEL, D_FF)),
                _const_spec((CONV_WIDTH, D_FF)),
                _const_spec((1, D_FF)),
                _const_spec((D_FF, D_MODEL)),
            ],
            out_specs=pl.BlockSpec((1, tm_ffn, D_MODEL), lambda b, s: (b, s, 0)),
            out_shape=jax.ShapeDtypeStruct((B, S, D_MODEL), x.dtype),
            scratch_shapes=[pltpu.VMEM((8, D_FF), jnp.float32),
                            pltpu.VMEM((tm_ffn, D_FF), bf)],
            compiler_params=pltpu.CompilerParams(
                dimension_semantics=("arbitrary", "arbitrary"), vmem_limit_bytes=VMEM_LIMIT),
            name="ffn",
        )(x, o_a, o_b, w_out[l].astype(bf),
          ffn_norm_g[l].reshape(1, -1), w_gate[l].astype(bf), w_up[l].astype(bf),
          conv_w[l], conv_b[l].reshape(1, -1), w_down[l].astype(bf))
    return x
```

```python
import functools
import math
from typing import Any, NamedTuple

import jax
import jax.numpy as jnp
from jax import lax
from jax.experimental import pallas as pl
from jax.experimental.pallas import tpu as pltpu

D_MODEL = 1024
MLA_HEADS = 8
MLA_NOPE = 64
MLA_ROPE = 32
MLA_V = 64
MLA_QK = MLA_NOPE + MLA_ROPE
Q_RANK = 384
KV_RANK = 256
DIFF_HEADS = 4
DIFF_D = 64
DIFF_V = 2 * DIFF_D
D_FF = 2816
CONV_WIDTH = 3
ROPE_THETA = 10000.0
EPS = 1e-6
LANES = 128
SUM_ROWS = 16
LOG2E = math.log2(math.e)

R_Q = 0
R_KV = R_Q + Q_RANK
R_KPE = R_KV + KV_RANK
R_DQ = R_KPE + MLA_ROPE
R_DK = R_DQ + DIFF_HEADS * 2 * DIFF_D
R_DV = R_DK + DIFF_HEADS * 2 * DIFF_D
IN_COLS = R_DV + DIFF_HEADS * DIFF_V

VMEM_LIMIT = 56 * 1024 * 1024


def _rms_rows(x, g):
    ms = jnp.mean(x * x, axis=-1, keepdims=True)
    return x * lax.rsqrt(ms + EPS) * g


def _rms_cols(xt, n, scale):
    ms = jnp.sum(xt * xt, axis=0, keepdims=True) * (1.0 / n)
    return scale * lax.rsqrt(scale * scale * ms + EPS)


def _rope_cols(xt, cos, sin):
    half = xt.shape[0] // 2
    x1, x2 = xt[:half], xt[half:]
    return x1 * cos - x2 * sin, x2 * cos + x1 * sin


def _proj_kernel(x_ref, g_attn_ref, w_in_ref, g_qa_ref, w_q_ref, g_kva_ref, w_k_ref, w_v_ref,
                 g_q_ref, g_k_ref, g_dq_ref, g_dk_ref,
                 cos_a_ref, sin_a_ref, cos_b_ref, sin_b_ref,
                 qt_ref, k_ref, vt_ref, dqt_ref, dk_ref, dvt_ref):
    tm = x_ref.shape[1]
    bf = jnp.bfloat16
    x = x_ref[0]
    r_tok = lax.rsqrt(jnp.mean(x * x, axis=-1, keepdims=True) + EPS)
    r_tok = jnp.transpose(jnp.broadcast_to(r_tok, (tm, LANES)))[0:1, :]
    h = (x * g_attn_ref[...]).astype(bf)

    def in_proj(r0, r1):
        return lax.dot_general(w_in_ref[r0:r1, :], h, (((1,), (1,)), ((), ())),
                               preferred_element_type=jnp.float32)

    cos_a, sin_a = cos_a_ref[...], sin_a_ref[...]
    cos_b, sin_b = cos_b_ref[...], sin_b_ref[...]
    zeros_pad = jnp.zeros((LANES - MLA_QK, tm), jnp.float32)
    one = jnp.ones((1, tm), jnp.float32)

    def head_a(nope, pe_roped, g):
        nope = nope * _rms_cols(nope, MLA_NOPE, one) * g[:MLA_NOPE]
        return jnp.concatenate([nope, *pe_roped, zeros_pad], axis=0)

    def rope_a(pe, g, scale):
        pe = pe * _rms_cols(pe, MLA_ROPE, scale) * g[MLA_NOPE:]
        return _rope_cols(pe, cos_a, sin_a)

    def head_b(xt, g):
        xt = xt * _rms_cols(xt, DIFF_D, r_tok) * g
        return jnp.concatenate(_rope_cols(xt, cos_b, sin_b), axis=0)

    lat = in_proj(R_Q, R_DQ)
    dq = in_proj(R_DQ, R_DK)
    dk = in_proj(R_DK, R_DV)

    cq = lat[R_Q:R_Q + Q_RANK]
    cq = (cq * _rms_cols(cq, Q_RANK, r_tok) * g_qa_ref[...]).astype(bf)
    q = jnp.dot(w_q_ref[...], cq, preferred_element_type=jnp.float32)
    ckv = lat[R_KV:R_KV + KV_RANK]
    ckv = (ckv * _rms_cols(ckv, KV_RANK, r_tok) * g_kva_ref[...]).astype(bf)
    kn = jnp.dot(w_k_ref[...], ckv, preferred_element_type=jnp.float32)
    v = jnp.dot(w_v_ref[...], ckv, preferred_element_type=jnp.float32).astype(bf)
    dv = (in_proj(R_DV, IN_COLS) * r_tok).astype(bf)
    tk = vt_ref.shape[3]
    for t in range(tm // tk):
        vt_ref[0, t] = v[:, t * tk:(t + 1) * tk]
        dvt_ref[0, t] = dv[:, t * tk:(t + 1) * tk]

    g_q = g_q_ref[...] * (MLA_QK ** -0.5 * LOG2E)
    for hd in range(MLA_HEADS):
        r0 = hd * MLA_QK
        qt_ref[0, hd * LANES:(hd + 1) * LANES, :] = head_a(
            q[r0:r0 + MLA_NOPE], rope_a(q[r0 + MLA_NOPE:r0 + MLA_QK], g_q, one), g_q).astype(bf)

    g_k = g_k_ref[...]
    kpe = rope_a(lat[R_KPE:R_KPE + MLA_ROPE], g_k, r_tok)
    for hd in range(MLA_HEADS):
        kt = head_a(kn[hd * MLA_NOPE:(hd + 1) * MLA_NOPE], kpe, g_k)
        k_ref[0, :, hd * LANES:(hd + 1) * LANES] = kt.T.astype(bf)

    g_dq = g_dq_ref[...] * (DIFF_D ** -0.5 * LOG2E)
    g_dk = g_dk_ref[...]
    zeros_half = jnp.zeros((DIFF_D, tm), bf)
    for hd in range(DIFF_HEADS):
        r0 = hd * 2 * DIFF_D
        q1 = head_b(dq[r0:r0 + DIFF_D], g_dq).astype(bf)
        q2 = head_b(dq[r0 + DIFF_D:r0 + 2 * DIFF_D], g_dq).astype(bf)
        b0 = 2 * hd * LANES
        dqt_ref[0, b0:b0 + LANES, :] = jnp.concatenate([q1, zeros_half], axis=0)
        dqt_ref[0, b0 + LANES:b0 + 2 * LANES, :] = jnp.concatenate([zeros_half, q2], axis=0)
        dkt = jnp.concatenate([head_b(dk[r0:r0 + DIFF_D], g_dk),
                               head_b(dk[r0 + DIFF_D:r0 + 2 * DIFF_D], g_dk)], axis=0)
        dk_ref[0, :, hd * LANES:(hd + 1) * LANES] = dkt.T.astype(bf)


def _causal_mask(s_t, tk, tq):
    key = lax.broadcasted_iota(jnp.int32, (tk, tq), 0)
    qry = lax.broadcasted_iota(jnp.int32, (tk, tq), 1)
    return jnp.where(key <= qry, s_t, -jnp.inf)


class _Chain(NamedTuple):
    k_ref: Any
    k_lanes: slice
    qt_ref: Any
    q_rows: slice
    vt_ref: Any
    v_rows: slice
    acc_rows: slice


def _pipelined_sweep(chains, qi, tq, tk, p_ref, al_ref, m_ref, acc_ref):
    assert tq == 2 * tk
    lower, upper = slice(0, tk), slice(tk, tq)

    def stage_a(j, cols):
        return [jnp.dot(ch.k_ref[0, j, :, ch.k_lanes], ch.qt_ref[0, ch.q_rows, cols],
                        preferred_element_type=jnp.float32) for ch in chains]

    def stage_b(s_all, slot, cols, first=False):
        for c, s_t in enumerate(s_all):
            if first:
                m_new = jnp.max(s_t, axis=0, keepdims=True)
            else:
                m_old = m_ref[c, :, cols]
                m_new = jnp.maximum(m_old, jnp.max(s_t, axis=0, keepdims=True))
                al_ref[slot, c, :, cols] = jnp.exp2(m_old - m_new)
            m_ref[c, :, cols] = m_new
            p_ref[slot, c, :, cols] = jnp.exp2(s_t - m_new).astype(p_ref.dtype)

    def stage_c(j, slot, first=False):
        ones = jnp.ones((SUM_ROWS, tk), p_ref.dtype)
        for c, ch in enumerate(chains):
            vt_ones = jnp.concatenate([ch.vt_ref[0, j, ch.v_rows, :], ones], axis=0)
            pv = jnp.dot(vt_ones, p_ref[slot, c], preferred_element_type=jnp.float32)
            if first:
                acc_ref[ch.acc_rows, :] = pv
            else:
                acc_ref[ch.acc_rows, :] = al_ref[slot, c] * acc_ref[ch.acc_rows, :] + pv

    d0, d1 = 2 * qi, 2 * qi + 1
    s_all = stage_a(d0, slice(0, tq))
    stage_b([jnp.concatenate([_causal_mask(s_t[:, lower], tk, tk), s_t[:, upper]], axis=1)
             for s_t in s_all], 0, slice(0, tq), first=True)

    s_all = stage_a(d1, upper)
    stage_c(d0, 0, first=True)
    stage_b([_causal_mask(s_t, tk, tk) for s_t in s_all], 1, upper)
    p_ref[1, :, :, lower] = jnp.zeros((len(chains), tk, tk), p_ref.dtype)
    al_ref[1, :, :, lower] = jnp.ones((len(chains), 1, tk), al_ref.dtype)

    def two_steps(k, carry):
        s_all = stage_a(2 * k, slice(0, tq))
        stage_c(jnp.where(k == 0, d1, 2 * k - 1), 1)
        stage_b(s_all, 0, slice(0, tq))
        s_all = stage_a(2 * k + 1, slice(0, tq))
        stage_c(2 * k, 0)
        stage_b(s_all, 1, slice(0, tq))
        return carry

    lax.fori_loop(0, qi, two_steps, 0)
    stage_c(jnp.where(qi == 0, d1, 2 * qi - 1), 1)


def _attn_kernel(lq1_ref, lk1_ref, lq2_ref, lk2_ref, g_sub_ref,
                 qt_ref, k_ref, vt_ref, dqt_ref, dk_ref, dvt_ref, oa_ref, ob_ref,
                 p_ref, al_ref, m_ref, acc_ref, *, tq, tk, lam_init):
    rows_a, rows_b = MLA_V + SUM_ROWS, DIFF_V + SUM_ROWS
    base_b = MLA_HEADS * rows_a
    lane_group = lambda i: slice(i * LANES, (i + 1) * LANES)
    chains = [_Chain(k_ref, lane_group(hd), qt_ref, lane_group(hd),
                     vt_ref, slice(hd * MLA_V, (hd + 1) * MLA_V),
                     slice(hd * rows_a, (hd + 1) * rows_a)) for hd in range(MLA_HEADS)]
    chains += [_Chain(dk_ref, lane_group(c // 2), dqt_ref, lane_group(c),
                      dvt_ref, slice((c // 2) * DIFF_V, (c // 2 + 1) * DIFF_V),
                      slice(base_b + c * rows_b, base_b + (c + 1) * rows_b))
               for c in range(2 * DIFF_HEADS)]
    _pipelined_sweep(chains, pl.program_id(1), tq, tk, p_ref, al_ref, m_ref, acc_ref)

    def normalised(ch, dv):
        a0 = ch.acc_rows.start
        return acc_ref[a0:a0 + dv, :] / acc_ref[a0 + dv:a0 + dv + 1, :]

    for hd in range(MLA_HEADS):
        oa_ref[0, hd * MLA_V:(hd + 1) * MLA_V, :] = normalised(chains[hd], MLA_V).astype(oa_ref.dtype)

    lam = (jnp.exp(jnp.sum(lq1_ref[...] * lk1_ref[...], axis=-1, keepdims=True))
           - jnp.exp(jnp.sum(lq2_ref[...] * lk2_ref[...], axis=-1, keepdims=True))
           + lam_init)
    for hd in range(DIFF_HEADS):
        c = MLA_HEADS + 2 * hd
        o = normalised(chains[c], DIFF_V) - lam * normalised(chains[c + 1], DIFF_V)
        ms = jnp.mean(o * o, axis=0, keepdims=True)
        o = o * lax.rsqrt(ms + EPS) * g_sub_ref[...] * (1.0 - lam_init)
        ob_ref[0, hd * DIFF_V:(hd + 1) * DIFF_V, :] = o.astype(ob_ref.dtype)


FF_CHUNK = 512


def _ffn_kernel(x_ref, oa_ref, ob_ref, w_out_ref, g_ffn_ref, w_gate_ref, w_up_ref,
                conv_w_ref, conv_b_ref, w_down_ref, out_ref, prev_ref, y_ref):
    si = pl.program_id(1)
    tm = x_ref.shape[1]
    contract0 = (((0,), (0,)), ((), ()))
    n_a = oa_ref.shape[1]
    mix = (lax.dot_general(oa_ref[0], w_out_ref[:n_a, :], contract0,
                           preferred_element_type=jnp.float32)
           + lax.dot_general(ob_ref[0], w_out_ref[n_a:, :], contract0,
                             preferred_element_type=jnp.float32))
    x1 = x_ref[0] + mix
    h = _rms_rows(x1, g_ffn_ref[...]).astype(jnp.bfloat16)

    @pl.when(si == 0)
    def _():
        prev_ref[...] = jnp.zeros_like(prev_ref)

    for c0 in range(0, D_FF, FF_CHUNK):
        cw = min(FF_CHUNK, D_FF - c0)
        g = jnp.dot(h, w_gate_ref[:, c0:c0 + cw], preferred_element_type=jnp.float32)
        u = jnp.dot(h, w_up_ref[:, c0:c0 + cw], preferred_element_type=jnp.float32)
        row = lax.broadcasted_iota(jnp.int32, (tm, cw), 0)
        p1 = prev_ref[7:8, c0:c0 + cw]
        p2 = prev_ref[6:7, c0:c0 + cw]
        g1 = jnp.where(row == 0, p1, pltpu.roll(g, 1, axis=0))
        g2 = jnp.where(row == 0, p2, jnp.where(row == 1, p1, pltpu.roll(g, 2, axis=0)))
        prev_ref[:, c0:c0 + cw] = g[tm - 8:tm, :]
        cg = (conv_b_ref[:, c0:c0 + cw] + g2 * conv_w_ref[0:1, c0:c0 + cw]
              + g1 * conv_w_ref[1:2, c0:c0 + cw] + g * conv_w_ref[2:3, c0:c0 + cw])
        y_ref[:, c0:c0 + cw] = (jax.nn.silu(cg) * u).astype(y_ref.dtype)

    out_ref[0] = x1 + jnp.dot(y_ref[...], w_down_ref[...], preferred_element_type=jnp.float32)


def _rope_tables(seq):
    pos = jnp.arange(seq, dtype=jnp.float32)[:, None]

    def tables(dim):
        inv = 1.0 / (ROPE_THETA ** (jnp.arange(0, dim, 2, dtype=jnp.float32) / dim))
        ang = pos * inv[None, :]
        return jnp.cos(ang), jnp.sin(ang)

    ca, sa = tables(MLA_ROPE)
    cb, sb = tables(DIFF_D)
    return ca.T, sa.T, cb.T, sb.T


def _const_spec(shape):
    return pl.BlockSpec(shape, lambda *_: (0,) * len(shape))


def kernel(x, attn_norm_g, w_in, q_a_norm_g, w_q_up, kv_a_norm_g, w_kv_up, mla_q_norm_g,
           mla_k_norm_g, diff_q_norm_g, diff_k_norm_g, lambda_q1, lambda_k1, lambda_q2, lambda_k2,
           diff_subln_g, w_out, ffn_norm_g, w_gate, w_up, conv_w, conv_b, w_down):
    B, S, _ = x.shape
    depth = w_in.shape[0]
    bf = jnp.bfloat16
    cos_a, sin_a, cos_b, sin_b = _rope_tables(S)

    tk = 256
    tq = 2 * tk
    tm_proj = 1024
    tm_ffn = 1024
    assert tm_proj % tk == 0 and S % tq == 0 and S % tm_proj == 0 and S % tm_ffn == 0
    nk = S // tk

    for l in range(depth):
        w_in_t = w_in[l].T.astype(bf)
        w_q_t = w_q_up[l].T.astype(bf)
        wkv_t = w_kv_up[l].T.reshape(MLA_HEADS, MLA_NOPE + MLA_V, KV_RANK)
        w_k_t = wkv_t[:, :MLA_NOPE].reshape(MLA_HEADS * MLA_NOPE, KV_RANK).astype(bf)
        w_v_t = wkv_t[:, MLA_NOPE:].reshape(MLA_HEADS * MLA_V, KV_RANK).astype(bf)

        n_tok_tiles = S // tm_proj
        tok3 = lambda b, s: (b, s, 0)
        feat3 = lambda b, s: (b, 0, s)
        blk4 = lambda b, s: (b, s, 0, 0)
        rope_a_spec = pl.BlockSpec((MLA_ROPE // 2, tm_proj), lambda b, s: (0, s))
        rope_b_spec = pl.BlockSpec((DIFF_D // 2, tm_proj), lambda b, s: (0, s))
        qt, k_a, vt, dqt, dk, dvt = pl.pallas_call(
            _proj_kernel,
            grid=(B, n_tok_tiles),
            in_specs=[
                pl.BlockSpec((1, tm_proj, D_MODEL), tok3),
                _const_spec((1, D_MODEL)),
                _const_spec((IN_COLS, D_MODEL)),
                _const_spec((Q_RANK, 1)),
                _const_spec((MLA_HEADS * MLA_QK, Q_RANK)),
                _const_spec((KV_RANK, 1)),
                _const_spec((MLA_HEADS * MLA_NOPE, KV_RANK)),
                _const_spec((MLA_HEADS * MLA_V, KV_RANK)),
                _const_spec((MLA_QK, 1)), _const_spec((MLA_QK, 1)),
                _const_spec((DIFF_D, 1)), _const_spec((DIFF_D, 1)),
                rope_a_spec, rope_a_spec, rope_b_spec, rope_b_spec,
            ],
            out_specs=[
                pl.BlockSpec((1, MLA_HEADS * LANES, tm_proj), feat3),
                pl.BlockSpec((1, tm_proj, MLA_HEADS * LANES), tok3),
                pl.BlockSpec((1, tm_proj // tk, MLA_HEADS * MLA_V, tk), blk4),
                pl.BlockSpec((1, DIFF_HEADS * 2 * LANES, tm_proj), feat3),
                pl.BlockSpec((1, tm_proj, DIFF_HEADS * LANES), tok3),
                pl.BlockSpec((1, tm_proj // tk, DIFF_HEADS * DIFF_V, tk), blk4),
            ],
            out_shape=[
                jax.ShapeDtypeStruct((B, MLA_HEADS * LANES, S), bf),
                jax.ShapeDtypeStruct((B, S, MLA_HEADS * LANES), bf),
                jax.ShapeDtypeStruct((B, nk, MLA_HEADS * MLA_V, tk), bf),
                jax.ShapeDtypeStruct((B, DIFF_HEADS * 2 * LANES, S), bf),
                jax.ShapeDtypeStruct((B, S, DIFF_HEADS * LANES), bf),
                jax.ShapeDtypeStruct((B, nk, DIFF_HEADS * DIFF_V, tk), bf),
            ],
            compiler_params=pltpu.CompilerParams(
                dimension_semantics=("arbitrary", "arbitrary"), vmem_limit_bytes=VMEM_LIMIT),
            name="proj",
        )(x, attn_norm_g[l].reshape(1, -1), w_in_t, q_a_norm_g[l].reshape(-1, 1), w_q_t,
          kv_a_norm_g[l].reshape(-1, 1), w_k_t, w_v_t,
          mla_q_norm_g[l].reshape(-1, 1), mla_k_norm_g[l].reshape(-1, 1),
          diff_q_norm_g[l].reshape(-1, 1), diff_k_norm_g[l].reshape(-1, 1),
          cos_a, sin_a, cos_b, sin_b)

        nq = S // tq
        n_chains = MLA_HEADS + 2 * DIFF_HEADS
        lam_init = 0.8 - 0.6 * math.exp(-0.3 * l)
        lam_spec = _const_spec((1, DIFF_D))
        q_tile = lambda b, i: (b, 0, i)
        per_batch = lambda b, i: (b, 0, 0, 0)
        o_a, o_b = pl.pallas_call(
            functools.partial(_attn_kernel, tq=tq, tk=tk, lam_init=lam_init),
            grid=(B, nq),
            in_specs=[
                lam_spec, lam_spec, lam_spec, lam_spec,
                _const_spec((DIFF_V, 1)),
                pl.BlockSpec((1, MLA_HEADS * LANES, tq), q_tile),
                pl.BlockSpec((1, nk, tk, MLA_HEADS * LANES), per_batch),
                pl.BlockSpec((1, nk, MLA_HEADS * MLA_V, tk), per_batch),
                pl.BlockSpec((1, DIFF_HEADS * 2 * LANES, tq), q_tile),
                pl.BlockSpec((1, nk, tk, DIFF_HEADS * LANES), per_batch),
                pl.BlockSpec((1, nk, DIFF_HEADS * DIFF_V, tk), per_batch),
            ],
            out_specs=[pl.BlockSpec((1, MLA_HEADS * MLA_V, tq), q_tile),
                       pl.BlockSpec((1, DIFF_HEADS * DIFF_V, tq), q_tile)],
            out_shape=[jax.ShapeDtypeStruct((B, MLA_HEADS * MLA_V, S), bf),
                       jax.ShapeDtypeStruct((B, DIFF_HEADS * DIFF_V, S), bf)],
            scratch_shapes=[
                pltpu.VMEM((2, n_chains, tk, tq), bf),
                pltpu.VMEM((2, n_chains, 1, tq), jnp.float32),
                pltpu.VMEM((n_chains, 1, tq), jnp.float32),
                pltpu.VMEM((MLA_HEADS * (MLA_V + SUM_ROWS)
                            + 2 * DIFF_HEADS * (DIFF_V + SUM_ROWS), tq), jnp.float32)],
            compiler_params=pltpu.CompilerParams(
                dimension_semantics=("arbitrary", "arbitrary"), vmem_limit_bytes=VMEM_LIMIT),
            name="attn",
        )(lambda_q1[l].reshape(1, -1), lambda_k1[l].reshape(1, -1),
          lambda_q2[l].reshape(1, -1), lambda_k2[l].reshape(1, -1),
          diff_subln_g[l].reshape(-1, 1),
          qt, k_a.reshape(B, nk, tk, MLA_HEADS * LANES), vt,
          dqt, dk.reshape(B, nk, tk, DIFF_HEADS * LANES), dvt)

        n_a = MLA_HEADS * MLA_V
        x = pl.pallas_call(
            _ffn_kernel,
            grid=(B, S // tm_ffn),
            in_specs=[
                pl.BlockSpec((1, tm_ffn, D_MODEL), lambda b, s: (b, s, 0)),
                pl.BlockSpec((1, n_a, tm_ffn), lambda b, s: (b, 0, s)),
                pl.BlockSpec((1, D_MODEL - n_a, tm_ffn), lambda b, s: (b, 0, s)),
                _const_spec((D_MODEL, D_MODEL)),
                _const_spec((1, D_MODEL)),
                _const_spec((D_MODEL, D_FF)),
                _const_spec((D_MODEL, D_FF)),
                _const_spec((CONV_WIDTH, D_FF)),
                _const_spec((1, D_FF)),
                _const_spec((D_FF, D_MODEL)),
            ],
            out_specs=pl.BlockSpec((1, tm_ffn, D_MODEL), lambda b, s: (b, s, 0)),
            out_shape=jax.ShapeDtypeStruct((B, S, D_MODEL), x.dtype),
            scratch_shapes=[pltpu.VMEM((8, D_FF), jnp.float32),
                            pltpu.VMEM((tm_ffn, D_FF), bf)],
            compiler_params=pltpu.CompilerParams(
                dimension_semantics=("arbitrary", "arbitrary"), vmem_limit_bytes=VMEM_LIMIT),
            name="ffn",
        )(x, o_a, o_b, w_out[l].astype(bf),
          ffn_norm_g[l].reshape(1, -1), w_gate[l].astype(bf), w_up[l].astype(bf),
          conv_w[l], conv_b[l].reshape(1, -1), w_down[l].astype(bf))
    return x
```

```python
import functools
import math
from typing import Any, NamedTuple

import jax
import jax.numpy as jnp
from jax import lax
from jax.experimental import pallas as pl
from jax.experimental.pallas import tpu as pltpu

D_MODEL = 1024
MLA_HEADS = 8
MLA_NOPE = 64
MLA_ROPE = 32
MLA_V = 64
MLA_QK = MLA_NOPE + MLA_ROPE
Q_RANK = 384
KV_RANK = 256
DIFF_HEADS = 4
DIFF_D = 64
DIFF_V = 2 * DIFF_D
D_FF = 2816
CONV_WIDTH = 3
ROPE_THETA = 10000.0
EPS = 1e-6
LANES = 128
SUM_ROWS = 16
LOG2E = math.log2(math.e)

R_Q = 0
R_KV = R_Q + Q_RANK
R_KPE = R_KV + KV_RANK
R_DQ = R_KPE + MLA_ROPE
R_DK = R_DQ + DIFF_HEADS * 2 * DIFF_D
R_DV = R_DK + DIFF_HEADS * 2 * DIFF_D
IN_COLS = R_DV + DIFF_HEADS * DIFF_V

VMEM_LIMIT = 56 * 1024 * 1024


def _rms_rows(x, g):
    ms = jnp.mean(x * x, axis=-1, keepdims=True)
    return x * lax.rsqrt(ms + EPS) * g


def _rms_cols(xt, n, scale):
    ms = jnp.sum(xt * xt, axis=0, keepdims=True) * (1.0 / n)
    return scale * lax.rsqrt(scale * scale * ms + EPS)


def _rope_cols(xt, cos, sin):
    half = xt.shape[0] // 2
    x1, x2 = xt[:half], xt[half:]
    return x1 * cos - x2 * sin, x2 * cos + x1 * sin


def _proj_kernel(x_ref, g_attn_ref, w_in_ref, g_qa_ref, w_q_ref, g_kva_ref, w_k_ref, w_v_ref,
                 g_q_ref, g_k_ref, g_dq_ref, g_dk_ref,
                 cos_a_ref, sin_a_ref, cos_b_ref, sin_b_ref,
                 qt_ref, k_ref, vt_ref, dqt_ref, dk_ref, dvt_ref):
    tm = x_ref.shape[1]
    bf = jnp.bfloat16
    x = x_ref[0]
    r_tok = lax.rsqrt(jnp.mean(x * x, axis=-1, keepdims=True) + EPS)
    r_tok = jnp.transpose(jnp.broadcast_to(r_tok, (tm, LANES)))[0:1, :]
    h = (x * g_attn_ref[...]).astype(bf)

    def in_proj(r0, r1):
        return lax.dot_general(w_in_ref[r0:r1, :], h, (((1,), (1,)), ((), ())),
                               preferred_element_type=jnp.float32)

    cos_a, sin_a = cos_a_ref[...], sin_a_ref[...]
    cos_b, sin_b = cos_b_ref[...], sin_b_ref[...]
    zeros_pad = jnp.zeros((LANES - MLA_QK, tm), jnp.float32)
    one = jnp.ones((1, tm), jnp.float32)

    def head_a(nope, pe_roped, g):
        nope = nope * _rms_cols(nope, MLA_NOPE, one) * g[:MLA_NOPE]
        return jnp.concatenate([nope, *pe_roped, zeros_pad], axis=0)

    def rope_a(pe, g, scale):
        pe = pe * _rms_cols(pe, MLA_ROPE, scale) * g[MLA_NOPE:]
        return _rope_cols(pe, cos_a, sin_a)

    def head_b(xt, g):
        xt = xt * _rms_cols(xt, DIFF_D, r_tok) * g
        return jnp.concatenate(_rope_cols(xt, cos_b, sin_b), axis=0)

    cq = in_proj(R_Q, R_KV)
    lat = in_proj(R_KV, R_DQ)
    dq = in_proj(R_DQ, R_DK)

    cq = (cq * _rms_cols(cq, Q_RANK, r_tok) * g_qa_ref[...]).astype(bf)
    q = jnp.dot(w_q_ref[...], cq, preferred_element_type=jnp.float32)
    ckv = lat[:KV_RANK]
    ckv = (ckv * _rms_cols(ckv, KV_RANK, r_tok) * g_kva_ref[...]).astype(bf)
    kn = jnp.dot(w_k_ref[...], ckv, preferred_element_type=jnp.float32)
    dk = in_proj(R_DK, R_DV)
    v = jnp.dot(w_v_ref[...], ckv, preferred_element_type=jnp.float32).astype(bf)
    dv = (in_proj(R_DV, IN_COLS) * r_tok).astype(bf)
    tk = vt_ref.shape[3]
    for t in range(tm // tk):
        vt_ref[0, t] = v[:, t * tk:(t + 1) * tk]
        dvt_ref[0, t] = dv[:, t * tk:(t + 1) * tk]

    g_q = g_q_ref[...] * (MLA_QK ** -0.5 * LOG2E)
    for hd in range(MLA_HEADS):
        r0 = hd * MLA_QK
        qt_ref[0, hd * LANES:(hd + 1) * LANES, :] = head_a(
            q[r0:r0 + MLA_NOPE], rope_a(q[r0 + MLA_NOPE:r0 + MLA_QK], g_q, one), g_q).astype(bf)

    g_k = g_k_ref[...]
    kpe = rope_a(lat[KV_RANK:], g_k, r_tok)
    for hd in range(MLA_HEADS):
        kt = head_a(kn[hd * MLA_NOPE:(hd + 1) * MLA_NOPE], kpe, g_k)
        k_ref[0, :, hd * LANES:(hd + 1) * LANES] = kt.T.astype(bf)

    g_dq = g_dq_ref[...] * (DIFF_D ** -0.5 * LOG2E)
    g_dk = g_dk_ref[...]
    zeros_half = jnp.zeros((DIFF_D, tm), bf)
    for hd in range(DIFF_HEADS):
        r0 = hd * 2 * DIFF_D
        q1 = head_b(dq[r0:r0 + DIFF_D], g_dq).astype(bf)
        q2 = head_b(dq[r0 + DIFF_D:r0 + 2 * DIFF_D], g_dq).astype(bf)
        b0 = 2 * hd * LANES
        dqt_ref[0, b0:b0 + LANES, :] = jnp.concatenate([q1, zeros_half], axis=0)
        dqt_ref[0, b0 + LANES:b0 + 2 * LANES, :] = jnp.concatenate([zeros_half, q2], axis=0)
        dkt = jnp.concatenate([head_b(dk[r0:r0 + DIFF_D], g_dk),
                               head_b(dk[r0 + DIFF_D:r0 + 2 * DIFF_D], g_dk)], axis=0)
        dk_ref[0, :, hd * LANES:(hd + 1) * LANES] = dkt.T.astype(bf)


def _causal_mask(s_t, tk, tq):
    key = lax.broadcasted_iota(jnp.int32, (tk, tq), 0)
    qry = lax.broadcasted_iota(jnp.int32, (tk, tq), 1)
    return jnp.where(key <= qry, s_t, -jnp.inf)


class _Chain(NamedTuple):
    k_ref: Any
    k_lanes: slice
    qt_ref: Any
    q_rows: slice
    vt_ref: Any
    v_rows: slice
    acc_rows: slice


def _pipelined_sweep(chains, qi, tq, tk, p_ref, al_ref, m_ref, acc_ref):
    assert tq == 2 * tk
    lower, upper = slice(0, tk), slice(tk, tq)

    def stage_a(j, cols):
        return [jnp.dot(ch.k_ref[0, j, :, ch.k_lanes], ch.qt_ref[0, ch.q_rows, cols],
                        preferred_element_type=jnp.float32) for ch in chains]

    def stage_b(s_all, slot, cols, first=False):
        for c, s_t in enumerate(s_all):
            if first:
                m_new = jnp.max(s_t, axis=0, keepdims=True)
            else:
                m_old = m_ref[c, :, cols]
                m_new = jnp.maximum(m_old, jnp.max(s_t, axis=0, keepdims=True))
                al_ref[slot, c, :, cols] = jnp.exp2(m_old - m_new)
            m_ref[c, :, cols] = m_new
            p_ref[slot, c, :, cols] = jnp.exp2(s_t - m_new).astype(p_ref.dtype)

    def stage_c(j, slot, first=False):
        ones = jnp.ones((SUM_ROWS, tk), p_ref.dtype)
        for c, ch in enumerate(chains):
            vt_ones = jnp.concatenate([ch.vt_ref[0, j, ch.v_rows, :], ones], axis=0)
            pv = jnp.dot(vt_ones, p_ref[slot, c], preferred_element_type=jnp.float32)
            if first:
                acc_ref[ch.acc_rows, :] = pv
            else:
                acc_ref[ch.acc_rows, :] = al_ref[slot, c] * acc_ref[ch.acc_rows, :] + pv

    d0, d1 = 2 * qi, 2 * qi + 1
    s_all = stage_a(d0, slice(0, tq))
    stage_b([jnp.concatenate([_causal_mask(s_t[:, lower], tk, tk), s_t[:, upper]], axis=1)
             for s_t in s_all], 0, slice(0, tq), first=True)

    s_all = stage_a(d1, upper)
    stage_c(d0, 0, first=True)
    stage_b([_causal_mask(s_t, tk, tk) for s_t in s_all], 1, upper)
    p_ref[1, :, :, lower] = jnp.zeros((len(chains), tk, tk), p_ref.dtype)
    al_ref[1, :, :, lower] = jnp.ones((len(chains), 1, tk), al_ref.dtype)

    def two_steps(k, carry):
        s_all = stage_a(2 * k, slice(0, tq))
        stage_c(jnp.where(k == 0, d1, 2 * k - 1), 1)
        stage_b(s_all, 0, slice(0, tq))
        s_all = stage_a(2 * k + 1, slice(0, tq))
        stage_c(2 * k, 0)
        stage_b(s_all, 1, slice(0, tq))
        return carry

    lax.fori_loop(0, qi, two_steps, 0)
    stage_c(jnp.where(qi == 0, d1, 2 * qi - 1), 1)


def _attn_kernel(lq1_ref, lk1_ref, lq2_ref, lk2_ref, g_sub_ref,
                 qt_ref, k_ref, vt_ref, dqt_ref, dk_ref, dvt_ref, oa_ref, ob_ref,
                 p_ref, al_ref, m_ref, acc_ref, *, tq, tk, lam_init):
    rows_a, rows_b = MLA_V + SUM_ROWS, DIFF_V + SUM_ROWS
    base_b = MLA_HEADS * rows_a
    lane_group = lambda i: slice(i * LANES, (i + 1) * LANES)
    chains = [_Chain(k_ref, lane_group(hd), qt_ref, lane_group(hd),
                     vt_ref, slice(hd * MLA_V, (hd + 1) * MLA_V),
                     slice(hd * rows_a, (hd + 1) * rows_a)) for hd in range(MLA_HEADS)]
    chains += [_Chain(dk_ref, lane_group(c // 2), dqt_ref, lane_group(c),
                      dvt_ref, slice((c // 2) * DIFF_V, (c // 2 + 1) * DIFF_V),
                      slice(base_b + c * rows_b, base_b + (c + 1) * rows_b))
               for c in range(2 * DIFF_HEADS)]
    _pipelined_sweep(chains, pl.program_id(1), tq, tk, p_ref, al_ref, m_ref, acc_ref)

    def normalised(ch, dv):
        a0 = ch.acc_rows.start
        return acc_ref[a0:a0 + dv, :] / acc_ref[a0 + dv:a0 + dv + 1, :]

    for hd in range(MLA_HEADS):
        oa_ref[0, hd * MLA_V:(hd + 1) * MLA_V, :] = normalised(chains[hd], MLA_V).astype(oa_ref.dtype)

    lam = (jnp.exp(jnp.sum(lq1_ref[...] * lk1_ref[...], axis=-1, keepdims=True))
           - jnp.exp(jnp.sum(lq2_ref[...] * lk2_ref[...], axis=-1, keepdims=True))
           + lam_init)
    for hd in range(DIFF_HEADS):
        c = MLA_HEADS + 2 * hd
        o = normalised(chains[c], DIFF_V) - lam * normalised(chains[c + 1], DIFF_V)
        ms = jnp.mean(o * o, axis=0, keepdims=True)
        o = o * lax.rsqrt(ms + EPS) * g_sub_ref[...] * (1.0 - lam_init)
        ob_ref[0, hd * DIFF_V:(hd + 1) * DIFF_V, :] = o.astype(ob_ref.dtype)


FF_CHUNK = 1024


def _ffn_kernel(x_ref, oa_ref, ob_ref, w_out_ref, g_ffn_ref, w_gate_ref, w_up_ref,
                conv_w_ref, conv_b_ref, w_down_ref, out_ref, prev_ref, y_ref):
    si = pl.program_id(1)
    tm = x_ref.shape[1]
    contract0 = (((0,), (0,)), ((), ()))
    n_a = oa_ref.shape[1]
    mix = (lax.dot_general(oa_ref[0], w_out_ref[:n_a, :], contract0,
                           preferred_element_type=jnp.float32)
           + lax.dot_general(ob_ref[0], w_out_ref[n_a:, :], contract0,
                             preferred_element_type=jnp.float32))
    x1 = x_ref[0] + mix
    h = _rms_rows(x1, g_ffn_ref[...]).astype(jnp.bfloat16)

    @pl.when(si == 0)
    def _():
        prev_ref[...] = jnp.zeros_like(prev_ref)

    for c0 in range(0, D_FF, FF_CHUNK):
        cw = min(FF_CHUNK, D_FF - c0)
        g = jnp.dot(h, w_gate_ref[:, c0:c0 + cw], preferred_element_type=jnp.float32)
        u = jnp.dot(h, w_up_ref[:, c0:c0 + cw], preferred_element_type=jnp.float32)
        row = lax.broadcasted_iota(jnp.int32, (tm, cw), 0)
        p1 = prev_ref[7:8, c0:c0 + cw]
        p2 = prev_ref[6:7, c0:c0 + cw]
        g1 = jnp.where(row == 0, p1, pltpu.roll(g, 1, axis=0))
        g2 = jnp.where(row == 0, p2, jnp.where(row == 1, p1, pltpu.roll(g, 2, axis=0)))
        prev_ref[:, c0:c0 + cw] = g[tm - 8:tm, :]
        cg = (conv_b_ref[:, c0:c0 + cw] + g2 * conv_w_ref[0:1, c0:c0 + cw]
              + g1 * conv_w_ref[1:2, c0:c0 + cw] + g * conv_w_ref[2:3, c0:c0 + cw])
        y_ref[:, c0:c0 + cw] = (jax.nn.silu(cg) * u).astype(y_ref.dtype)

    out_ref[0] = x1 + jnp.dot(y_ref[...], w_down_ref[...], preferred_element_type=jnp.float32)


def _rope_tables(seq):
    pos = jnp.arange(seq, dtype=jnp.float32)[:, None]

    def tables(dim):
        inv = 1.0 / (ROPE_THETA ** (jnp.arange(0, dim, 2, dtype=jnp.float32) / dim))
        ang = pos * inv[None, :]
        return jnp.cos(ang), jnp.sin(ang)

    ca, sa = tables(MLA_ROPE)
    cb, sb = tables(DIFF_D)
    return ca.T, sa.T, cb.T, sb.T


def _const_spec(shape):
    return pl.BlockSpec(shape, lambda *_: (0,) * len(shape))


def kernel(x, attn_norm_g, w_in, q_a_norm_g, w_q_up, kv_a_norm_g, w_kv_up, mla_q_norm_g,
           mla_k_norm_g, diff_q_norm_g, diff_k_norm_g, lambda_q1, lambda_k1, lambda_q2, lambda_k2,
           diff_subln_g, w_out, ffn_norm_g, w_gate, w_up, conv_w, conv_b, w_down):
    B, S, _ = x.shape
    depth = w_in.shape[0]
    bf = jnp.bfloat16
    cos_a, sin_a, cos_b, sin_b = _rope_tables(S)

    tk = 256
    tq = 2 * tk
    tm_proj = 1024
    tm_ffn = 1024
    assert tm_proj % tk == 0 and S % tq == 0 and S % tm_proj == 0 and S % tm_ffn == 0
    nk = S // tk

    for l in range(depth):
        w_in_t = w_in[l].T.astype(bf)
        w_q_t = w_q_up[l].T.astype(bf)
        wkv_t = w_kv_up[l].T.reshape(MLA_HEADS, MLA_NOPE + MLA_V, KV_RANK)
        w_k_t = wkv_t[:, :MLA_NOPE].reshape(MLA_HEADS * MLA_NOPE, KV_RANK).astype(bf)
        w_v_t = wkv_t[:, MLA_NOPE:].reshape(MLA_HEADS * MLA_V, KV_RANK).astype(bf)

        n_tok_tiles = S // tm_proj
        tok3 = lambda b, s: (b, s, 0)
        feat3 = lambda b, s: (b, 0, s)
        blk4 = lambda b, s: (b, s, 0, 0)
        rope_a_spec = pl.BlockSpec((MLA_ROPE // 2, tm_proj), lambda b, s: (0, s))
        rope_b_spec = pl.BlockSpec((DIFF_D // 2, tm_proj), lambda b, s: (0, s))
        qt, k_a, vt, dqt, dk, dvt = pl.pallas_call(
            _proj_kernel,
            grid=(B, n_tok_tiles),
            in_specs=[
                pl.BlockSpec((1, tm_proj, D_MODEL), tok3),
                _const_spec((1, D_MODEL)),
                _const_spec((IN_COLS, D_MODEL)),
                _const_spec((Q_RANK, 1)),
                _const_spec((MLA_HEADS * MLA_QK, Q_RANK)),
                _const_spec((KV_RANK, 1)),
                _const_spec((MLA_HEADS * MLA_NOPE, KV_RANK)),
                _const_spec((MLA_HEADS * MLA_V, KV_RANK)),
                _const_spec((MLA_QK, 1)), _const_spec((MLA_QK, 1)),
                _const_spec((DIFF_D, 1)), _const_spec((DIFF_D, 1)),
                rope_a_spec, rope_a_spec, rope_b_spec, rope_b_spec,
            ],
            out_specs=[
                pl.BlockSpec((1, MLA_HEADS * LANES, tm_proj), feat3),
                pl.BlockSpec((1, tm_proj, MLA_HEADS * LANES), tok3),
                pl.BlockSpec((1, tm_proj // tk, MLA_HEADS * MLA_V, tk), blk4),
                pl.BlockSpec((1, DIFF_HEADS * 2 * LANES, tm_proj), feat3),
                pl.BlockSpec((1, tm_proj, DIFF_HEADS * LANES), tok3),
                pl.BlockSpec((1, tm_proj // tk, DIFF_HEADS * DIFF_V, tk), blk4),
            ],
            out_shape=[
                jax.ShapeDtypeStruct((B, MLA_HEADS * LANES, S), bf),
                jax.ShapeDtypeStruct((B, S, MLA_HEADS * LANES), bf),
                jax.ShapeDtypeStruct((B, nk, MLA_HEADS * MLA_V, tk), bf),
                jax.ShapeDtypeStruct((B, DIFF_HEADS * 2 * LANES, S), bf),
                jax.ShapeDtypeStruct((B, S, DIFF_HEADS * LANES), bf),
                jax.ShapeDtypeStruct((B, nk, DIFF_HEADS * DIFF_V, tk), bf),
            ],
            compiler_params=pltpu.CompilerParams(
                dimension_semantics=("arbitrary", "arbitrary"), vmem_limit_bytes=VMEM_LIMIT),
            name="proj",
        )(x, attn_norm_g[l].reshape(1, -1), w_in_t, q_a_norm_g[l].reshape(-1, 1), w_q_t,
          kv_a_norm_g[l].reshape(-1, 1), w_k_t, w_v_t,
          mla_q_norm_g[l].reshape(-1, 1), mla_k_norm_g[l].reshape(-1, 1),
          diff_q_norm_g[l].reshape(-1, 1), diff_k_norm_g[l].reshape(-1, 1),
          cos_a, sin_a, cos_b, sin_b)

        nq = S // tq
        n_chains = MLA_HEADS + 2 * DIFF_HEADS
        lam_init = 0.8 - 0.6 * math.exp(-0.3 * l)
        lam_spec = _const_spec((1, DIFF_D))
        q_tile = lambda b, i: (b, 0, i)
        per_batch = lambda b, i: (b, 0, 0, 0)
        o_a, o_b = pl.pallas_call(
            functools.partial(_attn_kernel, tq=tq, tk=tk, lam_init=lam_init),
            grid=(B, nq),
            in_specs=[
                lam_spec, lam_spec, lam_spec, lam_spec,
                _const_spec((DIFF_V, 1)),
                pl.BlockSpec((1, MLA_HEADS * LANES, tq), q_tile),
                pl.BlockSpec((1, nk, tk, MLA_HEADS * LANES), per_batch),
                pl.BlockSpec((1, nk, MLA_HEADS * MLA_V, tk), per_batch),
                pl.BlockSpec((1, DIFF_HEADS * 2 * LANES, tq), q_tile),
                pl.BlockSpec((1, nk, tk, DIFF_HEADS * LANES), per_batch),
                pl.BlockSpec((1, nk, DIFF_HEADS * DIFF_V, tk), per_batch),
            ],
            out_specs=[pl.BlockSpec((1, MLA_HEADS * MLA_V, tq), q_tile),
                       pl.BlockSpec((1, DIFF_HEADS * DIFF_V, tq), q_tile)],
            out_shape=[jax.ShapeDtypeStruct((B, MLA_HEADS * MLA_V, S), bf),
                       jax.ShapeDtypeStruct((B, DIFF_HEADS * DIFF_V, S), bf)],
            scratch_shapes=[
                pltpu.VMEM((2, n_chains, tk, tq), bf),
                pltpu.VMEM((2, n_chains, 1, tq), jnp.float32),
                pltpu.VMEM((n_chains, 1, tq), jnp.float32),
                pltpu.VMEM((MLA_HEADS * (MLA_V + SUM_ROWS)
                            + 2 * DIFF_HEADS * (DIFF_V + SUM_ROWS), tq), jnp.float32)],
            compiler_params=pltpu.CompilerParams(
                dimension_semantics=("arbitrary", "arbitrary"), vmem_limit_bytes=VMEM_LIMIT),
            name="attn",
        )(lambda_q1[l].reshape(1, -1), lambda_k1[l].reshape(1, -1),
          lambda_q2[l].reshape(1, -1), lambda_k2[l].reshape(1, -1),
          diff_subln_g[l].reshape(-1, 1),
          qt, k_a.reshape(B, nk, tk, MLA_HEADS * LANES), vt,
          dqt, dk.reshape(B, nk, tk, DIFF_HEADS * LANES), dvt)

        n_a = MLA_HEADS * MLA_V
        x = pl.pallas_call(
            _ffn_kernel,
            grid=(B, S // tm_ffn),
            in_specs=[
                pl.BlockSpec((1, tm_ffn, D_MODEL), lambda b, s: (b, s, 0)),
                pl.BlockSpec((1, n_a, tm_ffn), lambda b, s: (b, 0, s)),
                pl.BlockSpec((1, D_MODEL - n_a, tm_ffn), lambda b, s: (b, 0, s)),
                _const_spec((D_MODEL, D_MODEL)),
                _const_spec((1, D_MODEL)),
                _const_spec((D_MODEL, D_FF)),
                _const_spec((D_MODEL, D_FF)),
                _const_spec((CONV_WIDTH, D_FF)),
                _const_spec((1, D_FF)),
                _const_spec((D_FF, D_MODEL)),
            ],
            out_specs=pl.BlockSpec((1, tm_ffn, D_MODEL), lambda b, s: (b, s, 0)),
            out_shape=jax.ShapeDtypeStruct((B, S, D_MODEL), x.dtype),
            scratch_shapes=[pltpu.VMEM((8, D_FF), jnp.float32),
                            pltpu.VMEM((tm_ffn, D_FF), bf)],
            compiler_params=pltpu.CompilerParams(
                dimension_semantics=("arbitrary", "arbitrary"), vmem_limit_bytes=VMEM_LIMIT),
            name="ffn",
        )(x, o_a, o_b, w_out[l].astype(bf),
          ffn_norm_g[l].reshape(1, -1), w_gate[l].astype(bf), w_up[l].astype(bf),
          conv_w[l], conv_b[l].reshape(1, -1), w_down[l].astype(bf))
    return x
```

```python
import functools
import math
from typing import Any, NamedTuple

import jax
import jax.numpy as jnp
from jax import lax
from jax.experimental import pallas as pl
from jax.experimental.pallas import tpu as pltpu

D_MODEL = 1024
MLA_HEADS = 8
MLA_NOPE = 64
MLA_ROPE = 32
MLA_V = 64
MLA_QK = MLA_NOPE + MLA_ROPE
Q_RANK = 384
KV_RANK = 256
DIFF_HEADS = 4
DIFF_D = 64
DIFF_V = 2 * DIFF_D
D_FF = 2816
CONV_WIDTH = 3
ROPE_THETA = 10000.0
EPS = 1e-6
LANES = 128
SUM_ROWS = 16
LOG2E = math.log2(math.e)
R_Q = 0
R_KV = R_Q + Q_RANK
R_KPE = R_KV + KV_RANK
R_DQ = R_KPE + MLA_ROPE
R_DK = R_DQ + DIFF_HEADS * 2 * DIFF_D
R_DV = R_DK + DIFF_HEADS * 2 * DIFF_D
IN_COLS = R_DV + DIFF_HEADS * DIFF_V

KEY_TILE = 256
PROJ_TOKENS = 1024
FFN_TOKENS = 1024
FF_CHUNK = 1024
V7X_VMEM_BYTES = 64 * 1024 * 1024
VMEM_LIMIT = V7X_VMEM_BYTES - 8 * 1024 * 1024


def _rms_rows(x, g):
    ms = jnp.mean(x * x, axis=-1, keepdims=True)
    return x * lax.rsqrt(ms + EPS) * g


def _rms_cols(xt, n, scale):
    ms = jnp.sum(xt * xt, axis=0, keepdims=True) * (1.0 / n)
    return scale * lax.rsqrt(scale * scale * ms + EPS)


def _rope_cols(xt, cos, sin):
    half = xt.shape[0] // 2
    x1, x2 = xt[:half], xt[half:]
    return x1 * cos - x2 * sin, x2 * cos + x1 * sin


def _proj_kernel(x_ref, g_attn_ref, w_in_ref, g_qa_ref, w_q_ref, g_kva_ref, w_k_ref, w_v_ref,
                 g_q_ref, g_k_ref, g_dq_ref, g_dk_ref,
                 cos_a_ref, sin_a_ref, cos_b_ref, sin_b_ref,
                 qt_ref, k_ref, vt_ref, dqt_ref, dk_ref, dvt_ref):
    tm = x_ref.shape[1]
    bf = jnp.bfloat16
    x = x_ref[0]
    r_tok = lax.rsqrt(jnp.mean(x * x, axis=-1, keepdims=True) + EPS)
    r_tok = jnp.transpose(jnp.broadcast_to(r_tok, (tm, LANES)))[0:1, :]
    h = (x * g_attn_ref[...]).astype(bf)

    def in_proj(r0, r1):
        return lax.dot_general(w_in_ref[r0:r1, :], h, (((1,), (1,)), ((), ())),
                               preferred_element_type=jnp.float32)

    cos_a, sin_a = cos_a_ref[...], sin_a_ref[...]
    cos_b, sin_b = cos_b_ref[...], sin_b_ref[...]
    zeros_pad = jnp.zeros((LANES - MLA_QK, tm), jnp.float32)
    one = jnp.ones((1, tm), jnp.float32)

    def head_a(nope, pe_roped, g):
        nope = nope * _rms_cols(nope, MLA_NOPE, one) * g[:MLA_NOPE]
        return jnp.concatenate([nope, *pe_roped, zeros_pad], axis=0)

    def rope_a(pe, g, scale):
        pe = pe * _rms_cols(pe, MLA_ROPE, scale) * g[MLA_NOPE:]
        return _rope_cols(pe, cos_a, sin_a)

    def head_b(xt, g):
        xt = xt * _rms_cols(xt, DIFF_D, r_tok) * g
        return jnp.concatenate(_rope_cols(xt, cos_b, sin_b), axis=0)

    cq = in_proj(R_Q, R_KV)
    lat = in_proj(R_KV, R_DQ)
    dq = in_proj(R_DQ, R_DK)

    cq = (cq * _rms_cols(cq, Q_RANK, r_tok) * g_qa_ref[...]).astype(bf)
    q = jnp.dot(w_q_ref[...], cq, preferred_element_type=jnp.float32)
    ckv = lat[:KV_RANK]
    ckv = (ckv * _rms_cols(ckv, KV_RANK, r_tok) * g_kva_ref[...]).astype(bf)
    kn = jnp.dot(w_k_ref[...], ckv, preferred_element_type=jnp.float32)
    dk = in_proj(R_DK, R_DV)
    v = jnp.dot(w_v_ref[...], ckv, preferred_element_type=jnp.float32).astype(bf)
    dv = (in_proj(R_DV, IN_COLS) * r_tok).astype(bf)
    tk = vt_ref.shape[3]
    for t in range(tm // tk):
        vt_ref[0, t] = v[:, t * tk:(t + 1) * tk]
        dvt_ref[0, t] = dv[:, t * tk:(t + 1) * tk]

    g_q = g_q_ref[...] * (MLA_QK ** -0.5 * LOG2E)
    for hd in range(MLA_HEADS):
        r0 = hd * MLA_QK
        qt_ref[0, hd * LANES:(hd + 1) * LANES, :] = head_a(
            q[r0:r0 + MLA_NOPE], rope_a(q[r0 + MLA_NOPE:r0 + MLA_QK], g_q, one), g_q).astype(bf)

    g_k = g_k_ref[...]
    kpe = rope_a(lat[KV_RANK:], g_k, r_tok)
    for hd in range(MLA_HEADS):
        kt = head_a(kn[hd * MLA_NOPE:(hd + 1) * MLA_NOPE], kpe, g_k)
        k_ref[0, :, hd * LANES:(hd + 1) * LANES] = kt.T.astype(bf)

    g_dq = g_dq_ref[...] * (DIFF_D ** -0.5 * LOG2E)
    g_dk = g_dk_ref[...]
    zeros_half = jnp.zeros((DIFF_D, tm), bf)
    for hd in range(DIFF_HEADS):
        r0 = hd * 2 * DIFF_D
        q1 = head_b(dq[r0:r0 + DIFF_D], g_dq).astype(bf)
        q2 = head_b(dq[r0 + DIFF_D:r0 + 2 * DIFF_D], g_dq).astype(bf)
        b0 = 2 * hd * LANES
        dqt_ref[0, b0:b0 + LANES, :] = jnp.concatenate([q1, zeros_half], axis=0)
        dqt_ref[0, b0 + LANES:b0 + 2 * LANES, :] = jnp.concatenate([zeros_half, q2], axis=0)
        dkt = jnp.concatenate([head_b(dk[r0:r0 + DIFF_D], g_dk),
                               head_b(dk[r0 + DIFF_D:r0 + 2 * DIFF_D], g_dk)], axis=0)
        dk_ref[0, :, hd * LANES:(hd + 1) * LANES] = dkt.T.astype(bf)


def _causal_mask(s_t, tk, tq):
    key = lax.broadcasted_iota(jnp.int32, (tk, tq), 0)
    qry = lax.broadcasted_iota(jnp.int32, (tk, tq), 1)
    return jnp.where(key <= qry, s_t, -jnp.inf)


class _Chain(NamedTuple):
    k_ref: Any
    k_lanes: slice
    qt_ref: Any
    q_rows: slice
    vt_ref: Any
    v_rows: slice
    acc_rows: slice


def _pipelined_sweep(chains, qi, tq, tk, p_ref, al_ref, m_ref, acc_ref):
    assert tq == 2 * tk
    lower, upper = slice(0, tk), slice(tk, tq)

    def stage_a(j, cols):
        return [jnp.dot(ch.k_ref[0, j, :, ch.k_lanes], ch.qt_ref[0, ch.q_rows, cols],
                        preferred_element_type=jnp.float32) for ch in chains]

    def stage_b(s_all, slot, cols, first=False):
        for c, s_t in enumerate(s_all):
            if first:
                m_new = jnp.max(s_t, axis=0, keepdims=True)
            else:
                m_old = m_ref[c, :, cols]
                m_new = jnp.maximum(m_old, jnp.max(s_t, axis=0, keepdims=True))
                al_ref[slot, c, :, cols] = jnp.exp2(m_old - m_new)
            m_ref[c, :, cols] = m_new
            p_ref[slot, c, :, cols] = jnp.exp2(s_t - m_new).astype(p_ref.dtype)

    def stage_c(j, slot, first=False):
        ones = jnp.ones((SUM_ROWS, tk), p_ref.dtype)
        for c, ch in enumerate(chains):
            vt_ones = jnp.concatenate([ch.vt_ref[0, j, ch.v_rows, :], ones], axis=0)
            pv = jnp.dot(vt_ones, p_ref[slot, c], preferred_element_type=jnp.float32)
            if first:
                acc_ref[ch.acc_rows, :] = pv
            else:
                acc_ref[ch.acc_rows, :] = al_ref[slot, c] * acc_ref[ch.acc_rows, :] + pv

    d0, d1 = 2 * qi, 2 * qi + 1
    s_all = stage_a(d0, slice(0, tq))
    stage_b([jnp.concatenate([_causal_mask(s_t[:, lower], tk, tk), s_t[:, upper]], axis=1)
             for s_t in s_all], 0, slice(0, tq), first=True)

    s_all = stage_a(d1, upper)
    stage_c(d0, 0, first=True)
    stage_b([_causal_mask(s_t, tk, tk) for s_t in s_all], 1, upper)
    p_ref[1, :, :, lower] = jnp.zeros((len(chains), tk, tk), p_ref.dtype)
    al_ref[1, :, :, lower] = jnp.ones((len(chains), 1, tk), al_ref.dtype)

    def two_steps(k, carry):
        s_all = stage_a(2 * k, slice(0, tq))
        stage_c(jnp.where(k == 0, d1, 2 * k - 1), 1)
        stage_b(s_all, 0, slice(0, tq))
        s_all = stage_a(2 * k + 1, slice(0, tq))
        stage_c(2 * k, 0)
        stage_b(s_all, 1, slice(0, tq))
        return carry

    lax.fori_loop(0, qi, two_steps, 0)
    stage_c(jnp.where(qi == 0, d1, 2 * qi - 1), 1)


def _attn_kernel(lq1_ref, lk1_ref, lq2_ref, lk2_ref, g_sub_ref,
                 qt_ref, k_ref, vt_ref, dqt_ref, dk_ref, dvt_ref, oa_ref, ob_ref,
                 p_ref, al_ref, m_ref, acc_ref, *, tq, tk, lam_init):
    rows_a, rows_b = MLA_V + SUM_ROWS, DIFF_V + SUM_ROWS
    base_b = MLA_HEADS * rows_a
    lane_group = lambda i: slice(i * LANES, (i + 1) * LANES)
    chains = [_Chain(k_ref, lane_group(hd), qt_ref, lane_group(hd),
                     vt_ref, slice(hd * MLA_V, (hd + 1) * MLA_V),
                     slice(hd * rows_a, (hd + 1) * rows_a)) for hd in range(MLA_HEADS)]
    chains += [_Chain(dk_ref, lane_group(c // 2), dqt_ref, lane_group(c),
                      dvt_ref, slice((c // 2) * DIFF_V, (c // 2 + 1) * DIFF_V),
                      slice(base_b + c * rows_b, base_b + (c + 1) * rows_b))
               for c in range(2 * DIFF_HEADS)]
    _pipelined_sweep(chains, pl.program_id(1), tq, tk, p_ref, al_ref, m_ref, acc_ref)

    def normalised(ch, dv):
        a0 = ch.acc_rows.start
        return acc_ref[a0:a0 + dv, :] / acc_ref[a0 + dv:a0 + dv + 1, :]

    for hd in range(MLA_HEADS):
        oa_ref[0, hd * MLA_V:(hd + 1) * MLA_V, :] = normalised(chains[hd], MLA_V).astype(oa_ref.dtype)

    lam = (jnp.exp(jnp.sum(lq1_ref[...] * lk1_ref[...], axis=-1, keepdims=True))
           - jnp.exp(jnp.sum(lq2_ref[...] * lk2_ref[...], axis=-1, keepdims=True))
           + lam_init)
    for hd in range(DIFF_HEADS):
        c = MLA_HEADS + 2 * hd
        o = normalised(chains[c], DIFF_V) - lam * normalised(chains[c + 1], DIFF_V)
        ms = jnp.mean(o * o, axis=0, keepdims=True)
        o = o * lax.rsqrt(ms + EPS) * g_sub_ref[...] * (1.0 - lam_init)
        ob_ref[0, hd * DIFF_V:(hd + 1) * DIFF_V, :] = o.astype(ob_ref.dtype)


def _ffn_kernel(x_ref, oa_ref, ob_ref, w_out_ref, g_ffn_ref, w_gate_ref, w_up_ref,
                conv_w_ref, conv_b_ref, w_down_ref, out_ref, prev_ref, y_ref):
    si = pl.program_id(1)
    tm = x_ref.shape[1]
    contract0 = (((0,), (0,)), ((), ()))
    n_a = oa_ref.shape[1]
    mix = (lax.dot_general(oa_ref[0], w_out_ref[:n_a, :], contract0,
                           preferred_element_type=jnp.float32)
           + lax.dot_general(ob_ref[0], w_out_ref[n_a:, :], contract0,
                             preferred_element_type=jnp.float32))
    x1 = x_ref[0] + mix
    h = _rms_rows(x1, g_ffn_ref[...]).astype(jnp.bfloat16)

    @pl.when(si == 0)
    def _():
        prev_ref[...] = jnp.zeros_like(prev_ref)

    for c0 in range(0, D_FF, FF_CHUNK):
        cw = min(FF_CHUNK, D_FF - c0)
        g = jnp.dot(h, w_gate_ref[:, c0:c0 + cw], preferred_element_type=jnp.float32)
        u = jnp.dot(h, w_up_ref[:, c0:c0 + cw], preferred_element_type=jnp.float32)
        row = lax.broadcasted_iota(jnp.int32, (tm, cw), 0)
        p1 = prev_ref[7:8, c0:c0 + cw]
        p2 = prev_ref[6:7, c0:c0 + cw]
        g1 = jnp.where(row == 0, p1, pltpu.roll(g, 1, axis=0))
        g2 = jnp.where(row == 0, p2, jnp.where(row == 1, p1, pltpu.roll(g, 2, axis=0)))
        prev_ref[:, c0:c0 + cw] = g[tm - 8:tm, :]
        cg = (conv_b_ref[:, c0:c0 + cw] + g2 * conv_w_ref[0:1, c0:c0 + cw]
              + g1 * conv_w_ref[1:2, c0:c0 + cw] + g * conv_w_ref[2:3, c0:c0 + cw])
        y_ref[:, c0:c0 + cw] = (jax.nn.silu(cg) * u).astype(y_ref.dtype)

    out_ref[0] = x1 + jnp.dot(y_ref[...], w_down_ref[...], preferred_element_type=jnp.float32)


def _rope_tables(seq):
    pos = jnp.arange(seq, dtype=jnp.float32)[:, None]

    def tables(dim):
        inv = 1.0 / (ROPE_THETA ** (jnp.arange(0, dim, 2, dtype=jnp.float32) / dim))
        ang = pos * inv[None, :]
        return jnp.cos(ang), jnp.sin(ang)

    ca, sa = tables(MLA_ROPE)
    cb, sb = tables(DIFF_D)
    return ca.T, sa.T, cb.T, sb.T


def _const_spec(shape):
    return pl.BlockSpec(shape, lambda *_: (0,) * len(shape))


def kernel(x, attn_norm_g, w_in, q_a_norm_g, w_q_up, kv_a_norm_g, w_kv_up, mla_q_norm_g,
           mla_k_norm_g, diff_q_norm_g, diff_k_norm_g, lambda_q1, lambda_k1, lambda_q2, lambda_k2,
           diff_subln_g, w_out, ffn_norm_g, w_gate, w_up, conv_w, conv_b, w_down):
    B, S, _ = x.shape
    depth = w_in.shape[0]
    bf = jnp.bfloat16
    cos_a, sin_a, cos_b, sin_b = _rope_tables(S)

    tk, tq = KEY_TILE, 2 * KEY_TILE
    tm_proj, tm_ffn = PROJ_TOKENS, FFN_TOKENS
    assert tm_proj % tk == 0 and S % tq == 0 and S % tm_proj == 0 and S % tm_ffn == 0
    nk = S // tk

    for l in range(depth):
        w_in_t = w_in[l].T.astype(bf)
        w_q_t = w_q_up[l].T.astype(bf)
        wkv_t = w_kv_up[l].T.reshape(MLA_HEADS, MLA_NOPE + MLA_V, KV_RANK)
        w_k_t = wkv_t[:, :MLA_NOPE].reshape(MLA_HEADS * MLA_NOPE, KV_RANK).astype(bf)
        w_v_t = wkv_t[:, MLA_NOPE:].reshape(MLA_HEADS * MLA_V, KV_RANK).astype(bf)

        n_tok_tiles = S // tm_proj
        tok3 = lambda b, s: (b, s, 0)
        feat3 = lambda b, s: (b, 0, s)
        blk4 = lambda b, s: (b, s, 0, 0)
        rope_a_spec = pl.BlockSpec((MLA_ROPE // 2, tm_proj), lambda b, s: (0, s))
        rope_b_spec = pl.BlockSpec((DIFF_D // 2, tm_proj), lambda b, s: (0, s))
        qt, k_a, vt, dqt, dk, dvt = pl.pallas_call(
            _proj_kernel,
            grid=(B, n_tok_tiles),
            in_specs=[
                pl.BlockSpec((1, tm_proj, D_MODEL), tok3),
                _const_spec((1, D_MODEL)),
                _const_spec((IN_COLS, D_MODEL)),
                _const_spec((Q_RANK, 1)),
                _const_spec((MLA_HEADS * MLA_QK, Q_RANK)),
                _const_spec((KV_RANK, 1)),
                _const_spec((MLA_HEADS * MLA_NOPE, KV_RANK)),
                _const_spec((MLA_HEADS * MLA_V, KV_RANK)),
                _const_spec((MLA_QK, 1)), _const_spec((MLA_QK, 1)),
                _const_spec((DIFF_D, 1)), _const_spec((DIFF_D, 1)),
                rope_a_spec, rope_a_spec, rope_b_spec, rope_b_spec,
            ],
            out_specs=[
                pl.BlockSpec((1, MLA_HEADS * LANES, tm_proj), feat3),
                pl.BlockSpec((1, tm_proj, MLA_HEADS * LANES), tok3),
                pl.BlockSpec((1, tm_proj // tk, MLA_HEADS * MLA_V, tk), blk4),
                pl.BlockSpec((1, DIFF_HEADS * 2 * LANES, tm_proj), feat3),
                pl.BlockSpec((1, tm_proj, DIFF_HEADS * LANES), tok3),
                pl.BlockSpec((1, tm_proj // tk, DIFF_HEADS * DIFF_V, tk), blk4),
            ],
            out_shape=[
                jax.ShapeDtypeStruct((B, MLA_HEADS * LANES, S), bf),
                jax.ShapeDtypeStruct((B, S, MLA_HEADS * LANES), bf),
                jax.ShapeDtypeStruct((B, nk, MLA_HEADS * MLA_V, tk), bf),
                jax.ShapeDtypeStruct((B, DIFF_HEADS * 2 * LANES, S), bf),
                jax.ShapeDtypeStruct((B, S, DIFF_HEADS * LANES), bf),
                jax.ShapeDtypeStruct((B, nk, DIFF_HEADS * DIFF_V, tk), bf),
            ],
            compiler_params=pltpu.CompilerParams(
                dimension_semantics=("arbitrary", "arbitrary"), vmem_limit_bytes=VMEM_LIMIT),
            name="proj",
        )(x, attn_norm_g[l].reshape(1, -1), w_in_t, q_a_norm_g[l].reshape(-1, 1), w_q_t,
          kv_a_norm_g[l].reshape(-1, 1), w_k_t, w_v_t,
          mla_q_norm_g[l].reshape(-1, 1), mla_k_norm_g[l].reshape(-1, 1),
          diff_q_norm_g[l].reshape(-1, 1), diff_k_norm_g[l].reshape(-1, 1),
          cos_a, sin_a, cos_b, sin_b)

        nq = S // tq
        n_chains = MLA_HEADS + 2 * DIFF_HEADS
        lam_init = 0.8 - 0.6 * math.exp(-0.3 * l)
        lam_spec = _const_spec((1, DIFF_D))
        q_tile = lambda b, i: (b, 0, i)
        per_batch = lambda b, i: (b, 0, 0, 0)
        o_a, o_b = pl.pallas_call(
            functools.partial(_attn_kernel, tq=tq, tk=tk, lam_init=lam_init),
            grid=(B, nq),
            in_specs=[
                lam_spec, lam_spec, lam_spec, lam_spec,
                _const_spec((DIFF_V, 1)),
                pl.BlockSpec((1, MLA_HEADS * LANES, tq), q_tile),
                pl.BlockSpec((1, nk, tk, MLA_HEADS * LANES), per_batch),
                pl.BlockSpec((1, nk, MLA_HEADS * MLA_V, tk), per_batch),
                pl.BlockSpec((1, DIFF_HEADS * 2 * LANES, tq), q_tile),
                pl.BlockSpec((1, nk, tk, DIFF_HEADS * LANES), per_batch),
                pl.BlockSpec((1, nk, DIFF_HEADS * DIFF_V, tk), per_batch),
            ],
            out_specs=[pl.BlockSpec((1, MLA_HEADS * MLA_V, tq), q_tile),
                       pl.BlockSpec((1, DIFF_HEADS * DIFF_V, tq), q_tile)],
            out_shape=[jax.ShapeDtypeStruct((B, MLA_HEADS * MLA_V, S), bf),
                       jax.ShapeDtypeStruct((B, DIFF_HEADS * DIFF_V, S), bf)],
            scratch_shapes=[
                pltpu.VMEM((2, n_chains, tk, tq), bf),
                pltpu.VMEM((2, n_chains, 1, tq), jnp.float32),
                pltpu.VMEM((n_chains, 1, tq), jnp.float32),
                pltpu.VMEM((MLA_HEADS * (MLA_V + SUM_ROWS)
                            + 2 * DIFF_HEADS * (DIFF_V + SUM_ROWS), tq), jnp.float32)],
            compiler_params=pltpu.CompilerParams(
                dimension_semantics=("arbitrary", "arbitrary"), vmem_limit_bytes=VMEM_LIMIT),
            name="attn",
        )(lambda_q1[l].reshape(1, -1), lambda_k1[l].reshape(1, -1),
          lambda_q2[l].reshape(1, -1), lambda_k2[l].reshape(1, -1),
          diff_subln_g[l].reshape(-1, 1),
          qt, k_a.reshape(B, nk, tk, MLA_HEADS * LANES), vt,
          dqt, dk.reshape(B, nk, tk, DIFF_HEADS * LANES), dvt)

        n_a = MLA_HEADS * MLA_V
        x = pl.pallas_call(
            _ffn_kernel,
            grid=(B, S // tm_ffn),
            in_specs=[
                pl.BlockSpec((1, tm_ffn, D_MODEL), lambda b, s: (b, s, 0)),
                pl.BlockSpec((1, n_a, tm_ffn), lambda b, s: (b, 0, s)),
                pl.BlockSpec((1, D_MODEL - n_a, tm_ffn), lambda b, s: (b, 0, s)),
                _const_spec((D_MODEL, D_MODEL)),
                _const_spec((1, D_MODEL)),
                _const_spec((D_MODEL, D_FF)),
                _const_spec((D_MODEL, D_FF)),
                _const_spec((CONV_WIDTH, D_FF)),
                _const_spec((1, D_FF)),
                _const_spec((D_FF, D_MODEL)),
            ],
            out_specs=pl.BlockSpec((1, tm_ffn, D_MODEL), lambda b, s: (b, s, 0)),
            out_shape=jax.ShapeDtypeStruct((B, S, D_MODEL), x.dtype),
            scratch_shapes=[pltpu.VMEM((8, D_FF), jnp.float32),
                            pltpu.VMEM((tm_ffn, D_FF), bf)],
            compiler_params=pltpu.CompilerParams(
                dimension_semantics=("arbitrary", "arbitrary"), vmem_limit_bytes=VMEM_LIMIT),
            name="ffn",
        )(x, o_a, o_b, w_out[l].astype(bf),
          ffn_norm_g[l].reshape(1, -1), w_gate[l].astype(bf), w_up[l].astype(bf),
          conv_w[l], conv_b[l].reshape(1, -1), w_down[l].astype(bf))
    return x
```

```python
import functools
import math
from typing import Any, NamedTuple

import jax
import jax.numpy as jnp
from jax import lax
from jax.experimental import pallas as pl
from jax.experimental.pallas import tpu as pltpu

D_MODEL = 1024
MLA_HEADS = 8
MLA_NOPE = 64
MLA_ROPE = 32
MLA_V = 64
MLA_QK = MLA_NOPE + MLA_ROPE
Q_RANK = 384
KV_RANK = 256
DIFF_HEADS = 4
DIFF_D = 64
DIFF_V = 2 * DIFF_D
D_FF = 2816
CONV_WIDTH = 3
ROPE_THETA = 10000.0
EPS = 1e-6
LANES = 128
SUBLANES = 8
SUM_ROWS = 16
LOG2E = math.log2(math.e)
R_Q = 0
R_KV = R_Q + Q_RANK
R_KPE = R_KV + KV_RANK
R_DQ = R_KPE + MLA_ROPE
R_DK = R_DQ + DIFF_HEADS * 2 * DIFF_D
R_DV = R_DK + DIFF_HEADS * 2 * DIFF_D
IN_COLS = R_DV + DIFF_HEADS * DIFF_V

KEY_TILE = 256
PROJ_TOKENS = 1024
FFN_TOKENS = 1024
FF_CHUNK = 1024
V7X_VMEM_BYTES = 64 * 1024 * 1024
VMEM_LIMIT = V7X_VMEM_BYTES - 8 * 1024 * 1024


def _rms_rows(x, g):
    ms = jnp.mean(x * x, axis=-1, keepdims=True)
    return x * lax.rsqrt(ms + EPS) * g


def _rms_cols(xt, n, scale):
    ms = jnp.sum(xt * xt, axis=0, keepdims=True) * (1.0 / n)
    return scale * lax.rsqrt(scale * scale * ms + EPS)


def _rope_cols(xt, cos, sin):
    half = xt.shape[0] // 2
    x1, x2 = xt[:half], xt[half:]
    return x1 * cos - x2 * sin, x2 * cos + x1 * sin


def _proj_kernel(x_ref, g_attn_ref, w_in_ref, g_qa_ref, w_q_ref, g_kva_ref, w_k_ref, w_v_ref,
                 g_q_ref, g_k_ref, g_dq_ref, g_dk_ref,
                 cos_a_ref, sin_a_ref, cos_b_ref, sin_b_ref,
                 qt_ref, k_ref, vt_ref, dqt_ref, dk_ref, dvt_ref):
    tm = x_ref.shape[1]
    bf = jnp.bfloat16
    x = x_ref[0]
    r_tok = lax.rsqrt(jnp.mean(x * x, axis=-1, keepdims=True) + EPS)
    r_tok = jnp.transpose(jnp.broadcast_to(r_tok, (tm, LANES)))[0:1, :]
    h = (x * g_attn_ref[...]).astype(bf)

    def in_proj(r0, r1):
        return lax.dot_general(w_in_ref[r0:r1, :], h, (((1,), (1,)), ((), ())),
                               preferred_element_type=jnp.float32)

    cos_a, sin_a = cos_a_ref[...], sin_a_ref[...]
    cos_b, sin_b = cos_b_ref[...], sin_b_ref[...]
    zeros_pad = jnp.zeros((LANES - MLA_QK, tm), jnp.float32)
    one = jnp.ones((1, tm), jnp.float32)

    def head_a(nope, pe_roped, g):
        nope = nope * _rms_cols(nope, MLA_NOPE, one) * g[:MLA_NOPE]
        return jnp.concatenate([nope, *pe_roped, zeros_pad], axis=0)

    def rope_a(pe, g, scale):
        pe = pe * _rms_cols(pe, MLA_ROPE, scale) * g[MLA_NOPE:]
        return _rope_cols(pe, cos_a, sin_a)

    def head_b(xt, g):
        xt = xt * _rms_cols(xt, DIFF_D, r_tok) * g
        return jnp.concatenate(_rope_cols(xt, cos_b, sin_b), axis=0)

    cq = in_proj(R_Q, R_KV)
    lat = in_proj(R_KV, R_DQ)
    dq = in_proj(R_DQ, R_DK)

    cq = (cq * _rms_cols(cq, Q_RANK, r_tok) * g_qa_ref[...]).astype(bf)
    q = jnp.dot(w_q_ref[...], cq, preferred_element_type=jnp.float32)
    ckv = lat[:KV_RANK]
    ckv = (ckv * _rms_cols(ckv, KV_RANK, r_tok) * g_kva_ref[...]).astype(bf)
    kn = jnp.dot(w_k_ref[...], ckv, preferred_element_type=jnp.float32)
    dk = in_proj(R_DK, R_DV)
    v = jnp.dot(w_v_ref[...], ckv, preferred_element_type=jnp.float32).astype(bf)
    dv = (in_proj(R_DV, IN_COLS) * r_tok).astype(bf)
    tk = vt_ref.shape[3]
    for t in range(tm // tk):
        vt_ref[0, t] = v[:, t * tk:(t + 1) * tk]
        dvt_ref[0, t] = dv[:, t * tk:(t + 1) * tk]

    g_q = g_q_ref[...] * (MLA_QK ** -0.5 * LOG2E)
    for hd in range(MLA_HEADS):
        r0 = hd * MLA_QK
        qt_ref[0, hd * LANES:(hd + 1) * LANES, :] = head_a(
            q[r0:r0 + MLA_NOPE], rope_a(q[r0 + MLA_NOPE:r0 + MLA_QK], g_q, one), g_q).astype(bf)

    g_k = g_k_ref[...]
    kpe = rope_a(lat[KV_RANK:], g_k, r_tok)
    for hd in range(MLA_HEADS):
        kt = head_a(kn[hd * MLA_NOPE:(hd + 1) * MLA_NOPE], kpe, g_k)
        k_ref[0, :, hd * LANES:(hd + 1) * LANES] = kt.T.astype(bf)

    g_dq = g_dq_ref[...] * (DIFF_D ** -0.5 * LOG2E)
    g_dk = g_dk_ref[...]
    zeros_half = jnp.zeros((DIFF_D, tm), bf)
    for hd in range(DIFF_HEADS):
        r0 = hd * 2 * DIFF_D
        q1 = head_b(dq[r0:r0 + DIFF_D], g_dq).astype(bf)
        q2 = head_b(dq[r0 + DIFF_D:r0 + 2 * DIFF_D], g_dq).astype(bf)
        b0 = 2 * hd * LANES
        dqt_ref[0, b0:b0 + LANES, :] = jnp.concatenate([q1, zeros_half], axis=0)
        dqt_ref[0, b0 + LANES:b0 + 2 * LANES, :] = jnp.concatenate([zeros_half, q2], axis=0)
        dkt = jnp.concatenate([head_b(dk[r0:r0 + DIFF_D], g_dk),
                               head_b(dk[r0 + DIFF_D:r0 + 2 * DIFF_D], g_dk)], axis=0)
        dk_ref[0, :, hd * LANES:(hd + 1) * LANES] = dkt.T.astype(bf)


def _causal_mask(s_t, tk, tq):
    key = lax.broadcasted_iota(jnp.int32, (tk, tq), 0)
    qry = lax.broadcasted_iota(jnp.int32, (tk, tq), 1)
    return jnp.where(key <= qry, s_t, -jnp.inf)


class _Chain(NamedTuple):
    k_ref: Any
    k_lanes: slice
    qt_ref: Any
    q_rows: slice
    vt_ref: Any
    v_rows: slice
    acc_rows: slice


def _pipelined_sweep(chains, qi, tq, tk, p_ref, al_ref, m_ref, acc_ref):
    assert tq == 2 * tk
    lower, upper = slice(0, tk), slice(tk, tq)

    def stage_a(j, cols):
        return [jnp.dot(ch.k_ref[0, j, :, ch.k_lanes], ch.qt_ref[0, ch.q_rows, cols],
                        preferred_element_type=jnp.float32) for ch in chains]

    def stage_b(s_all, slot, cols, first=False):
        for c, s_t in enumerate(s_all):
            if first:
                m_new = jnp.max(s_t, axis=0, keepdims=True)
            else:
                m_old = m_ref[c, :, cols]
                m_new = jnp.maximum(m_old, jnp.max(s_t, axis=0, keepdims=True))
                al_ref[slot, c, :, cols] = jnp.exp2(m_old - m_new)
            m_ref[c, :, cols] = m_new
            p_ref[slot, c, :, cols] = jnp.exp2(s_t - m_new).astype(p_ref.dtype)

    def stage_c(j, slot, first=False):
        ones = jnp.ones((SUM_ROWS, tk), p_ref.dtype)
        for c, ch in enumerate(chains):
            vt_ones = jnp.concatenate([ch.vt_ref[0, j, ch.v_rows, :], ones], axis=0)
            pv = jnp.dot(vt_ones, p_ref[slot, c], preferred_element_type=jnp.float32)
            if first:
                acc_ref[ch.acc_rows, :] = pv
            else:
                acc_ref[ch.acc_rows, :] = al_ref[slot, c] * acc_ref[ch.acc_rows, :] + pv

    d0, d1 = 2 * qi, 2 * qi + 1
    s_all = stage_a(d0, slice(0, tq))
    stage_b([jnp.concatenate([_causal_mask(s_t[:, lower], tk, tk), s_t[:, upper]], axis=1)
             for s_t in s_all], 0, slice(0, tq), first=True)

    s_all = stage_a(d1, upper)
    stage_c(d0, 0, first=True)
    stage_b([_causal_mask(s_t, tk, tk) for s_t in s_all], 1, upper)
    p_ref[1, :, :, lower] = jnp.zeros((len(chains), tk, tk), p_ref.dtype)
    al_ref[1, :, :, lower] = jnp.ones((len(chains), 1, tk), al_ref.dtype)

    def two_steps(k, carry):
        s_all = stage_a(2 * k, slice(0, tq))
        stage_c(jnp.where(k == 0, d1, 2 * k - 1), 1)
        stage_b(s_all, 0, slice(0, tq))
        s_all = stage_a(2 * k + 1, slice(0, tq))
        stage_c(2 * k, 0)
        stage_b(s_all, 1, slice(0, tq))
        return carry

    lax.fori_loop(0, qi, two_steps, 0)
    stage_c(jnp.where(qi == 0, d1, 2 * qi - 1), 1)


def _attn_kernel(lq1_ref, lk1_ref, lq2_ref, lk2_ref, g_sub_ref,
                 qt_ref, k_ref, vt_ref, dqt_ref, dk_ref, dvt_ref, oa_ref, ob_ref,
                 p_ref, al_ref, m_ref, acc_ref, *, tq, tk, lam_init):
    rows_a, rows_b = MLA_V + SUM_ROWS, DIFF_V + SUM_ROWS
    base_b = MLA_HEADS * rows_a
    lane_group = lambda i: slice(i * LANES, (i + 1) * LANES)
    chains = [_Chain(k_ref, lane_group(hd), qt_ref, lane_group(hd),
                     vt_ref, slice(hd * MLA_V, (hd + 1) * MLA_V),
                     slice(hd * rows_a, (hd + 1) * rows_a)) for hd in range(MLA_HEADS)]
    chains += [_Chain(dk_ref, lane_group(c // 2), dqt_ref, lane_group(c),
                      dvt_ref, slice((c // 2) * DIFF_V, (c // 2 + 1) * DIFF_V),
                      slice(base_b + c * rows_b, base_b + (c + 1) * rows_b))
               for c in range(2 * DIFF_HEADS)]
    _pipelined_sweep(chains, pl.program_id(1), tq, tk, p_ref, al_ref, m_ref, acc_ref)

    def normalised(ch, dv):
        a0 = ch.acc_rows.start
        return acc_ref[a0:a0 + dv, :] / acc_ref[a0 + dv:a0 + dv + 1, :]

    for hd in range(MLA_HEADS):
        oa_ref[0, hd * MLA_V:(hd + 1) * MLA_V, :] = normalised(chains[hd], MLA_V).astype(oa_ref.dtype)

    lam = (jnp.exp(jnp.sum(lq1_ref[...] * lk1_ref[...], axis=-1, keepdims=True))
           - jnp.exp(jnp.sum(lq2_ref[...] * lk2_ref[...], axis=-1, keepdims=True))
           + lam_init)
    for hd in range(DIFF_HEADS):
        c = MLA_HEADS + 2 * hd
        o = normalised(chains[c], DIFF_V) - lam * normalised(chains[c + 1], DIFF_V)
        ms = jnp.mean(o * o, axis=0, keepdims=True)
        o = o * lax.rsqrt(ms + EPS) * g_sub_ref[...] * (1.0 - lam_init)
        ob_ref[0, hd * DIFF_V:(hd + 1) * DIFF_V, :] = o.astype(ob_ref.dtype)


def _ffn_kernel(x_ref, oa_ref, ob_ref, w_out_ref, g_ffn_ref, w_gate_ref, w_up_ref,
                conv_w_ref, conv_b_ref, w_down_ref, out_ref, prev_ref, y_ref):
    assert CONV_WIDTH == 3
    si = pl.program_id(1)
    tm = x_ref.shape[1]
    contract0 = (((0,), (0,)), ((), ()))
    n_a = oa_ref.shape[1]
    mix = (lax.dot_general(oa_ref[0], w_out_ref[:n_a, :], contract0,
                           preferred_element_type=jnp.float32)
           + lax.dot_general(ob_ref[0], w_out_ref[n_a:, :], contract0,
                             preferred_element_type=jnp.float32))
    x1 = x_ref[0] + mix
    h = _rms_rows(x1, g_ffn_ref[...]).astype(jnp.bfloat16)

    @pl.when(si == 0)
    def _():
        prev_ref[...] = jnp.zeros_like(prev_ref)

    for c0 in range(0, D_FF, FF_CHUNK):
        cw = min(FF_CHUNK, D_FF - c0)
        g = jnp.dot(h, w_gate_ref[:, c0:c0 + cw], preferred_element_type=jnp.float32)
        u = jnp.dot(h, w_up_ref[:, c0:c0 + cw], preferred_element_type=jnp.float32)
        row = lax.broadcasted_iota(jnp.int32, (tm, cw), 0)
        p1 = prev_ref[SUBLANES - 1:SUBLANES, c0:c0 + cw]
        p2 = prev_ref[SUBLANES - 2:SUBLANES - 1, c0:c0 + cw]
        g1 = jnp.where(row == 0, p1, pltpu.roll(g, 1, axis=0))
        g2 = jnp.where(row == 0, p2, jnp.where(row == 1, p1, pltpu.roll(g, 2, axis=0)))
        prev_ref[:, c0:c0 + cw] = g[tm - SUBLANES:tm, :]
        cg = (conv_b_ref[:, c0:c0 + cw] + g2 * conv_w_ref[0:1, c0:c0 + cw]
              + g1 * conv_w_ref[1:2, c0:c0 + cw] + g * conv_w_ref[2:3, c0:c0 + cw])
        y_ref[:, c0:c0 + cw] = (jax.nn.silu(cg) * u).astype(y_ref.dtype)

    out_ref[0] = x1 + jnp.dot(y_ref[...], w_down_ref[...], preferred_element_type=jnp.float32)


def _rope_tables(seq):
    pos = jnp.arange(seq, dtype=jnp.float32)[:, None]

    def tables(dim):
        inv = 1.0 / (ROPE_THETA ** (jnp.arange(0, dim, 2, dtype=jnp.float32) / dim))
        ang = pos * inv[None, :]
        return jnp.cos(ang), jnp.sin(ang)

    ca, sa = tables(MLA_ROPE)
    cb, sb = tables(DIFF_D)
    return ca.T, sa.T, cb.T, sb.T


def _const_spec(shape):
    return pl.BlockSpec(shape, lambda *_: (0,) * len(shape))


def kernel(x, attn_norm_g, w_in, q_a_norm_g, w_q_up, kv_a_norm_g, w_kv_up, mla_q_norm_g,
           mla_k_norm_g, diff_q_norm_g, diff_k_norm_g, lambda_q1, lambda_k1, lambda_q2, lambda_k2,
           diff_subln_g, w_out, ffn_norm_g, w_gate, w_up, conv_w, conv_b, w_down):
    B, S, _ = x.shape
    depth = w_in.shape[0]
    bf = jnp.bfloat16
    cos_a, sin_a, cos_b, sin_b = _rope_tables(S)

    tk, tq = KEY_TILE, 2 * KEY_TILE
    tm_proj, tm_ffn = PROJ_TOKENS, FFN_TOKENS
    assert tm_proj % tk == 0 and S % tq == 0 and S % tm_proj == 0 and S % tm_ffn == 0
    nk = S // tk

    for l in range(depth):
        w_in_t = w_in[l].T.astype(bf)
        w_q_t = w_q_up[l].T.astype(bf)
        wkv_t = w_kv_up[l].T.reshape(MLA_HEADS, MLA_NOPE + MLA_V, KV_RANK)
        w_k_t = wkv_t[:, :MLA_NOPE].reshape(MLA_HEADS * MLA_NOPE, KV_RANK).astype(bf)
        w_v_t = wkv_t[:, MLA_NOPE:].reshape(MLA_HEADS * MLA_V, KV_RANK).astype(bf)

        n_tok_tiles = S // tm_proj
        tok3 = lambda b, s: (b, s, 0)
        feat3 = lambda b, s: (b, 0, s)
        blk4 = lambda b, s: (b, s, 0, 0)
        rope_a_spec = pl.BlockSpec((MLA_ROPE // 2, tm_proj), lambda b, s: (0, s))
        rope_b_spec = pl.BlockSpec((DIFF_D // 2, tm_proj), lambda b, s: (0, s))
        qt, k_a, vt, dqt, dk, dvt = pl.pallas_call(
            _proj_kernel,
            grid=(B, n_tok_tiles),
            in_specs=[
                pl.BlockSpec((1, tm_proj, D_MODEL), tok3),
                _const_spec((1, D_MODEL)),
                _const_spec((IN_COLS, D_MODEL)),
                _const_spec((Q_RANK, 1)),
                _const_spec((MLA_HEADS * MLA_QK, Q_RANK)),
                _const_spec((KV_RANK, 1)),
                _const_spec((MLA_HEADS * MLA_NOPE, KV_RANK)),
                _const_spec((MLA_HEADS * MLA_V, KV_RANK)),
                _const_spec((MLA_QK, 1)), _const_spec((MLA_QK, 1)),
                _const_spec((DIFF_D, 1)), _const_spec((DIFF_D, 1)),
                rope_a_spec, rope_a_spec, rope_b_spec, rope_b_spec,
            ],
            out_specs=[
                pl.BlockSpec((1, MLA_HEADS * LANES, tm_proj), feat3),
                pl.BlockSpec((1, tm_proj, MLA_HEADS * LANES), tok3),
                pl.BlockSpec((1, tm_proj // tk, MLA_HEADS * MLA_V, tk), blk4),
                pl.BlockSpec((1, DIFF_HEADS * 2 * LANES, tm_proj), feat3),
                pl.BlockSpec((1, tm_proj, DIFF_HEADS * LANES), tok3),
                pl.BlockSpec((1, tm_proj // tk, DIFF_HEADS * DIFF_V, tk), blk4),
            ],
            out_shape=[
                jax.ShapeDtypeStruct((B, MLA_HEADS * LANES, S), bf),
                jax.ShapeDtypeStruct((B, S, MLA_HEADS * LANES), bf),
                jax.ShapeDtypeStruct((B, nk, MLA_HEADS * MLA_V, tk), bf),
                jax.ShapeDtypeStruct((B, DIFF_HEADS * 2 * LANES, S), bf),
                jax.ShapeDtypeStruct((B, S, DIFF_HEADS * LANES), bf),
                jax.ShapeDtypeStruct((B, nk, DIFF_HEADS * DIFF_V, tk), bf),
            ],
            compiler_params=pltpu.CompilerParams(
                dimension_semantics=("arbitrary", "arbitrary"), vmem_limit_bytes=VMEM_LIMIT),
            name="proj",
        )(x, attn_norm_g[l].reshape(1, -1), w_in_t, q_a_norm_g[l].reshape(-1, 1), w_q_t,
          kv_a_norm_g[l].reshape(-1, 1), w_k_t, w_v_t,
          mla_q_norm_g[l].reshape(-1, 1), mla_k_norm_g[l].reshape(-1, 1),
          diff_q_norm_g[l].reshape(-1, 1), diff_k_norm_g[l].reshape(-1, 1),
          cos_a, sin_a, cos_b, sin_b)

        nq = S // tq
        n_chains = MLA_HEADS + 2 * DIFF_HEADS
        lam_init = 0.8 - 0.6 * math.exp(-0.3 * l)
        lam_spec = _const_spec((1, DIFF_D))
        q_tile = lambda b, i: (b, 0, i)
        per_batch = lambda b, i: (b, 0, 0, 0)
        o_a, o_b = pl.pallas_call(
            functools.partial(_attn_kernel, tq=tq, tk=tk, lam_init=lam_init),
            grid=(B, nq),
            in_specs=[
                lam_spec, lam_spec, lam_spec, lam_spec,
                _const_spec((DIFF_V, 1)),
                pl.BlockSpec((1, MLA_HEADS * LANES, tq), q_tile),
                pl.BlockSpec((1, nk, tk, MLA_HEADS * LANES), per_batch),
                pl.BlockSpec((1, nk, MLA_HEADS * MLA_V, tk), per_batch),
                pl.BlockSpec((1, DIFF_HEADS * 2 * LANES, tq), q_tile),
                pl.BlockSpec((1, nk, tk, DIFF_HEADS * LANES), per_batch),
                pl.BlockSpec((1, nk, DIFF_HEADS * DIFF_V, tk), per_batch),
            ],
            out_specs=[pl.BlockSpec((1, MLA_HEADS * MLA_V, tq), q_tile),
                       pl.BlockSpec((1, DIFF_HEADS * DIFF_V, tq), q_tile)],
            out_shape=[jax.ShapeDtypeStruct((B, MLA_HEADS * MLA_V, S), bf),
                       jax.ShapeDtypeStruct((B, DIFF_HEADS * DIFF_V, S), bf)],
            scratch_shapes=[
                pltpu.VMEM((2, n_chains, tk, tq), bf),
                pltpu.VMEM((2, n_chains, 1, tq), jnp.float32),
                pltpu.VMEM((n_chains, 1, tq), jnp.float32),
                pltpu.VMEM((MLA_HEADS * (MLA_V + SUM_ROWS)
                            + 2 * DIFF_HEADS * (DIFF_V + SUM_ROWS), tq), jnp.float32)],
            compiler_params=pltpu.CompilerParams(
                dimension_semantics=("arbitrary", "arbitrary"), vmem_limit_bytes=VMEM_LIMIT),
            name="attn",
        )(lambda_q1[l].reshape(1, -1), lambda_k1[l].reshape(1, -1),
          lambda_q2[l].reshape(1, -1), lambda_k2[l].reshape(1, -1),
          diff_subln_g[l].reshape(-1, 1),
          qt, k_a.reshape(B, nk, tk, MLA_HEADS * LANES), vt,
          dqt, dk.reshape(B, nk, tk, DIFF_HEADS * LANES), dvt)

        n_a = MLA_HEADS * MLA_V
        x = pl.pallas_call(
            _ffn_kernel,
            grid=(B, S // tm_ffn),
            in_specs=[
                pl.BlockSpec((1, tm_ffn, D_MODEL), lambda b, s: (b, s, 0)),
                pl.BlockSpec((1, n_a, tm_ffn), lambda b, s: (b, 0, s)),
                pl.BlockSpec((1, D_MODEL - n_a, tm_ffn), lambda b, s: (b, 0, s)),
                _const_spec((D_MODEL, D_MODEL)),
                _const_spec((1, D_MODEL)),
                _const_spec((D_MODEL, D_FF)),
                _const_spec((D_MODEL, D_FF)),
                _const_spec((CONV_WIDTH, D_FF)),
                _const_spec((1, D_FF)),
                _const_spec((D_FF, D_MODEL)),
            ],
            out_specs=pl.BlockSpec((1, tm_ffn, D_MODEL), lambda b, s: (b, s, 0)),
            out_shape=jax.ShapeDtypeStruct((B, S, D_MODEL), x.dtype),
            scratch_shapes=[pltpu.VMEM((SUBLANES, D_FF), jnp.float32),
                            pltpu.VMEM((tm_ffn, D_FF), bf)],
            compiler_params=pltpu.CompilerParams(
                dimension_semantics=("arbitrary", "arbitrary"), vmem_limit_bytes=VMEM_LIMIT),
            name="ffn",
        )(x, o_a, o_b, w_out[l].astype(bf),
          ffn_norm_g[l].reshape(1, -1), w_gate[l].astype(bf), w_up[l].astype(bf),
          conv_w[l], conv_b[l].reshape(1, -1), w_down[l].astype(bf))
    return x
```

```python
import functools
import math
from typing import Any, NamedTuple

import jax
import jax.numpy as jnp
from jax import lax
from jax.experimental import pallas as pl
from jax.experimental.pallas import tpu as pltpu

D_MODEL = 1024
MLA_HEADS = 8
MLA_NOPE = 64
MLA_ROPE = 32
MLA_V = 64
MLA_QK = MLA_NOPE + MLA_ROPE
Q_RANK = 384
KV_RANK = 256
DIFF_HEADS = 4
DIFF_D = 64
DIFF_V = 2 * DIFF_D
D_FF = 2816
CONV_WIDTH = 3
ROPE_THETA = 10000.0
EPS = 1e-6
LANES = 128
SUBLANES = 8
SUM_ROWS = 16
LOG2E = math.log2(math.e)
R_Q = 0
R_KV = R_Q + Q_RANK
R_KPE = R_KV + KV_RANK
R_DQ = R_KPE + MLA_ROPE
R_DK = R_DQ + DIFF_HEADS * 2 * DIFF_D
R_DV = R_DK + DIFF_HEADS * 2 * DIFF_D
IN_COLS = R_DV + DIFF_HEADS * DIFF_V

KEY_TILE = 256
PROJ_TOKENS = 1024
FFN_TOKENS = 1024
FF_CHUNK = 1024
V7X_VMEM_BYTES = 64 * 1024 * 1024
VMEM_LIMIT = V7X_VMEM_BYTES - 8 * 1024 * 1024


def _rms_rows(x, g):
    ms = jnp.mean(x * x, axis=-1, keepdims=True)
    return x * lax.rsqrt(ms + EPS) * g


def _rms_cols(xt, n, scale):
    ms = jnp.sum(xt * xt, axis=0, keepdims=True) * (1.0 / n)
    return scale * lax.rsqrt(scale * scale * ms + EPS)


def _rope_cols(xt, cos, sin):
    half = xt.shape[0] // 2
    x1, x2 = xt[:half], xt[half:]
    return x1 * cos - x2 * sin, x2 * cos + x1 * sin


def _proj_kernel(x_ref, g_attn_ref, w_in_ref, g_qa_ref, w_q_ref, g_kva_ref, w_k_ref, w_v_ref,
                 g_q_ref, g_k_ref, g_dq_ref, g_dk_ref,
                 cos_a_ref, sin_a_ref, cos_b_ref, sin_b_ref,
                 w_gate_ref, w_up_ref, w_down_ref, w_out_ref,
                 qt_ref, k_ref, vt_ref, dqt_ref, dk_ref, dvt_ref,
                 w_gate_bf_ref, w_up_bf_ref, w_down_bf_ref, w_out_bf_ref):
    tm = x_ref.shape[1]
    bf = jnp.bfloat16
    for src_ref, dst_ref in ((w_gate_ref, w_gate_bf_ref), (w_up_ref, w_up_bf_ref),
                             (w_down_ref, w_down_bf_ref), (w_out_ref, w_out_bf_ref)):
        dst_ref[...] = src_ref[...].astype(bf)
    x = x_ref[0]
    r_tok = lax.rsqrt(jnp.mean(x * x, axis=-1, keepdims=True) + EPS)
    r_tok = jnp.transpose(jnp.broadcast_to(r_tok, (tm, LANES)))[0:1, :]
    h = (x * g_attn_ref[...]).astype(bf)

    def in_proj(r0, r1):
        return lax.dot_general(w_in_ref[r0:r1, :], h, (((1,), (1,)), ((), ())),
                               preferred_element_type=jnp.float32)

    cos_a, sin_a = cos_a_ref[...], sin_a_ref[...]
    cos_b, sin_b = cos_b_ref[...], sin_b_ref[...]
    zeros_pad = jnp.zeros((LANES - MLA_QK, tm), jnp.float32)
    one = jnp.ones((1, tm), jnp.float32)

    def head_a(nope, pe_roped, g):
        nope = nope * _rms_cols(nope, MLA_NOPE, one) * g[:MLA_NOPE]
        return jnp.concatenate([nope, *pe_roped, zeros_pad], axis=0)

    def rope_a(pe, g, scale):
        pe = pe * _rms_cols(pe, MLA_ROPE, scale) * g[MLA_NOPE:]
        return _rope_cols(pe, cos_a, sin_a)

    def head_b(xt, g):
        xt = xt * _rms_cols(xt, DIFF_D, r_tok) * g
        return jnp.concatenate(_rope_cols(xt, cos_b, sin_b), axis=0)

    cq = in_proj(R_Q, R_KV)
    lat = in_proj(R_KV, R_DQ)
    dq = in_proj(R_DQ, R_DK)

    cq = (cq * _rms_cols(cq, Q_RANK, r_tok) * g_qa_ref[...]).astype(bf)
    q = jnp.dot(w_q_ref[...], cq, preferred_element_type=jnp.float32)
    ckv = lat[:KV_RANK]
    ckv = (ckv * _rms_cols(ckv, KV_RANK, r_tok) * g_kva_ref[...]).astype(bf)
    kn = jnp.dot(w_k_ref[...], ckv, preferred_element_type=jnp.float32)
    dk = in_proj(R_DK, R_DV)
    v = jnp.dot(w_v_ref[...], ckv, preferred_element_type=jnp.float32).astype(bf)
    dv = (in_proj(R_DV, IN_COLS) * r_tok).astype(bf)
    tk = vt_ref.shape[3]
    for t in range(tm // tk):
        vt_ref[0, t] = v[:, t * tk:(t + 1) * tk]
        dvt_ref[0, t] = dv[:, t * tk:(t + 1) * tk]

    g_q = g_q_ref[...] * (MLA_QK ** -0.5 * LOG2E)
    for hd in range(MLA_HEADS):
        r0 = hd * MLA_QK
        qt_ref[0, hd * LANES:(hd + 1) * LANES, :] = head_a(
            q[r0:r0 + MLA_NOPE], rope_a(q[r0 + MLA_NOPE:r0 + MLA_QK], g_q, one), g_q).astype(bf)

    g_k = g_k_ref[...]
    kpe = rope_a(lat[KV_RANK:], g_k, r_tok)
    for hd in range(MLA_HEADS):
        kt = head_a(kn[hd * MLA_NOPE:(hd + 1) * MLA_NOPE], kpe, g_k)
        k_ref[0, :, hd * LANES:(hd + 1) * LANES] = kt.T.astype(bf)

    g_dq = g_dq_ref[...] * (DIFF_D ** -0.5 * LOG2E)
    g_dk = g_dk_ref[...]
    zeros_half = jnp.zeros((DIFF_D, tm), bf)
    for hd in range(DIFF_HEADS):
        r0 = hd * 2 * DIFF_D
        q1 = head_b(dq[r0:r0 + DIFF_D], g_dq).astype(bf)
        q2 = head_b(dq[r0 + DIFF_D:r0 + 2 * DIFF_D], g_dq).astype(bf)
        b0 = 2 * hd * LANES
        dqt_ref[0, b0:b0 + LANES, :] = jnp.concatenate([q1, zeros_half], axis=0)
        dqt_ref[0, b0 + LANES:b0 + 2 * LANES, :] = jnp.concatenate([zeros_half, q2], axis=0)
        dkt = jnp.concatenate([head_b(dk[r0:r0 + DIFF_D], g_dk),
                               head_b(dk[r0 + DIFF_D:r0 + 2 * DIFF_D], g_dk)], axis=0)
        dk_ref[0, :, hd * LANES:(hd + 1) * LANES] = dkt.T.astype(bf)


def _causal_mask(s_t, tk, tq):
    key = lax.broadcasted_iota(jnp.int32, (tk, tq), 0)
    qry = lax.broadcasted_iota(jnp.int32, (tk, tq), 1)
    return jnp.where(key <= qry, s_t, -jnp.inf)


class _Chain(NamedTuple):
    k_ref: Any
    k_lanes: slice
    qt_ref: Any
    q_rows: slice
    vt_ref: Any
    v_rows: slice
    acc_rows: slice


def _pipelined_sweep(chains, qi, tq, tk, p_ref, al_ref, m_ref, acc_ref):
    assert tq == 2 * tk
    lower, upper = slice(0, tk), slice(tk, tq)

    def stage_a(j, cols):
        return [jnp.dot(ch.k_ref[0, j, :, ch.k_lanes], ch.qt_ref[0, ch.q_rows, cols],
                        preferred_element_type=jnp.float32) for ch in chains]

    def stage_b(s_all, slot, cols, first=False):
        for c, s_t in enumerate(s_all):
            if first:
                m_new = jnp.max(s_t, axis=0, keepdims=True)
            else:
                m_old = m_ref[c, :, cols]
                m_new = jnp.maximum(m_old, jnp.max(s_t, axis=0, keepdims=True))
                al_ref[slot, c, :, cols] = jnp.exp2(m_old - m_new)
            m_ref[c, :, cols] = m_new
            p_ref[slot, c, :, cols] = jnp.exp2(s_t - m_new).astype(p_ref.dtype)

    def stage_c(j, slot, first=False):
        ones = jnp.ones((SUM_ROWS, tk), p_ref.dtype)
        for c, ch in enumerate(chains):
            vt_ones = jnp.concatenate([ch.vt_ref[0, j, ch.v_rows, :], ones], axis=0)
            pv = jnp.dot(vt_ones, p_ref[slot, c], preferred_element_type=jnp.float32)
            if first:
                acc_ref[ch.acc_rows, :] = pv
            else:
                acc_ref[ch.acc_rows, :] = al_ref[slot, c] * acc_ref[ch.acc_rows, :] + pv

    d0, d1 = 2 * qi, 2 * qi + 1
    s_all = stage_a(d0, slice(0, tq))
    stage_b([jnp.concatenate([_causal_mask(s_t[:, lower], tk, tk), s_t[:, upper]], axis=1)
             for s_t in s_all], 0, slice(0, tq), first=True)

    s_all = stage_a(d1, upper)
    stage_c(d0, 0, first=True)
    stage_b([_causal_mask(s_t, tk, tk) for s_t in s_all], 1, upper)
    p_ref[1, :, :, lower] = jnp.zeros((len(chains), tk, tk), p_ref.dtype)
    al_ref[1, :, :, lower] = jnp.ones((len(chains), 1, tk), al_ref.dtype)

    def two_steps(k, carry):
        s_all = stage_a(2 * k, slice(0, tq))
        stage_c(jnp.where(k == 0, d1, 2 * k - 1), 1)
        stage_b(s_all, 0, slice(0, tq))
        s_all = stage_a(2 * k + 1, slice(0, tq))
        stage_c(2 * k, 0)
        stage_b(s_all, 1, slice(0, tq))
        return carry

    lax.fori_loop(0, qi, two_steps, 0)
    stage_c(jnp.where(qi == 0, d1, 2 * qi - 1), 1)


def _attn_kernel(lq1_ref, lk1_ref, lq2_ref, lk2_ref, g_sub_ref,
                 qt_ref, k_ref, vt_ref, dqt_ref, dk_ref, dvt_ref, oa_ref, ob_ref,
                 p_ref, al_ref, m_ref, acc_ref, *, tq, tk, lam_init):
    rows_a, rows_b = MLA_V + SUM_ROWS, DIFF_V + SUM_ROWS
    base_b = MLA_HEADS * rows_a
    lane_group = lambda i: slice(i * LANES, (i + 1) * LANES)
    chains = [_Chain(k_ref, lane_group(hd), qt_ref, lane_group(hd),
                     vt_ref, slice(hd * MLA_V, (hd + 1) * MLA_V),
                     slice(hd * rows_a, (hd + 1) * rows_a)) for hd in range(MLA_HEADS)]
    chains += [_Chain(dk_ref, lane_group(c // 2), dqt_ref, lane_group(c),
                      dvt_ref, slice((c // 2) * DIFF_V, (c // 2 + 1) * DIFF_V),
                      slice(base_b + c * rows_b, base_b + (c + 1) * rows_b))
               for c in range(2 * DIFF_HEADS)]
    _pipelined_sweep(chains, pl.program_id(1), tq, tk, p_ref, al_ref, m_ref, acc_ref)

    def normalised(ch, dv):
        a0 = ch.acc_rows.start
        return acc_ref[a0:a0 + dv, :] / acc_ref[a0 + dv:a0 + dv + 1, :]

    for hd in range(MLA_HEADS):
        oa_ref[0, hd * MLA_V:(hd + 1) * MLA_V, :] = normalised(chains[hd], MLA_V).astype(oa_ref.dtype)

    lam = (jnp.exp(jnp.sum(lq1_ref[...] * lk1_ref[...], axis=-1, keepdims=True))
           - jnp.exp(jnp.sum(lq2_ref[...] * lk2_ref[...], axis=-1, keepdims=True))
           + lam_init)
    for hd in range(DIFF_HEADS):
        c = MLA_HEADS + 2 * hd
        o = normalised(chains[c], DIFF_V) - lam * normalised(chains[c + 1], DIFF_V)
        ms = jnp.mean(o * o, axis=0, keepdims=True)
        o = o * lax.rsqrt(ms + EPS) * g_sub_ref[...] * (1.0 - lam_init)
        ob_ref[0, hd * DIFF_V:(hd + 1) * DIFF_V, :] = o.astype(ob_ref.dtype)


def _ffn_kernel(x_ref, oa_ref, ob_ref, w_out_ref, g_ffn_ref, w_gate_ref, w_up_ref,
                conv_w_ref, conv_b_ref, w_down_ref, out_ref, prev_ref, y_ref):
    assert CONV_WIDTH == 3
    si = pl.program_id(1)
    tm = x_ref.shape[1]
    contract0 = (((0,), (0,)), ((), ()))
    n_a = oa_ref.shape[1]
    mix = (lax.dot_general(oa_ref[0], w_out_ref[:n_a, :], contract0,
                           preferred_element_type=jnp.float32)
           + lax.dot_general(ob_ref[0], w_out_ref[n_a:, :], contract0,
                             preferred_element_type=jnp.float32))
    x1 = x_ref[0] + mix
    h = _rms_rows(x1, g_ffn_ref[...]).astype(jnp.bfloat16)

    @pl.when(si == 0)
    def _():
        prev_ref[...] = jnp.zeros_like(prev_ref)

    for c0 in range(0, D_FF, FF_CHUNK):
        cw = min(FF_CHUNK, D_FF - c0)
        g = jnp.dot(h, w_gate_ref[:, c0:c0 + cw], preferred_element_type=jnp.float32)
        u = jnp.dot(h, w_up_ref[:, c0:c0 + cw], preferred_element_type=jnp.float32)
        row = lax.broadcasted_iota(jnp.int32, (tm, cw), 0)
        p1 = prev_ref[SUBLANES - 1:SUBLANES, c0:c0 + cw]
        p2 = prev_ref[SUBLANES - 2:SUBLANES - 1, c0:c0 + cw]
        g1 = jnp.where(row == 0, p1, pltpu.roll(g, 1, axis=0))
        g2 = jnp.where(row == 0, p2, jnp.where(row == 1, p1, pltpu.roll(g, 2, axis=0)))
        prev_ref[:, c0:c0 + cw] = g[tm - SUBLANES:tm, :]
        cg = (conv_b_ref[:, c0:c0 + cw] + g2 * conv_w_ref[0:1, c0:c0 + cw]
              + g1 * conv_w_ref[1:2, c0:c0 + cw] + g * conv_w_ref[2:3, c0:c0 + cw])
        y_ref[:, c0:c0 + cw] = (jax.nn.silu(cg) * u).astype(y_ref.dtype)

    out_ref[0] = x1 + jnp.dot(y_ref[...], w_down_ref[...], preferred_element_type=jnp.float32)


def _rope_tables(seq):
    pos = jnp.arange(seq, dtype=jnp.float32)[:, None]

    def tables(dim):
        inv = 1.0 / (ROPE_THETA ** (jnp.arange(0, dim, 2, dtype=jnp.float32) / dim))
        ang = pos * inv[None, :]
        return jnp.cos(ang), jnp.sin(ang)

    ca, sa = tables(MLA_ROPE)
    cb, sb = tables(DIFF_D)
    return ca.T, sa.T, cb.T, sb.T


def _const_spec(shape):
    return pl.BlockSpec(shape, lambda *_: (0,) * len(shape))


def kernel(x, attn_norm_g, w_in, q_a_norm_g, w_q_up, kv_a_norm_g, w_kv_up, mla_q_norm_g,
           mla_k_norm_g, diff_q_norm_g, diff_k_norm_g, lambda_q1, lambda_k1, lambda_q2, lambda_k2,
           diff_subln_g, w_out, ffn_norm_g, w_gate, w_up, conv_w, conv_b, w_down):
    B, S, _ = x.shape
    depth = w_in.shape[0]
    bf = jnp.bfloat16
    cos_a, sin_a, cos_b, sin_b = _rope_tables(S)

    tk, tq = KEY_TILE, 2 * KEY_TILE
    tm_proj, tm_ffn = PROJ_TOKENS, FFN_TOKENS
    assert tm_proj % tk == 0 and S % tq == 0 and S % tm_proj == 0 and S % tm_ffn == 0
    nk = S // tk

    for l in range(depth):
        w_in_t = w_in[l].T.astype(bf)
        w_q_t = w_q_up[l].T.astype(bf)
        wkv_t = w_kv_up[l].T.reshape(MLA_HEADS, MLA_NOPE + MLA_V, KV_RANK)
        w_k_t = wkv_t[:, :MLA_NOPE].reshape(MLA_HEADS * MLA_NOPE, KV_RANK).astype(bf)
        w_v_t = wkv_t[:, MLA_NOPE:].reshape(MLA_HEADS * MLA_V, KV_RANK).astype(bf)

        n_tok_tiles = S // tm_proj
        tok3 = lambda b, s: (b, s, 0)
        feat3 = lambda b, s: (b, 0, s)
        blk4 = lambda b, s: (b, s, 0, 0)
        rope_a_spec = pl.BlockSpec((MLA_ROPE // 2, tm_proj), lambda b, s: (0, s))
        rope_b_spec = pl.BlockSpec((DIFF_D // 2, tm_proj), lambda b, s: (0, s))
        cast_weights = (w_gate[l], w_up[l], w_down[l], w_out[l])
        n_steps = B * n_tok_tiles
        assert all(w.shape[0] % (n_steps * 2 * SUBLANES) == 0 for w in cast_weights)
        cast_specs = [pl.BlockSpec((w.shape[0] // n_steps, w.shape[1]),
                                   lambda b, s: (b * n_tok_tiles + s, 0)) for w in cast_weights]
        qt, k_a, vt, dqt, dk, dvt, w_gate_bf, w_up_bf, w_down_bf, w_out_bf = pl.pallas_call(
            _proj_kernel,
            grid=(B, n_tok_tiles),
            in_specs=[
                pl.BlockSpec((1, tm_proj, D_MODEL), tok3),
                _const_spec((1, D_MODEL)),
                _const_spec((IN_COLS, D_MODEL)),
                _const_spec((Q_RANK, 1)),
                _const_spec((MLA_HEADS * MLA_QK, Q_RANK)),
                _const_spec((KV_RANK, 1)),
                _const_spec((MLA_HEADS * MLA_NOPE, KV_RANK)),
                _const_spec((MLA_HEADS * MLA_V, KV_RANK)),
                _const_spec((MLA_QK, 1)), _const_spec((MLA_QK, 1)),
                _const_spec((DIFF_D, 1)), _const_spec((DIFF_D, 1)),
                rope_a_spec, rope_a_spec, rope_b_spec, rope_b_spec,
                *cast_specs,
            ],
            out_specs=[
                pl.BlockSpec((1, MLA_HEADS * LANES, tm_proj), feat3),
                pl.BlockSpec((1, tm_proj, MLA_HEADS * LANES), tok3),
                pl.BlockSpec((1, tm_proj // tk, MLA_HEADS * MLA_V, tk), blk4),
                pl.BlockSpec((1, DIFF_HEADS * 2 * LANES, tm_proj), feat3),
                pl.BlockSpec((1, tm_proj, DIFF_HEADS * LANES), tok3),
                pl.BlockSpec((1, tm_proj // tk, DIFF_HEADS * DIFF_V, tk), blk4),
                *cast_specs,
            ],
            out_shape=[
                jax.ShapeDtypeStruct((B, MLA_HEADS * LANES, S), bf),
                jax.ShapeDtypeStruct((B, S, MLA_HEADS * LANES), bf),
                jax.ShapeDtypeStruct((B, nk, MLA_HEADS * MLA_V, tk), bf),
                jax.ShapeDtypeStruct((B, DIFF_HEADS * 2 * LANES, S), bf),
                jax.ShapeDtypeStruct((B, S, DIFF_HEADS * LANES), bf),
                jax.ShapeDtypeStruct((B, nk, DIFF_HEADS * DIFF_V, tk), bf),
                *[jax.ShapeDtypeStruct(w.shape, bf) for w in cast_weights],
            ],
            compiler_params=pltpu.CompilerParams(
                dimension_semantics=("arbitrary", "arbitrary"), vmem_limit_bytes=VMEM_LIMIT),
            name="proj",
        )(x, attn_norm_g[l].reshape(1, -1), w_in_t, q_a_norm_g[l].reshape(-1, 1), w_q_t,
          kv_a_norm_g[l].reshape(-1, 1), w_k_t, w_v_t,
          mla_q_norm_g[l].reshape(-1, 1), mla_k_norm_g[l].reshape(-1, 1),
          diff_q_norm_g[l].reshape(-1, 1), diff_k_norm_g[l].reshape(-1, 1),
          cos_a, sin_a, cos_b, sin_b, *cast_weights)

        nq = S // tq
        n_chains = MLA_HEADS + 2 * DIFF_HEADS
        lam_init = 0.8 - 0.6 * math.exp(-0.3 * l)
        lam_spec = _const_spec((1, DIFF_D))
        q_tile = lambda b, i: (b, 0, i)
        per_batch = lambda b, i: (b, 0, 0, 0)
        o_a, o_b = pl.pallas_call(
            functools.partial(_attn_kernel, tq=tq, tk=tk, lam_init=lam_init),
            grid=(B, nq),
            in_specs=[
                lam_spec, lam_spec, lam_spec, lam_spec,
                _const_spec((DIFF_V, 1)),
                pl.BlockSpec((1, MLA_HEADS * LANES, tq), q_tile),
                pl.BlockSpec((1, nk, tk, MLA_HEADS * LANES), per_batch),
                pl.BlockSpec((1, nk, MLA_HEADS * MLA_V, tk), per_batch),
                pl.BlockSpec((1, DIFF_HEADS * 2 * LANES, tq), q_tile),
                pl.BlockSpec((1, nk, tk, DIFF_HEADS * LANES), per_batch),
                pl.BlockSpec((1, nk, DIFF_HEADS * DIFF_V, tk), per_batch),
            ],
            out_specs=[pl.BlockSpec((1, MLA_HEADS * MLA_V, tq), q_tile),
                       pl.BlockSpec((1, DIFF_HEADS * DIFF_V, tq), q_tile)],
            out_shape=[jax.ShapeDtypeStruct((B, MLA_HEADS * MLA_V, S), bf),
                       jax.ShapeDtypeStruct((B, DIFF_HEADS * DIFF_V, S), bf)],
            scratch_shapes=[
                pltpu.VMEM((2, n_chains, tk, tq), bf),
                pltpu.VMEM((2, n_chains, 1, tq), jnp.float32),
                pltpu.VMEM((n_chains, 1, tq), jnp.float32),
                pltpu.VMEM((MLA_HEADS * (MLA_V + SUM_ROWS)
                            + 2 * DIFF_HEADS * (DIFF_V + SUM_ROWS), tq), jnp.float32)],
            compiler_params=pltpu.CompilerParams(
                dimension_semantics=("arbitrary", "arbitrary"), vmem_limit_bytes=VMEM_LIMIT),
            name="attn",
        )(lambda_q1[l].reshape(1, -1), lambda_k1[l].reshape(1, -1),
          lambda_q2[l].reshape(1, -1), lambda_k2[l].reshape(1, -1),
          diff_subln_g[l].reshape(-1, 1),
          qt, k_a.reshape(B, nk, tk, MLA_HEADS * LANES), vt,
          dqt, dk.reshape(B, nk, tk, DIFF_HEADS * LANES), dvt)

        n_a = MLA_HEADS * MLA_V
        x = pl.pallas_call(
            _ffn_kernel,
            grid=(B, S // tm_ffn),
            in_specs=[
                pl.BlockSpec((1, tm_ffn, D_MODEL), lambda b, s: (b, s, 0)),
                pl.BlockSpec((1, n_a, tm_ffn), lambda b, s: (b, 0, s)),
                pl.BlockSpec((1, D_MODEL - n_a, tm_ffn), lambda b, s: (b, 0, s)),
                _const_spec((D_MODEL, D_MODEL)),
                _const_spec((1, D_MODEL)),
                _const_spec((D_MODEL, D_FF)),
                _const_spec((D_MODEL, D_FF)),
                _const_spec((CONV_WIDTH, D_FF)),
                _const_spec((1, D_FF)),
                _const_spec((D_FF, D_MODEL)),
            ],
            out_specs=pl.BlockSpec((1, tm_ffn, D_MODEL), lambda b, s: (b, s, 0)),
            out_shape=jax.ShapeDtypeStruct((B, S, D_MODEL), x.dtype),
            scratch_shapes=[pltpu.VMEM((SUBLANES, D_FF), jnp.float32),
                            pltpu.VMEM((tm_ffn, D_FF), bf)],
            compiler_params=pltpu.CompilerParams(
                dimension_semantics=("arbitrary", "arbitrary"), vmem_limit_bytes=VMEM_LIMIT),
            name="ffn",
        )(x, o_a, o_b, w_out_bf, ffn_norm_g[l].reshape(1, -1), w_gate_bf, w_up_bf,
          conv_w[l], conv_b[l].reshape(1, -1), w_down_bf)
    return x
```

```python
import functools
import math
from typing import Any, NamedTuple

import jax
import jax.numpy as jnp
from jax import lax
from jax.experimental import pallas as pl
from jax.experimental.pallas import tpu as pltpu

D_MODEL = 1024
MLA_HEADS = 8
MLA_NOPE = 64
MLA_ROPE = 32
MLA_V = 64
MLA_QK = MLA_NOPE + MLA_ROPE
Q_RANK = 384
KV_RANK = 256
DIFF_HEADS = 4
DIFF_D = 64
DIFF_V = 2 * DIFF_D
D_FF = 2816
CONV_WIDTH = 3
ROPE_THETA = 10000.0
EPS = 1e-6
LANES = 128
SUBLANES = 8
SUM_ROWS = 16
LOG2E = math.log2(math.e)
R_Q = 0
R_KV = R_Q + Q_RANK
R_KPE = R_KV + KV_RANK
R_DQ = R_KPE + MLA_ROPE
R_DK = R_DQ + DIFF_HEADS * 2 * DIFF_D
R_DV = R_DK + DIFF_HEADS * 2 * DIFF_D
IN_COLS = R_DV + DIFF_HEADS * DIFF_V

KEY_TILE = 256
PROJ_TOKENS = 1024
FFN_TOKENS = 1024
FF_CHUNK = 1024
V7X_VMEM_BYTES = 64 * 1024 * 1024
VMEM_LIMIT = V7X_VMEM_BYTES - 8 * 1024 * 1024


def _rms_rows(x, g):
    ms = jnp.mean(x * x, axis=-1, keepdims=True)
    return x * lax.rsqrt(ms + EPS) * g


def _rms_cols(xt, n, scale):
    ms = jnp.sum(xt * xt, axis=0, keepdims=True) * (1.0 / n)
    return scale * lax.rsqrt(scale * scale * ms + EPS)


def _rope_cols(xt, cos, sin):
    half = xt.shape[0] // 2
    x1, x2 = xt[:half], xt[half:]
    return x1 * cos - x2 * sin, x2 * cos + x1 * sin


def _proj_kernel(x_ref, g_attn_ref, w_in_ref, g_qa_ref, w_q_ref, g_kva_ref, w_k_ref, w_v_ref,
                 g_q_ref, g_k_ref, g_dq_ref, g_dk_ref,
                 cos_a_ref, sin_a_ref, cos_b_ref, sin_b_ref,
                 w_gate_ref, w_up_ref, w_down_ref, w_out_ref,
                 qt_ref, k_ref, vt_ref, dqt_ref, dk_ref, dvt_ref,
                 w_gate_bf_ref, w_up_bf_ref, w_down_bf_ref, w_out_bf_ref):
    tm = x_ref.shape[1]
    bf = jnp.bfloat16
    for src_ref, dst_ref in ((w_gate_ref, w_gate_bf_ref), (w_up_ref, w_up_bf_ref),
                             (w_down_ref, w_down_bf_ref), (w_out_ref, w_out_bf_ref)):
        dst_ref[...] = src_ref[...].astype(bf)
    x = x_ref[0]
    r_tok = lax.rsqrt(jnp.mean(x * x, axis=-1, keepdims=True) + EPS)
    r_tok = jnp.transpose(jnp.broadcast_to(r_tok, (tm, LANES)))[0:1, :]
    h = (x * g_attn_ref[...]).astype(bf)

    def in_proj(r0, r1):
        return lax.dot_general(w_in_ref[r0:r1, :], h, (((1,), (1,)), ((), ())),
                               preferred_element_type=jnp.float32)

    cos_a, sin_a = cos_a_ref[...], sin_a_ref[...]
    cos_b, sin_b = cos_b_ref[...], sin_b_ref[...]
    zeros_pad = jnp.zeros((LANES - MLA_QK, tm), jnp.float32)
    one = jnp.ones((1, tm), jnp.float32)

    def head_a(nope, pe_roped, g):
        nope = nope * _rms_cols(nope, MLA_NOPE, one) * g[:MLA_NOPE]
        return jnp.concatenate([nope, *pe_roped, zeros_pad], axis=0)

    def rope_a(pe, g, scale):
        pe = pe * _rms_cols(pe, MLA_ROPE, scale) * g[MLA_NOPE:]
        return _rope_cols(pe, cos_a, sin_a)

    def head_b(xt, g):
        xt = xt * _rms_cols(xt, DIFF_D, r_tok) * g
        return jnp.concatenate(_rope_cols(xt, cos_b, sin_b), axis=0)

    cq = in_proj(R_Q, R_KV)
    lat = in_proj(R_KV, R_DQ)
    dq = in_proj(R_DQ, R_DK)

    cq = (cq * _rms_cols(cq, Q_RANK, r_tok) * g_qa_ref[...]).astype(bf)
    q = jnp.dot(w_q_ref[...], cq, preferred_element_type=jnp.float32)
    ckv = lat[:KV_RANK]
    ckv = (ckv * _rms_cols(ckv, KV_RANK, r_tok) * g_kva_ref[...]).astype(bf)
    kn = jnp.dot(w_k_ref[...], ckv, preferred_element_type=jnp.float32)
    dk = in_proj(R_DK, R_DV)
    v = jnp.dot(w_v_ref[...], ckv, preferred_element_type=jnp.float32).astype(bf)
    dv = (in_proj(R_DV, IN_COLS) * r_tok).astype(bf)
    tk = vt_ref.shape[3]
    for t in range(tm // tk):
        vt_ref[0, t] = v[:, t * tk:(t + 1) * tk]
        dvt_ref[0, t] = dv[:, t * tk:(t + 1) * tk]

    g_q = g_q_ref[...] * (MLA_QK ** -0.5 * LOG2E)
    for hd in range(MLA_HEADS):
        r0 = hd * MLA_QK
        qt_ref[0, hd * LANES:(hd + 1) * LANES, :] = head_a(
            q[r0:r0 + MLA_NOPE], rope_a(q[r0 + MLA_NOPE:r0 + MLA_QK], g_q, one), g_q).astype(bf)

    g_k = g_k_ref[...]
    kpe = rope_a(lat[KV_RANK:], g_k, r_tok)
    for hd in range(MLA_HEADS):
        kt = head_a(kn[hd * MLA_NOPE:(hd + 1) * MLA_NOPE], kpe, g_k)
        k_ref[0, :, hd * LANES:(hd + 1) * LANES] = kt.T.astype(bf)

    g_dq = g_dq_ref[...] * (DIFF_D ** -0.5 * LOG2E)
    g_dk = g_dk_ref[...]
    zeros_half = jnp.zeros((DIFF_D, tm), bf)
    for hd in range(DIFF_HEADS):
        r0 = hd * 2 * DIFF_D
        q1 = head_b(dq[r0:r0 + DIFF_D], g_dq).astype(bf)
        q2 = head_b(dq[r0 + DIFF_D:r0 + 2 * DIFF_D], g_dq).astype(bf)
        b0 = 2 * hd * LANES
        dqt_ref[0, b0:b0 + LANES, :] = jnp.concatenate([q1, zeros_half], axis=0)
        dqt_ref[0, b0 + LANES:b0 + 2 * LANES, :] = jnp.concatenate([zeros_half, q2], axis=0)
        dkt = jnp.concatenate([head_b(dk[r0:r0 + DIFF_D], g_dk),
                               head_b(dk[r0 + DIFF_D:r0 + 2 * DIFF_D], g_dk)], axis=0)
        dk_ref[0, :, hd * LANES:(hd + 1) * LANES] = dkt.T.astype(bf)


def _causal_mask(s_t, tk, tq):
    key = lax.broadcasted_iota(jnp.int32, (tk, tq), 0)
    qry = lax.broadcasted_iota(jnp.int32, (tk, tq), 1)
    return jnp.where(key <= qry, s_t, -jnp.inf)


class _Chain(NamedTuple):
    k_ref: Any
    k_lanes: slice
    qt_ref: Any
    q_rows: slice
    vt_ref: Any
    v_rows: slice
    acc_rows: slice


def _pipelined_sweep(chains, qi, tq, tk, p_ref, al_ref, m_ref, acc_ref):
    assert tq == 2 * tk
    lower, upper = slice(0, tk), slice(tk, tq)

    def stage_a(j, cols):
        return [jnp.dot(ch.k_ref[0, j, :, ch.k_lanes], ch.qt_ref[0, ch.q_rows, cols],
                        preferred_element_type=jnp.float32) for ch in chains]

    def stage_b(s_all, slot, cols, first=False):
        for c, s_t in enumerate(s_all):
            if first:
                m_new = jnp.max(s_t, axis=0, keepdims=True)
            else:
                m_old = m_ref[c, :, cols]
                m_new = jnp.maximum(m_old, jnp.max(s_t, axis=0, keepdims=True))
                al_ref[slot, c, :, cols] = jnp.exp2(m_old - m_new)
            m_ref[c, :, cols] = m_new
            p_ref[slot, c, :, cols] = jnp.exp2(s_t - m_new).astype(p_ref.dtype)

    def stage_c(j, slot, cols):
        ones = jnp.ones((SUM_ROWS, tk), p_ref.dtype)
        for c, ch in enumerate(chains):
            vt_ones = jnp.concatenate([ch.vt_ref[0, j, ch.v_rows, :], ones], axis=0)
            pv = jnp.dot(vt_ones, p_ref[slot, c, :, cols], preferred_element_type=jnp.float32)
            acc_ref[ch.acc_rows, cols] = al_ref[slot, c, :, cols] * acc_ref[ch.acc_rows, cols] + pv

    full = slice(0, tq)
    d0, d1 = 2 * qi, 2 * qi + 1
    acc_ref[...] = jnp.zeros_like(acc_ref)
    al_ref[0] = jnp.ones(al_ref.shape[1:], al_ref.dtype)
    s_all = stage_a(d0, full)
    stage_b([jnp.concatenate([_causal_mask(s_t[:, lower], tk, tk), s_t[:, upper]], axis=1)
             for s_t in s_all], 0, full, first=True)

    def two_steps(k, carry):
        s_all = stage_a(2 * k, full)
        stage_c(jnp.where(k == 0, d0, 2 * k - 1), 0, full)
        stage_b(s_all, 1, full)
        s_all = stage_a(2 * k + 1, full)
        stage_c(2 * k, 1, full)
        stage_b(s_all, 0, full)
        return carry

    lax.fori_loop(0, qi, two_steps, 0)
    s_all = stage_a(d1, upper)
    stage_c(jnp.where(qi == 0, d0, 2 * qi - 1), 0, full)
    stage_b([_causal_mask(s_t, tk, tk) for s_t in s_all], 1, upper)
    stage_c(d1, 1, upper)


def _attn_kernel(lq1_ref, lk1_ref, lq2_ref, lk2_ref, g_sub_ref,
                 qt_ref, k_ref, vt_ref, dqt_ref, dk_ref, dvt_ref, oa_ref, ob_ref,
                 p_ref, al_ref, m_ref, acc_ref, *, tq, tk, lam_init):
    rows_a, rows_b = MLA_V + SUM_ROWS, DIFF_V + SUM_ROWS
    base_b = MLA_HEADS * rows_a
    lane_group = lambda i: slice(i * LANES, (i + 1) * LANES)
    chains = [_Chain(k_ref, lane_group(hd), qt_ref, lane_group(hd),
                     vt_ref, slice(hd * MLA_V, (hd + 1) * MLA_V),
                     slice(hd * rows_a, (hd + 1) * rows_a)) for hd in range(MLA_HEADS)]
    chains += [_Chain(dk_ref, lane_group(c // 2), dqt_ref, lane_group(c),
                      dvt_ref, slice((c // 2) * DIFF_V, (c // 2 + 1) * DIFF_V),
                      slice(base_b + c * rows_b, base_b + (c + 1) * rows_b))
               for c in range(2 * DIFF_HEADS)]
    _pipelined_sweep(chains, pl.program_id(1), tq, tk, p_ref, al_ref, m_ref, acc_ref)

    def normalised(ch, dv):
        a0 = ch.acc_rows.start
        return acc_ref[a0:a0 + dv, :] / acc_ref[a0 + dv:a0 + dv + 1, :]

    for hd in range(MLA_HEADS):
        oa_ref[0, hd * MLA_V:(hd + 1) * MLA_V, :] = normalised(chains[hd], MLA_V).astype(oa_ref.dtype)

    lam = (jnp.exp(jnp.sum(lq1_ref[...] * lk1_ref[...], axis=-1, keepdims=True))
           - jnp.exp(jnp.sum(lq2_ref[...] * lk2_ref[...], axis=-1, keepdims=True))
           + lam_init)
    for hd in range(DIFF_HEADS):
        c = MLA_HEADS + 2 * hd
        o = normalised(chains[c], DIFF_V) - lam * normalised(chains[c + 1], DIFF_V)
        ms = jnp.mean(o * o, axis=0, keepdims=True)
        o = o * lax.rsqrt(ms + EPS) * g_sub_ref[...] * (1.0 - lam_init)
        ob_ref[0, hd * DIFF_V:(hd + 1) * DIFF_V, :] = o.astype(ob_ref.dtype)


def _ffn_kernel(x_ref, oa_ref, ob_ref, w_out_ref, g_ffn_ref, w_gate_ref, w_up_ref,
                conv_w_ref, conv_b_ref, w_down_ref, out_ref, prev_ref, y_ref):
    assert CONV_WIDTH == 3
    si = pl.program_id(1)
    tm = x_ref.shape[1]
    contract0 = (((0,), (0,)), ((), ()))
    n_a = oa_ref.shape[1]
    mix = (lax.dot_general(oa_ref[0], w_out_ref[:n_a, :], contract0,
                           preferred_element_type=jnp.float32)
           + lax.dot_general(ob_ref[0], w_out_ref[n_a:, :], contract0,
                             preferred_element_type=jnp.float32))
    x1 = x_ref[0] + mix
    h = _rms_rows(x1, g_ffn_ref[...]).astype(jnp.bfloat16)

    @pl.when(si == 0)
    def _():
        prev_ref[...] = jnp.zeros_like(prev_ref)

    for c0 in range(0, D_FF, FF_CHUNK):
        cw = min(FF_CHUNK, D_FF - c0)
        g = jnp.dot(h, w_gate_ref[:, c0:c0 + cw], preferred_element_type=jnp.float32)
        u = jnp.dot(h, w_up_ref[:, c0:c0 + cw], preferred_element_type=jnp.float32)
        row = lax.broadcasted_iota(jnp.int32, (tm, cw), 0)
        p1 = prev_ref[SUBLANES - 1:SUBLANES, c0:c0 + cw]
        p2 = prev_ref[SUBLANES - 2:SUBLANES - 1, c0:c0 + cw]
        g1 = jnp.where(row == 0, p1, pltpu.roll(g, 1, axis=0))
        g2 = jnp.where(row == 0, p2, jnp.where(row == 1, p1, pltpu.roll(g, 2, axis=0)))
        prev_ref[:, c0:c0 + cw] = g[tm - SUBLANES:tm, :]
        cg = (conv_b_ref[:, c0:c0 + cw] + g2 * conv_w_ref[0:1, c0:c0 + cw]
              + g1 * conv_w_ref[1:2, c0:c0 + cw] + g * conv_w_ref[2:3, c0:c0 + cw])
        y_ref[:, c0:c0 + cw] = (jax.nn.silu(cg) * u).astype(y_ref.dtype)

    out_ref[0] = x1 + jnp.dot(y_ref[...], w_down_ref[...], preferred_element_type=jnp.float32)


def _rope_tables(seq):
    pos = jnp.arange(seq, dtype=jnp.float32)[:, None]

    def tables(dim):
        inv = 1.0 / (ROPE_THETA ** (jnp.arange(0, dim, 2, dtype=jnp.float32) / dim))
        ang = pos * inv[None, :]
        return jnp.cos(ang), jnp.sin(ang)

    ca, sa = tables(MLA_ROPE)
    cb, sb = tables(DIFF_D)
    return ca.T, sa.T, cb.T, sb.T


def _const_spec(shape):
    return pl.BlockSpec(shape, lambda *_: (0,) * len(shape))


def kernel(x, attn_norm_g, w_in, q_a_norm_g, w_q_up, kv_a_norm_g, w_kv_up, mla_q_norm_g,
           mla_k_norm_g, diff_q_norm_g, diff_k_norm_g, lambda_q1, lambda_k1, lambda_q2, lambda_k2,
           diff_subln_g, w_out, ffn_norm_g, w_gate, w_up, conv_w, conv_b, w_down):
    B, S, _ = x.shape
    depth = w_in.shape[0]
    bf = jnp.bfloat16
    cos_a, sin_a, cos_b, sin_b = _rope_tables(S)

    tk, tq = KEY_TILE, 2 * KEY_TILE
    tm_proj, tm_ffn = PROJ_TOKENS, FFN_TOKENS
    assert tm_proj % tk == 0 and S % tq == 0 and S % tm_proj == 0 and S % tm_ffn == 0
    nk = S // tk

    for l in range(depth):
        w_in_t = w_in[l].T.astype(bf)
        w_q_t = w_q_up[l].T.astype(bf)
        wkv_t = w_kv_up[l].T.reshape(MLA_HEADS, MLA_NOPE + MLA_V, KV_RANK)
        w_k_t = wkv_t[:, :MLA_NOPE].reshape(MLA_HEADS * MLA_NOPE, KV_RANK).astype(bf)
        w_v_t = wkv_t[:, MLA_NOPE:].reshape(MLA_HEADS * MLA_V, KV_RANK).astype(bf)

        n_tok_tiles = S // tm_proj
        tok3 = lambda b, s: (b, s, 0)
        feat3 = lambda b, s: (b, 0, s)
        blk4 = lambda b, s: (b, s, 0, 0)
        rope_a_spec = pl.BlockSpec((MLA_ROPE // 2, tm_proj), lambda b, s: (0, s))
        rope_b_spec = pl.BlockSpec((DIFF_D // 2, tm_proj), lambda b, s: (0, s))
        cast_weights = (w_gate[l], w_up[l], w_down[l], w_out[l])
        n_steps = B * n_tok_tiles
        assert all(w.shape[0] % (n_steps * 2 * SUBLANES) == 0 for w in cast_weights)
        cast_specs = [pl.BlockSpec((w.shape[0] // n_steps, w.shape[1]),
                                   lambda b, s: (b * n_tok_tiles + s, 0)) for w in cast_weights]
        qt, k_a, vt, dqt, dk, dvt, w_gate_bf, w_up_bf, w_down_bf, w_out_bf = pl.pallas_call(
            _proj_kernel,
            grid=(B, n_tok_tiles),
            in_specs=[
                pl.BlockSpec((1, tm_proj, D_MODEL), tok3),
                _const_spec((1, D_MODEL)),
                _const_spec((IN_COLS, D_MODEL)),
                _const_spec((Q_RANK, 1)),
                _const_spec((MLA_HEADS * MLA_QK, Q_RANK)),
                _const_spec((KV_RANK, 1)),
                _const_spec((MLA_HEADS * MLA_NOPE, KV_RANK)),
                _const_spec((MLA_HEADS * MLA_V, KV_RANK)),
                _const_spec((MLA_QK, 1)), _const_spec((MLA_QK, 1)),
                _const_spec((DIFF_D, 1)), _const_spec((DIFF_D, 1)),
                rope_a_spec, rope_a_spec, rope_b_spec, rope_b_spec,
                *cast_specs,
            ],
            out_specs=[
                pl.BlockSpec((1, MLA_HEADS * LANES, tm_proj), feat3),
                pl.BlockSpec((1, tm_proj, MLA_HEADS * LANES), tok3),
                pl.BlockSpec((1, tm_proj // tk, MLA_HEADS * MLA_V, tk), blk4),
                pl.BlockSpec((1, DIFF_HEADS * 2 * LANES, tm_proj), feat3),
                pl.BlockSpec((1, tm_proj, DIFF_HEADS * LANES), tok3),
                pl.BlockSpec((1, tm_proj // tk, DIFF_HEADS * DIFF_V, tk), blk4),
                *cast_specs,
            ],
            out_shape=[
                jax.ShapeDtypeStruct((B, MLA_HEADS * LANES, S), bf),
                jax.ShapeDtypeStruct((B, S, MLA_HEADS * LANES), bf),
                jax.ShapeDtypeStruct((B, nk, MLA_HEADS * MLA_V, tk), bf),
                jax.ShapeDtypeStruct((B, DIFF_HEADS * 2 * LANES, S), bf),
                jax.ShapeDtypeStruct((B, S, DIFF_HEADS * LANES), bf),
                jax.ShapeDtypeStruct((B, nk, DIFF_HEADS * DIFF_V, tk), bf),
                *[jax.ShapeDtypeStruct(w.shape, bf) for w in cast_weights],
            ],
            compiler_params=pltpu.CompilerParams(
                dimension_semantics=("arbitrary", "arbitrary"), vmem_limit_bytes=VMEM_LIMIT),
            name="proj",
        )(x, attn_norm_g[l].reshape(1, -1), w_in_t, q_a_norm_g[l].reshape(-1, 1), w_q_t,
          kv_a_norm_g[l].reshape(-1, 1), w_k_t, w_v_t,
          mla_q_norm_g[l].reshape(-1, 1), mla_k_norm_g[l].reshape(-1, 1),
          diff_q_norm_g[l].reshape(-1, 1), diff_k_norm_g[l].reshape(-1, 1),
          cos_a, sin_a, cos_b, sin_b, *cast_weights)

        nq = S // tq
        n_chains = MLA_HEADS + 2 * DIFF_HEADS
        lam_init = 0.8 - 0.6 * math.exp(-0.3 * l)
        lam_spec = _const_spec((1, DIFF_D))
        q_tile = lambda b, i: (b, 0, i)
        per_batch = lambda b, i: (b, 0, 0, 0)
        o_a, o_b = pl.pallas_call(
            functools.partial(_attn_kernel, tq=tq, tk=tk, lam_init=lam_init),
            grid=(B, nq),
            in_specs=[
                lam_spec, lam_spec, lam_spec, lam_spec,
                _const_spec((DIFF_V, 1)),
                pl.BlockSpec((1, MLA_HEADS * LANES, tq), q_tile),
                pl.BlockSpec((1, nk, tk, MLA_HEADS * LANES), per_batch),
                pl.BlockSpec((1, nk, MLA_HEADS * MLA_V, tk), per_batch),
                pl.BlockSpec((1, DIFF_HEADS * 2 * LANES, tq), q_tile),
                pl.BlockSpec((1, nk, tk, DIFF_HEADS * LANES), per_batch),
                pl.BlockSpec((1, nk, DIFF_HEADS * DIFF_V, tk), per_batch),
            ],
            out_specs=[pl.BlockSpec((1, MLA_HEADS * MLA_V, tq), q_tile),
                       pl.BlockSpec((1, DIFF_HEADS * DIFF_V, tq), q_tile)],
            out_shape=[jax.ShapeDtypeStruct((B, MLA_HEADS * MLA_V, S), bf),
                       jax.ShapeDtypeStruct((B, DIFF_HEADS * DIFF_V, S), bf)],
            scratch_shapes=[
                pltpu.VMEM((2, n_chains, tk, tq), bf),
                pltpu.VMEM((2, n_chains, 1, tq), jnp.float32),
                pltpu.VMEM((n_chains, 1, tq), jnp.float32),
                pltpu.VMEM((MLA_HEADS * (MLA_V + SUM_ROWS)
                            + 2 * DIFF_HEADS * (DIFF_V + SUM_ROWS), tq), jnp.float32)],
            compiler_params=pltpu.CompilerParams(
                dimension_semantics=("arbitrary", "arbitrary"), vmem_limit_bytes=VMEM_LIMIT),
            name="attn",
        )(lambda_q1[l].reshape(1, -1), lambda_k1[l].reshape(1, -1),
          lambda_q2[l].reshape(1, -1), lambda_k2[l].reshape(1, -1),
          diff_subln_g[l].reshape(-1, 1),
          qt, k_a.reshape(B, nk, tk, MLA_HEADS * LANES), vt,
          dqt, dk.reshape(B, nk, tk, DIFF_HEADS * LANES), dvt)

        n_a = MLA_HEADS * MLA_V
        x = pl.pallas_call(
            _ffn_kernel,
            grid=(B, S // tm_ffn),
            in_specs=[
                pl.BlockSpec((1, tm_ffn, D_MODEL), lambda b, s: (b, s, 0)),
                pl.BlockSpec((1, n_a, tm_ffn), lambda b, s: (b, 0, s)),
                pl.BlockSpec((1, D_MODEL - n_a, tm_ffn), lambda b, s: (b, 0, s)),
                _const_spec((D_MODEL, D_MODEL)),
                _const_spec((1, D_MODEL)),
                _const_spec((D_MODEL, D_FF)),
                _const_spec((D_MODEL, D_FF)),
                _const_spec((CONV_WIDTH, D_FF)),
                _const_spec((1, D_FF)),
                _const_spec((D_FF, D_MODEL)),
            ],
            out_specs=pl.BlockSpec((1, tm_ffn, D_MODEL), lambda b, s: (b, s, 0)),
            out_shape=jax.ShapeDtypeStruct((B, S, D_MODEL), x.dtype),
            scratch_shapes=[pltpu.VMEM((SUBLANES, D_FF), jnp.float32),
                            pltpu.VMEM((tm_ffn, D_FF), bf)],
            compiler_params=pltpu.CompilerParams(
                dimension_semantics=("arbitrary", "arbitrary"), vmem_limit_bytes=VMEM_LIMIT),
            name="ffn",
        )(x, o_a, o_b, w_out_bf, ffn_norm_g[l].reshape(1, -1), w_gate_bf, w_up_bf,
          conv_w[l], conv_b[l].reshape(1, -1), w_down_bf)
    return x
```

```python
import functools
import math
from typing import Any, NamedTuple

import jax
import jax.numpy as jnp
from jax import lax
from jax.experimental import pallas as pl
from jax.experimental.pallas import tpu as pltpu

D_MODEL = 1024
MLA_HEADS = 8
MLA_NOPE = 64
MLA_ROPE = 32
MLA_V = 64
MLA_QK = MLA_NOPE + MLA_ROPE
Q_RANK = 384
KV_RANK = 256
DIFF_HEADS = 4
DIFF_D = 64
DIFF_V = 2 * DIFF_D
D_FF = 2816
CONV_WIDTH = 3
ROPE_THETA = 10000.0
EPS = 1e-6
LANES = 128
SUBLANES = 8
SUM_ROWS = 16
LOG2E = math.log2(math.e)
R_Q = 0
R_KV = R_Q + Q_RANK
R_KPE = R_KV + KV_RANK
R_DQ = R_KPE + MLA_ROPE
R_DK = R_DQ + DIFF_HEADS * 2 * DIFF_D
R_DV = R_DK + DIFF_HEADS * 2 * DIFF_D
IN_COLS = R_DV + DIFF_HEADS * DIFF_V

KEY_TILE = 256
PROJ_TOKENS = 1024
FFN_TOKENS = 1024
FF_CHUNK = 1024
V7X_VMEM_BYTES = 64 * 1024 * 1024
VMEM_LIMIT = V7X_VMEM_BYTES - 8 * 1024 * 1024


def _rms_rows(x, g):
    ms = jnp.mean(x * x, axis=-1, keepdims=True)
    return x * lax.rsqrt(ms + EPS) * g


def _rms_cols(xt, n, scale):
    ms = jnp.sum(xt * xt, axis=0, keepdims=True) * (1.0 / n)
    return scale * lax.rsqrt(scale * scale * ms + EPS)


def _rope_cols(xt, cos, sin):
    half = xt.shape[0] // 2
    x1, x2 = xt[:half], xt[half:]
    return x1 * cos - x2 * sin, x2 * cos + x1 * sin


def _proj_kernel(x_ref, g_attn_ref, w_in_ref, g_qa_ref, w_q_ref, g_kva_ref, w_k_ref, w_v_ref,
                 g_q_ref, g_k_ref, g_dq_ref, g_dk_ref,
                 cos_a_ref, sin_a_ref, cos_b_ref, sin_b_ref,
                 w_gate_ref, w_up_ref, w_down_ref, w_out_ref,
                 qt_ref, k_ref, vt_ref, dqt_ref, dk_ref, dvt_ref,
                 w_gate_bf_ref, w_up_bf_ref, w_down_bf_ref, w_out_bf_ref):
    tm = x_ref.shape[1]
    bf = jnp.bfloat16
    for src_ref, dst_ref in ((w_gate_ref, w_gate_bf_ref), (w_up_ref, w_up_bf_ref),
                             (w_down_ref, w_down_bf_ref), (w_out_ref, w_out_bf_ref)):
        dst_ref[...] = src_ref[...].astype(bf)
    x = x_ref[0]
    r_tok = lax.rsqrt(jnp.mean(x * x, axis=-1, keepdims=True) + EPS)
    r_tok = jnp.transpose(jnp.broadcast_to(r_tok, (tm, LANES)))[0:1, :]
    h = (x * g_attn_ref[...]).astype(bf)

    def in_proj(r0, r1):
        return lax.dot_general(w_in_ref[r0:r1, :], h, (((1,), (1,)), ((), ())),
                               preferred_element_type=jnp.float32)

    cos_a, sin_a = cos_a_ref[...], sin_a_ref[...]
    cos_b, sin_b = cos_b_ref[...], sin_b_ref[...]
    zeros_pad = jnp.zeros((LANES - MLA_QK, tm), jnp.float32)
    one = jnp.ones((1, tm), jnp.float32)

    def head_a(nope, pe_roped, g):
        nope = nope * _rms_cols(nope, MLA_NOPE, one) * g[:MLA_NOPE]
        return jnp.concatenate([nope, *pe_roped, zeros_pad], axis=0)

    def rope_a(pe, g, scale):
        pe = pe * _rms_cols(pe, MLA_ROPE, scale) * g[MLA_NOPE:]
        return _rope_cols(pe, cos_a, sin_a)

    def head_b(xt, g):
        xt = xt * _rms_cols(xt, DIFF_D, r_tok) * g
        return jnp.concatenate(_rope_cols(xt, cos_b, sin_b), axis=0)

    cq = in_proj(R_Q, R_KV)
    lat = in_proj(R_KV, R_DQ)
    dq = in_proj(R_DQ, R_DK)

    cq = (cq * _rms_cols(cq, Q_RANK, r_tok) * g_qa_ref[...]).astype(bf)
    q = jnp.dot(w_q_ref[...], cq, preferred_element_type=jnp.float32)
    ckv = lat[:KV_RANK]
    ckv = (ckv * _rms_cols(ckv, KV_RANK, r_tok) * g_kva_ref[...]).astype(bf)
    kn = jnp.dot(w_k_ref[...], ckv, preferred_element_type=jnp.float32)
    dk = in_proj(R_DK, R_DV)
    v = jnp.dot(w_v_ref[...], ckv, preferred_element_type=jnp.float32).astype(bf)
    dv = (in_proj(R_DV, IN_COLS) * r_tok).astype(bf)
    tk = vt_ref.shape[3]
    for t in range(tm // tk):
        vt_ref[0, t] = v[:, t * tk:(t + 1) * tk]
        dvt_ref[0, t] = dv[:, t * tk:(t + 1) * tk]

    g_q = g_q_ref[...] * (MLA_QK ** -0.5 * LOG2E)
    for hd in range(MLA_HEADS):
        r0 = hd * MLA_QK
        qt_ref[0, hd * LANES:(hd + 1) * LANES, :] = head_a(
            q[r0:r0 + MLA_NOPE], rope_a(q[r0 + MLA_NOPE:r0 + MLA_QK], g_q, one), g_q).astype(bf)

    g_k = g_k_ref[...]
    kpe = rope_a(lat[KV_RANK:], g_k, r_tok)
    for hd in range(MLA_HEADS):
        kt = head_a(kn[hd * MLA_NOPE:(hd + 1) * MLA_NOPE], kpe, g_k)
        k_ref[0, :, hd * LANES:(hd + 1) * LANES] = kt.T.astype(bf)

    g_dq = g_dq_ref[...] * (DIFF_D ** -0.5 * LOG2E)
    g_dk = g_dk_ref[...]
    zeros_half = jnp.zeros((DIFF_D, tm), bf)
    for hd in range(DIFF_HEADS):
        r0 = hd * 2 * DIFF_D
        q1 = head_b(dq[r0:r0 + DIFF_D], g_dq).astype(bf)
        q2 = head_b(dq[r0 + DIFF_D:r0 + 2 * DIFF_D], g_dq).astype(bf)
        b0 = 2 * hd * LANES
        dqt_ref[0, b0:b0 + LANES, :] = jnp.concatenate([q1, zeros_half], axis=0)
        dqt_ref[0, b0 + LANES:b0 + 2 * LANES, :] = jnp.concatenate([zeros_half, q2], axis=0)
        dkt = jnp.concatenate([head_b(dk[r0:r0 + DIFF_D], g_dk),
                               head_b(dk[r0 + DIFF_D:r0 + 2 * DIFF_D], g_dk)], axis=0)
        dk_ref[0, :, hd * LANES:(hd + 1) * LANES] = dkt.T.astype(bf)


def _causal_mask(s_t, tk, tq):
    key = lax.broadcasted_iota(jnp.int32, (tk, tq), 0)
    qry = lax.broadcasted_iota(jnp.int32, (tk, tq), 1)
    return jnp.where(key <= qry, s_t, -jnp.inf)


class _Chain(NamedTuple):
    k_ref: Any
    k_lanes: slice
    qt_ref: Any
    q_rows: slice
    vt_ref: Any
    v_rows: slice
    acc_rows: slice


def _pipelined_sweep(chains, qi, tq, tk, p_ref, al_ref, m_ref, acc_ref):
    assert tq == 2 * tk
    lower, upper = slice(0, tk), slice(tk, tq)

    def scores(ch, j, cols):
        return jnp.dot(ch.k_ref[0, j, :, ch.k_lanes], ch.qt_ref[0, ch.q_rows, cols],
                       preferred_element_type=jnp.float32)

    def stage_b(s_all, slot, cols, first=False):
        for c, s_t in enumerate(s_all):
            if first:
                m_new = jnp.max(s_t, axis=0, keepdims=True)
            else:
                m_old = m_ref[c, :, cols]
                m_new = jnp.maximum(m_old, jnp.max(s_t, axis=0, keepdims=True))
                al_ref[slot, c, :, cols] = jnp.exp2(m_old - m_new)
            m_ref[c, :, cols] = m_new
            p_ref[slot, c, :, cols] = jnp.exp2(s_t - m_new).astype(p_ref.dtype)

    ones = jnp.ones((SUM_ROWS, tk), p_ref.dtype)

    def pv_update(c, ch, j, slot, cols):
        vt_ones = jnp.concatenate([ch.vt_ref[0, j, ch.v_rows, :], ones], axis=0)
        pv = jnp.dot(vt_ones, p_ref[slot, c, :, cols], preferred_element_type=jnp.float32)
        acc_ref[ch.acc_rows, cols] = al_ref[slot, c, :, cols] * acc_ref[ch.acc_rows, cols] + pv

    def stage_ac(j, cols, prev_tile, prev_slot):
        s_all = []
        for c, ch in enumerate(chains):
            s_all.append(scores(ch, j, cols))
            pv_update(c, ch, prev_tile, prev_slot, full)
        return s_all

    full = slice(0, tq)
    d0, d1 = 2 * qi, 2 * qi + 1
    acc_ref[...] = jnp.zeros_like(acc_ref)
    al_ref[0] = jnp.ones(al_ref.shape[1:], al_ref.dtype)
    s_all = [scores(ch, d0, full) for ch in chains]
    stage_b([jnp.concatenate([_causal_mask(s_t[:, lower], tk, tk), s_t[:, upper]], axis=1)
             for s_t in s_all], 0, full, first=True)

    def two_steps(k, carry):
        stage_b(stage_ac(2 * k, full, jnp.where(k == 0, d0, 2 * k - 1), 0), 1, full)
        stage_b(stage_ac(2 * k + 1, full, 2 * k, 1), 0, full)
        return carry

    lax.fori_loop(0, qi, two_steps, 0)
    s_all = stage_ac(d1, upper, jnp.where(qi == 0, d0, 2 * qi - 1), 0)
    stage_b([_causal_mask(s_t, tk, tk) for s_t in s_all], 1, upper)
    for c, ch in enumerate(chains):
        pv_update(c, ch, d1, 1, upper)


def _attn_kernel(lq1_ref, lk1_ref, lq2_ref, lk2_ref, g_sub_ref,
                 qt_ref, k_ref, vt_ref, dqt_ref, dk_ref, dvt_ref, oa_ref, ob_ref,
                 p_ref, al_ref, m_ref, acc_ref, *, tq, tk, lam_init):
    rows_a, rows_b = MLA_V + SUM_ROWS, DIFF_V + SUM_ROWS
    base_b = MLA_HEADS * rows_a
    lane_group = lambda i: slice(i * LANES, (i + 1) * LANES)
    chains = [_Chain(k_ref, lane_group(hd), qt_ref, lane_group(hd),
                     vt_ref, slice(hd * MLA_V, (hd + 1) * MLA_V),
                     slice(hd * rows_a, (hd + 1) * rows_a)) for hd in range(MLA_HEADS)]
    chains += [_Chain(dk_ref, lane_group(c // 2), dqt_ref, lane_group(c),
                      dvt_ref, slice((c // 2) * DIFF_V, (c // 2 + 1) * DIFF_V),
                      slice(base_b + c * rows_b, base_b + (c + 1) * rows_b))
               for c in range(2 * DIFF_HEADS)]
    _pipelined_sweep(chains, pl.program_id(1), tq, tk, p_ref, al_ref, m_ref, acc_ref)

    def normalised(ch, dv):
        a0 = ch.acc_rows.start
        return acc_ref[a0:a0 + dv, :] / acc_ref[a0 + dv:a0 + dv + 1, :]

    for hd in range(MLA_HEADS):
        oa_ref[0, hd * MLA_V:(hd + 1) * MLA_V, :] = normalised(chains[hd], MLA_V).astype(oa_ref.dtype)

    lam = (jnp.exp(jnp.sum(lq1_ref[...] * lk1_ref[...], axis=-1, keepdims=True))
           - jnp.exp(jnp.sum(lq2_ref[...] * lk2_ref[...], axis=-1, keepdims=True))
           + lam_init)
    for hd in range(DIFF_HEADS):
        c = MLA_HEADS + 2 * hd
        o = normalised(chains[c], DIFF_V) - lam * normalised(chains[c + 1], DIFF_V)
        ms = jnp.mean(o * o, axis=0, keepdims=True)
        o = o * lax.rsqrt(ms + EPS) * g_sub_ref[...] * (1.0 - lam_init)
        ob_ref[0, hd * DIFF_V:(hd + 1) * DIFF_V, :] = o.astype(ob_ref.dtype)


def _ffn_kernel(x_ref, oa_ref, ob_ref, w_out_ref, g_ffn_ref, w_gate_ref, w_up_ref,
                conv_w_ref, conv_b_ref, w_down_ref, out_ref, prev_ref, y_ref):
    assert CONV_WIDTH == 3
    si = pl.program_id(1)
    tm = x_ref.shape[1]
    contract0 = (((0,), (0,)), ((), ()))
    n_a = oa_ref.shape[1]
    mix = (lax.dot_general(oa_ref[0], w_out_ref[:n_a, :], contract0,
                           preferred_element_type=jnp.float32)
           + lax.dot_general(ob_ref[0], w_out_ref[n_a:, :], contract0,
                             preferred_element_type=jnp.float32))
    x1 = x_ref[0] + mix
    h = _rms_rows(x1, g_ffn_ref[...]).astype(jnp.bfloat16)

    @pl.when(si == 0)
    def _():
        prev_ref[...] = jnp.zeros_like(prev_ref)

    for c0 in range(0, D_FF, FF_CHUNK):
        cw = min(FF_CHUNK, D_FF - c0)
        g = jnp.dot(h, w_gate_ref[:, c0:c0 + cw], preferred_element_type=jnp.float32)
        u = jnp.dot(h, w_up_ref[:, c0:c0 + cw], preferred_element_type=jnp.float32)
        row = lax.broadcasted_iota(jnp.int32, (tm, cw), 0)
        p1 = prev_ref[SUBLANES - 1:SUBLANES, c0:c0 + cw]
        p2 = prev_ref[SUBLANES - 2:SUBLANES - 1, c0:c0 + cw]
        g1 = jnp.where(row == 0, p1, pltpu.roll(g, 1, axis=0))
        g2 = jnp.where(row == 0, p2, jnp.where(row == 1, p1, pltpu.roll(g, 2, axis=0)))
        prev_ref[:, c0:c0 + cw] = g[tm - SUBLANES:tm, :]
        cg = (conv_b_ref[:, c0:c0 + cw] + g2 * conv_w_ref[0:1, c0:c0 + cw]
              + g1 * conv_w_ref[1:2, c0:c0 + cw] + g * conv_w_ref[2:3, c0:c0 + cw])
        y_ref[:, c0:c0 + cw] = (jax.nn.silu(cg) * u).astype(y_ref.dtype)

    out_ref[0] = x1 + jnp.dot(y_ref[...], w_down_ref[...], preferred_element_type=jnp.float32)


def _rope_tables(seq):
    pos = jnp.arange(seq, dtype=jnp.float32)[:, None]

    def tables(dim):
        inv = 1.0 / (ROPE_THETA ** (jnp.arange(0, dim, 2, dtype=jnp.float32) / dim))
        ang = pos * inv[None, :]
        return jnp.cos(ang), jnp.sin(ang)

    ca, sa = tables(MLA_ROPE)
    cb, sb = tables(DIFF_D)
    return ca.T, sa.T, cb.T, sb.T


def _const_spec(shape):
    return pl.BlockSpec(shape, lambda *_: (0,) * len(shape))


def kernel(x, attn_norm_g, w_in, q_a_norm_g, w_q_up, kv_a_norm_g, w_kv_up, mla_q_norm_g,
           mla_k_norm_g, diff_q_norm_g, diff_k_norm_g, lambda_q1, lambda_k1, lambda_q2, lambda_k2,
           diff_subln_g, w_out, ffn_norm_g, w_gate, w_up, conv_w, conv_b, w_down):
    B, S, _ = x.shape
    depth = w_in.shape[0]
    bf = jnp.bfloat16
    cos_a, sin_a, cos_b, sin_b = _rope_tables(S)

    tk, tq = KEY_TILE, 2 * KEY_TILE
    tm_proj, tm_ffn = PROJ_TOKENS, FFN_TOKENS
    assert tm_proj % tk == 0 and S % tq == 0 and S % tm_proj == 0 and S % tm_ffn == 0
    nk = S // tk

    for l in range(depth):
        w_in_t = w_in[l].T.astype(bf)
        w_q_t = w_q_up[l].T.astype(bf)
        wkv_t = w_kv_up[l].T.reshape(MLA_HEADS, MLA_NOPE + MLA_V, KV_RANK)
        w_k_t = wkv_t[:, :MLA_NOPE].reshape(MLA_HEADS * MLA_NOPE, KV_RANK).astype(bf)
        w_v_t = wkv_t[:, MLA_NOPE:].reshape(MLA_HEADS * MLA_V, KV_RANK).astype(bf)

        n_tok_tiles = S // tm_proj
        tok3 = lambda b, s: (b, s, 0)
        feat3 = lambda b, s: (b, 0, s)
        blk4 = lambda b, s: (b, s, 0, 0)
        rope_a_spec = pl.BlockSpec((MLA_ROPE // 2, tm_proj), lambda b, s: (0, s))
        rope_b_spec = pl.BlockSpec((DIFF_D // 2, tm_proj), lambda b, s: (0, s))
        cast_weights = (w_gate[l], w_up[l], w_down[l], w_out[l])
        n_steps = B * n_tok_tiles
        assert all(w.shape[0] % (n_steps * 2 * SUBLANES) == 0 for w in cast_weights)
        cast_specs = [pl.BlockSpec((w.shape[0] // n_steps, w.shape[1]),
                                   lambda b, s: (b * n_tok_tiles + s, 0)) for w in cast_weights]
        qt, k_a, vt, dqt, dk, dvt, w_gate_bf, w_up_bf, w_down_bf, w_out_bf = pl.pallas_call(
            _proj_kernel,
            grid=(B, n_tok_tiles),
            in_specs=[
                pl.BlockSpec((1, tm_proj, D_MODEL), tok3),
                _const_spec((1, D_MODEL)),
                _const_spec((IN_COLS, D_MODEL)),
                _const_spec((Q_RANK, 1)),
                _const_spec((MLA_HEADS * MLA_QK, Q_RANK)),
                _const_spec((KV_RANK, 1)),
                _const_spec((MLA_HEADS * MLA_NOPE, KV_RANK)),
                _const_spec((MLA_HEADS * MLA_V, KV_RANK)),
                _const_spec((MLA_QK, 1)), _const_spec((MLA_QK, 1)),
                _const_spec((DIFF_D, 1)), _const_spec((DIFF_D, 1)),
                rope_a_spec, rope_a_spec, rope_b_spec, rope_b_spec,
                *cast_specs,
            ],
            out_specs=[
                pl.BlockSpec((1, MLA_HEADS * LANES, tm_proj), feat3),
                pl.BlockSpec((1, tm_proj, MLA_HEADS * LANES), tok3),
                pl.BlockSpec((1, tm_proj // tk, MLA_HEADS * MLA_V, tk), blk4),
                pl.BlockSpec((1, DIFF_HEADS * 2 * LANES, tm_proj), feat3),
                pl.BlockSpec((1, tm_proj, DIFF_HEADS * LANES), tok3),
                pl.BlockSpec((1, tm_proj // tk, DIFF_HEADS * DIFF_V, tk), blk4),
                *cast_specs,
            ],
            out_shape=[
                jax.ShapeDtypeStruct((B, MLA_HEADS * LANES, S), bf),
                jax.ShapeDtypeStruct((B, S, MLA_HEADS * LANES), bf),
                jax.ShapeDtypeStruct((B, nk, MLA_HEADS * MLA_V, tk), bf),
                jax.ShapeDtypeStruct((B, DIFF_HEADS * 2 * LANES, S), bf),
                jax.ShapeDtypeStruct((B, S, DIFF_HEADS * LANES), bf),
                jax.ShapeDtypeStruct((B, nk, DIFF_HEADS * DIFF_V, tk), bf),
                *[jax.ShapeDtypeStruct(w.shape, bf) for w in cast_weights],
            ],
            compiler_params=pltpu.CompilerParams(
                dimension_semantics=("arbitrary", "arbitrary"), vmem_limit_bytes=VMEM_LIMIT),
            name="proj",
        )(x, attn_norm_g[l].reshape(1, -1), w_in_t, q_a_norm_g[l].reshape(-1, 1), w_q_t,
          kv_a_norm_g[l].reshape(-1, 1), w_k_t, w_v_t,
          mla_q_norm_g[l].reshape(-1, 1), mla_k_norm_g[l].reshape(-1, 1),
          diff_q_norm_g[l].reshape(-1, 1), diff_k_norm_g[l].reshape(-1, 1),
          cos_a, sin_a, cos_b, sin_b, *cast_weights)

        nq = S // tq
        n_chains = MLA_HEADS + 2 * DIFF_HEADS
        lam_init = 0.8 - 0.6 * math.exp(-0.3 * l)
        lam_spec = _const_spec((1, DIFF_D))
        q_tile = lambda b, i: (b, 0, i)
        per_batch = lambda b, i: (b, 0, 0, 0)
        o_a, o_b = pl.pallas_call(
            functools.partial(_attn_kernel, tq=tq, tk=tk, lam_init=lam_init),
            grid=(B, nq),
            in_specs=[
                lam_spec, lam_spec, lam_spec, lam_spec,
                _const_spec((DIFF_V, 1)),
                pl.BlockSpec((1, MLA_HEADS * LANES, tq), q_tile),
                pl.BlockSpec((1, nk, tk, MLA_HEADS * LANES), per_batch),
                pl.BlockSpec((1, nk, MLA_HEADS * MLA_V, tk), per_batch),
                pl.BlockSpec((1, DIFF_HEADS * 2 * LANES, tq), q_tile),
                pl.BlockSpec((1, nk, tk, DIFF_HEADS * LANES), per_batch),
                pl.BlockSpec((1, nk, DIFF_HEADS * DIFF_V, tk), per_batch),
            ],
            out_specs=[pl.BlockSpec((1, MLA_HEADS * MLA_V, tq), q_tile),
                       pl.BlockSpec((1, DIFF_HEADS * DIFF_V, tq), q_tile)],
            out_shape=[jax.ShapeDtypeStruct((B, MLA_HEADS * MLA_V, S), bf),
                       jax.ShapeDtypeStruct((B, DIFF_HEADS * DIFF_V, S), bf)],
            scratch_shapes=[
                pltpu.VMEM((2, n_chains, tk, tq), bf),
                pltpu.VMEM((2, n_chains, 1, tq), jnp.float32),
                pltpu.VMEM((n_chains, 1, tq), jnp.float32),
                pltpu.VMEM((MLA_HEADS * (MLA_V + SUM_ROWS)
                            + 2 * DIFF_HEADS * (DIFF_V + SUM_ROWS), tq), jnp.float32)],
            compiler_params=pltpu.CompilerParams(
                dimension_semantics=("arbitrary", "arbitrary"), vmem_limit_bytes=VMEM_LIMIT),
            name="attn",
        )(lambda_q1[l].reshape(1, -1), lambda_k1[l].reshape(1, -1),
          lambda_q2[l].reshape(1, -1), lambda_k2[l].reshape(1, -1),
          diff_subln_g[l].reshape(-1, 1),
          qt, k_a.reshape(B, nk, tk, MLA_HEADS * LANES), vt,
          dqt, dk.reshape(B, nk, tk, DIFF_HEADS * LANES), dvt)

        n_a = MLA_HEADS * MLA_V
        x = pl.pallas_call(
            _ffn_kernel,
            grid=(B, S // tm_ffn),
            in_specs=[
                pl.BlockSpec((1, tm_ffn, D_MODEL), lambda b, s: (b, s, 0)),
                pl.BlockSpec((1, n_a, tm_ffn), lambda b, s: (b, 0, s)),
                pl.BlockSpec((1, D_MODEL - n_a, tm_ffn), lambda b, s: (b, 0, s)),
                _const_spec((D_MODEL, D_MODEL)),
                _const_spec((1, D_MODEL)),
                _const_spec((D_MODEL, D_FF)),
                _const_spec((D_MODEL, D_FF)),
                _const_spec((CONV_WIDTH, D_FF)),
                _const_spec((1, D_FF)),
                _const_spec((D_FF, D_MODEL)),
            ],
            out_specs=pl.BlockSpec((1, tm_ffn, D_MODEL), lambda b, s: (b, s, 0)),
            out_shape=jax.ShapeDtypeStruct((B, S, D_MODEL), x.dtype),
            scratch_shapes=[pltpu.VMEM((SUBLANES, D_FF), jnp.float32),
                            pltpu.VMEM((tm_ffn, D_FF), bf)],
            compiler_params=pltpu.CompilerParams(
                dimension_semantics=("arbitrary", "arbitrary"), vmem_limit_bytes=VMEM_LIMIT),
            name="ffn",
        )(x, o_a, o_b, w_out_bf, ffn_norm_g[l].reshape(1, -1), w_gate_bf, w_up_bf,
          conv_w[l], conv_b[l].reshape(1, -1), w_down_bf)
    return x
```

```python
import functools
import math
from typing import Any, NamedTuple

import jax
import jax.numpy as jnp
from jax import lax
from jax.experimental import pallas as pl
from jax.experimental.pallas import tpu as pltpu

D_MODEL = 1024
MLA_HEADS = 8
MLA_NOPE = 64
MLA_ROPE = 32
MLA_V = 64
MLA_QK = MLA_NOPE + MLA_ROPE
Q_RANK = 384
KV_RANK = 256
DIFF_HEADS = 4
DIFF_D = 64
DIFF_V = 2 * DIFF_D
D_FF = 2816
CONV_WIDTH = 3
ROPE_THETA = 10000.0
EPS = 1e-6
LANES = 128
SUBLANES = 8
SUM_ROWS = 16
LOG2E = math.log2(math.e)
R_Q = 0
R_KV = R_Q + Q_RANK
R_KPE = R_KV + KV_RANK
R_DQ = R_KPE + MLA_ROPE
R_DK = R_DQ + DIFF_HEADS * 2 * DIFF_D
R_DV = R_DK + DIFF_HEADS * 2 * DIFF_D
IN_COLS = R_DV + DIFF_HEADS * DIFF_V

KEY_TILE = 256
PROJ_TOKENS = 1024
FFN_TOKENS = 1024
FF_CHUNK = 1024
V7X_VMEM_BYTES = 64 * 1024 * 1024
VMEM_LIMIT = V7X_VMEM_BYTES - 8 * 1024 * 1024


def _rms_rows(x, g):
    ms = jnp.mean(x * x, axis=-1, keepdims=True)
    return x * lax.rsqrt(ms + EPS) * g


def _rms_cols(xt, n, scale):
    ms = jnp.sum(xt * xt, axis=0, keepdims=True) * (1.0 / n)
    return scale * lax.rsqrt(scale * scale * ms + EPS)


def _rope_cols(xt, cos, sin):
    half = xt.shape[0] // 2
    x1, x2 = xt[:half], xt[half:]
    return x1 * cos - x2 * sin, x2 * cos + x1 * sin


def _proj_kernel(x_ref, g_attn_ref, w_in_ref, g_qa_ref, w_q_ref, g_kva_ref, w_k_ref, w_v_ref,
                 g_q_ref, g_k_ref, g_dq_ref, g_dk_ref,
                 cos_a_ref, sin_a_ref, cos_b_ref, sin_b_ref,
                 w_gate_ref, w_up_ref, w_down_ref, w_out_ref,
                 qt_ref, k_ref, vt_ref, dqt_ref, dk_ref, dvt_ref,
                 w_gate_bf_ref, w_up_bf_ref, w_down_bf_ref, w_out_bf_ref):
    tm = x_ref.shape[1]
    bf = jnp.bfloat16
    for src_ref, dst_ref in ((w_gate_ref, w_gate_bf_ref), (w_up_ref, w_up_bf_ref),
                             (w_down_ref, w_down_bf_ref), (w_out_ref, w_out_bf_ref)):
        dst_ref[...] = src_ref[...].astype(bf)
    x = x_ref[0]
    r_tok = lax.rsqrt(jnp.mean(x * x, axis=-1, keepdims=True) + EPS)
    r_tok = jnp.transpose(jnp.broadcast_to(r_tok, (tm, LANES)))[0:1, :]
    h = (x * g_attn_ref[...]).astype(bf)

    def in_proj(r0, r1):
        return lax.dot_general(w_in_ref[r0:r1, :], h, (((1,), (1,)), ((), ())),
                               preferred_element_type=jnp.float32)

    cos_a, sin_a = cos_a_ref[...], sin_a_ref[...]
    cos_b, sin_b = cos_b_ref[...], sin_b_ref[...]
    zeros_pad = jnp.zeros((LANES - MLA_QK, tm), jnp.float32)
    one = jnp.ones((1, tm), jnp.float32)

    def head_a(nope, pe_roped, g):
        nope = nope * _rms_cols(nope, MLA_NOPE, one) * g[:MLA_NOPE]
        return jnp.concatenate([nope, *pe_roped, zeros_pad], axis=0)

    def rope_a(pe, g, scale):
        pe = pe * _rms_cols(pe, MLA_ROPE, scale) * g[MLA_NOPE:]
        return _rope_cols(pe, cos_a, sin_a)

    def head_b(xt, g):
        xt = xt * _rms_cols(xt, DIFF_D, r_tok) * g
        return jnp.concatenate(_rope_cols(xt, cos_b, sin_b), axis=0)

    cq = in_proj(R_Q, R_KV)
    lat = in_proj(R_KV, R_DQ)
    dq = in_proj(R_DQ, R_DK)

    cq = (cq * _rms_cols(cq, Q_RANK, r_tok) * g_qa_ref[...]).astype(bf)
    q = jnp.dot(w_q_ref[...], cq, preferred_element_type=jnp.float32)
    ckv = lat[:KV_RANK]
    ckv = (ckv * _rms_cols(ckv, KV_RANK, r_tok) * g_kva_ref[...]).astype(bf)
    kn = jnp.dot(w_k_ref[...], ckv, preferred_element_type=jnp.float32)
    dk = in_proj(R_DK, R_DV)
    v = jnp.dot(w_v_ref[...], ckv, preferred_element_type=jnp.float32).astype(bf)
    dv = (in_proj(R_DV, IN_COLS) * r_tok).astype(bf)
    tk = vt_ref.shape[3]
    for t in range(tm // tk):
        vt_ref[0, t] = v[:, t * tk:(t + 1) * tk]
        dvt_ref[0, t] = dv[:, t * tk:(t + 1) * tk]

    g_q = g_q_ref[...] * (MLA_QK ** -0.5 * LOG2E)
    for hd in range(MLA_HEADS):
        r0 = hd * MLA_QK
        qt_ref[0, hd * LANES:(hd + 1) * LANES, :] = head_a(
            q[r0:r0 + MLA_NOPE], rope_a(q[r0 + MLA_NOPE:r0 + MLA_QK], g_q, one), g_q).astype(bf)

    g_k = g_k_ref[...]
    kpe = rope_a(lat[KV_RANK:], g_k, r_tok)
    for hd in range(MLA_HEADS):
        kt = head_a(kn[hd * MLA_NOPE:(hd + 1) * MLA_NOPE], kpe, g_k)
        k_ref[0, :, hd * LANES:(hd + 1) * LANES] = kt.T.astype(bf)

    g_dq = g_dq_ref[...] * (DIFF_D ** -0.5 * LOG2E)
    g_dk = g_dk_ref[...]
    zeros_half = jnp.zeros((DIFF_D, tm), bf)
    for hd in range(DIFF_HEADS):
        r0 = hd * 2 * DIFF_D
        q1 = head_b(dq[r0:r0 + DIFF_D], g_dq).astype(bf)
        q2 = head_b(dq[r0 + DIFF_D:r0 + 2 * DIFF_D], g_dq).astype(bf)
        b0 = 2 * hd * LANES
        dqt_ref[0, b0:b0 + LANES, :] = jnp.concatenate([q1, zeros_half], axis=0)
        dqt_ref[0, b0 + LANES:b0 + 2 * LANES, :] = jnp.concatenate([zeros_half, q2], axis=0)
        dkt = jnp.concatenate([head_b(dk[r0:r0 + DIFF_D], g_dk),
                               head_b(dk[r0 + DIFF_D:r0 + 2 * DIFF_D], g_dk)], axis=0)
        dk_ref[0, :, hd * LANES:(hd + 1) * LANES] = dkt.T.astype(bf)


def _causal_mask(s_t, tk, tq):
    key = lax.broadcasted_iota(jnp.int32, (tk, tq), 0)
    qry = lax.broadcasted_iota(jnp.int32, (tk, tq), 1)
    return jnp.where(key <= qry, s_t, -jnp.inf)


class _Chain(NamedTuple):
    k_ref: Any
    k_lanes: slice
    qt_ref: Any
    q_rows: slice
    vt_ref: Any
    v_rows: slice
    acc_rows: slice


def _pipelined_sweep(chains, qi, tq, tk, p_ref, al_ref, m_ref, l_ref, acc_ref):
    assert tq == 2 * tk
    lower, upper = slice(0, tk), slice(tk, tq)

    def scores(ch, j, cols):
        return jnp.dot(ch.k_ref[0, j, :, ch.k_lanes], ch.qt_ref[0, ch.q_rows, cols],
                       preferred_element_type=jnp.float32)

    def stage_b(s_all, slot, cols, first=False):
        for c, s_t in enumerate(s_all):
            if first:
                m_new = jnp.max(s_t, axis=0, keepdims=True)
                p = jnp.exp2(s_t - m_new)
                l_ref[c, :, cols] = jnp.sum(p, axis=0, keepdims=True)
            else:
                m_old = m_ref[c, :, cols]
                m_new = jnp.maximum(m_old, jnp.max(s_t, axis=0, keepdims=True))
                alpha = jnp.exp2(m_old - m_new)
                al_ref[slot, c, :, cols] = alpha
                p = jnp.exp2(s_t - m_new)
                l_ref[c, :, cols] = alpha * l_ref[c, :, cols] + jnp.sum(p, axis=0, keepdims=True)
            m_ref[c, :, cols] = m_new
            p_ref[slot, c, :, cols] = p.astype(p_ref.dtype)

    def pv_update(c, ch, j, slot, cols):
        pv = jnp.dot(ch.vt_ref[0, j, ch.v_rows, :], p_ref[slot, c, :, cols],
                     preferred_element_type=jnp.float32)
        acc_ref[ch.acc_rows, cols] = al_ref[slot, c, :, cols] * acc_ref[ch.acc_rows, cols] + pv

    def stage_ac(j, cols, prev_tile, prev_slot):
        s_all = []
        for c, ch in enumerate(chains):
            s_all.append(scores(ch, j, cols))
            pv_update(c, ch, prev_tile, prev_slot, full)
        return s_all

    full = slice(0, tq)
    d0, d1 = 2 * qi, 2 * qi + 1
    acc_ref[...] = jnp.zeros_like(acc_ref)
    al_ref[0] = jnp.ones(al_ref.shape[1:], al_ref.dtype)
    s_all = [scores(ch, d0, full) for ch in chains]
    stage_b([jnp.concatenate([_causal_mask(s_t[:, lower], tk, tk), s_t[:, upper]], axis=1)
             for s_t in s_all], 0, full, first=True)

    def two_steps(k, carry):
        stage_b(stage_ac(2 * k, full, jnp.where(k == 0, d0, 2 * k - 1), 0), 1, full)
        stage_b(stage_ac(2 * k + 1, full, 2 * k, 1), 0, full)
        return carry

    lax.fori_loop(0, qi, two_steps, 0)
    s_all = stage_ac(d1, upper, jnp.where(qi == 0, d0, 2 * qi - 1), 0)
    stage_b([_causal_mask(s_t, tk, tk) for s_t in s_all], 1, upper)
    for c, ch in enumerate(chains):
        pv_update(c, ch, d1, 1, upper)


def _attn_kernel(lq1_ref, lk1_ref, lq2_ref, lk2_ref, g_sub_ref,
                 qt_ref, k_ref, vt_ref, dqt_ref, dk_ref, dvt_ref, oa_ref, ob_ref,
                 p_ref, al_ref, m_ref, l_ref, acc_ref, *, tq, tk, lam_init):
    rows_a, rows_b = MLA_V, DIFF_V
    base_b = MLA_HEADS * rows_a
    lane_group = lambda i: slice(i * LANES, (i + 1) * LANES)
    chains = [_Chain(k_ref, lane_group(hd), qt_ref, lane_group(hd),
                     vt_ref, slice(hd * MLA_V, (hd + 1) * MLA_V),
                     slice(hd * rows_a, (hd + 1) * rows_a)) for hd in range(MLA_HEADS)]
    chains += [_Chain(dk_ref, lane_group(c // 2), dqt_ref, lane_group(c),
                      dvt_ref, slice((c // 2) * DIFF_V, (c // 2 + 1) * DIFF_V),
                      slice(base_b + c * rows_b, base_b + (c + 1) * rows_b))
               for c in range(2 * DIFF_HEADS)]
    _pipelined_sweep(chains, pl.program_id(1), tq, tk, p_ref, al_ref, m_ref, l_ref, acc_ref)

    def normalised(c):
        return acc_ref[chains[c].acc_rows, :] / l_ref[c]

    for hd in range(MLA_HEADS):
        oa_ref[0, hd * MLA_V:(hd + 1) * MLA_V, :] = normalised(hd).astype(oa_ref.dtype)

    lam = (jnp.exp(jnp.sum(lq1_ref[...] * lk1_ref[...], axis=-1, keepdims=True))
           - jnp.exp(jnp.sum(lq2_ref[...] * lk2_ref[...], axis=-1, keepdims=True))
           + lam_init)
    for hd in range(DIFF_HEADS):
        c = MLA_HEADS + 2 * hd
        o = normalised(c) - lam * normalised(c + 1)
        ms = jnp.mean(o * o, axis=0, keepdims=True)
        o = o * lax.rsqrt(ms + EPS) * g_sub_ref[...] * (1.0 - lam_init)
        ob_ref[0, hd * DIFF_V:(hd + 1) * DIFF_V, :] = o.astype(ob_ref.dtype)


def _ffn_kernel(x_ref, oa_ref, ob_ref, w_out_ref, g_ffn_ref, w_gate_ref, w_up_ref,
                conv_w_ref, conv_b_ref, w_down_ref, out_ref, prev_ref, y_ref):
    assert CONV_WIDTH == 3
    si = pl.program_id(1)
    tm = x_ref.shape[1]
    contract0 = (((0,), (0,)), ((), ()))
    n_a = oa_ref.shape[1]
    mix = (lax.dot_general(oa_ref[0], w_out_ref[:n_a, :], contract0,
                           preferred_element_type=jnp.float32)
           + lax.dot_general(ob_ref[0], w_out_ref[n_a:, :], contract0,
                             preferred_element_type=jnp.float32))
    x1 = x_ref[0] + mix
    h = _rms_rows(x1, g_ffn_ref[...]).astype(jnp.bfloat16)

    @pl.when(si == 0)
    def _():
        prev_ref[...] = jnp.zeros_like(prev_ref)

    for c0 in range(0, D_FF, FF_CHUNK):
        cw = min(FF_CHUNK, D_FF - c0)
        g = jnp.dot(h, w_gate_ref[:, c0:c0 + cw], preferred_element_type=jnp.float32)
        u = jnp.dot(h, w_up_ref[:, c0:c0 + cw], preferred_element_type=jnp.float32)
        row = lax.broadcasted_iota(jnp.int32, (tm, cw), 0)
        p1 = prev_ref[SUBLANES - 1:SUBLANES, c0:c0 + cw]
        p2 = prev_ref[SUBLANES - 2:SUBLANES - 1, c0:c0 + cw]
        g1 = jnp.where(row == 0, p1, pltpu.roll(g, 1, axis=0))
        g2 = jnp.where(row == 0, p2, jnp.where(row == 1, p1, pltpu.roll(g, 2, axis=0)))
        prev_ref[:, c0:c0 + cw] = g[tm - SUBLANES:tm, :]
        cg = (conv_b_ref[:, c0:c0 + cw] + g2 * conv_w_ref[0:1, c0:c0 + cw]
              + g1 * conv_w_ref[1:2, c0:c0 + cw] + g * conv_w_ref[2:3, c0:c0 + cw])
        y_ref[:, c0:c0 + cw] = (jax.nn.silu(cg) * u).astype(y_ref.dtype)

    out_ref[0] = x1 + jnp.dot(y_ref[...], w_down_ref[...], preferred_element_type=jnp.float32)


def _rope_tables(seq):
    pos = jnp.arange(seq, dtype=jnp.float32)[:, None]

    def tables(dim):
        inv = 1.0 / (ROPE_THETA ** (jnp.arange(0, dim, 2, dtype=jnp.float32) / dim))
        ang = pos * inv[None, :]
        return jnp.cos(ang), jnp.sin(ang)

    ca, sa = tables(MLA_ROPE)
    cb, sb = tables(DIFF_D)
    return ca.T, sa.T, cb.T, sb.T


def _const_spec(shape):
    return pl.BlockSpec(shape, lambda *_: (0,) * len(shape))


def kernel(x, attn_norm_g, w_in, q_a_norm_g, w_q_up, kv_a_norm_g, w_kv_up, mla_q_norm_g,
           mla_k_norm_g, diff_q_norm_g, diff_k_norm_g, lambda_q1, lambda_k1, lambda_q2, lambda_k2,
           diff_subln_g, w_out, ffn_norm_g, w_gate, w_up, conv_w, conv_b, w_down):
    B, S, _ = x.shape
    depth = w_in.shape[0]
    bf = jnp.bfloat16
    cos_a, sin_a, cos_b, sin_b = _rope_tables(S)

    tk, tq = KEY_TILE, 2 * KEY_TILE
    tm_proj, tm_ffn = PROJ_TOKENS, FFN_TOKENS
    assert tm_proj % tk == 0 and S % tq == 0 and S % tm_proj == 0 and S % tm_ffn == 0
    nk = S // tk

    for l in range(depth):
        w_in_t = w_in[l].T.astype(bf)
        w_q_t = w_q_up[l].T.astype(bf)
        wkv_t = w_kv_up[l].T.reshape(MLA_HEADS, MLA_NOPE + MLA_V, KV_RANK)
        w_k_t = wkv_t[:, :MLA_NOPE].reshape(MLA_HEADS * MLA_NOPE, KV_RANK).astype(bf)
        w_v_t = wkv_t[:, MLA_NOPE:].reshape(MLA_HEADS * MLA_V, KV_RANK).astype(bf)

        n_tok_tiles = S // tm_proj
        tok3 = lambda b, s: (b, s, 0)
        feat3 = lambda b, s: (b, 0, s)
        blk4 = lambda b, s: (b, s, 0, 0)
        rope_a_spec = pl.BlockSpec((MLA_ROPE // 2, tm_proj), lambda b, s: (0, s))
        rope_b_spec = pl.BlockSpec((DIFF_D // 2, tm_proj), lambda b, s: (0, s))
        cast_weights = (w_gate[l], w_up[l], w_down[l], w_out[l])
        n_steps = B * n_tok_tiles
        assert all(w.shape[0] % (n_steps * 2 * SUBLANES) == 0 for w in cast_weights)
        cast_specs = [pl.BlockSpec((w.shape[0] // n_steps, w.shape[1]),
                                   lambda b, s: (b * n_tok_tiles + s, 0)) for w in cast_weights]
        qt, k_a, vt, dqt, dk, dvt, w_gate_bf, w_up_bf, w_down_bf, w_out_bf = pl.pallas_call(
            _proj_kernel,
            grid=(B, n_tok_tiles),
            in_specs=[
                pl.BlockSpec((1, tm_proj, D_MODEL), tok3),
                _const_spec((1, D_MODEL)),
                _const_spec((IN_COLS, D_MODEL)),
                _const_spec((Q_RANK, 1)),
                _const_spec((MLA_HEADS * MLA_QK, Q_RANK)),
                _const_spec((KV_RANK, 1)),
                _const_spec((MLA_HEADS * MLA_NOPE, KV_RANK)),
                _const_spec((MLA_HEADS * MLA_V, KV_RANK)),
                _const_spec((MLA_QK, 1)), _const_spec((MLA_QK, 1)),
                _const_spec((DIFF_D, 1)), _const_spec((DIFF_D, 1)),
                rope_a_spec, rope_a_spec, rope_b_spec, rope_b_spec,
                *cast_specs,
            ],
            out_specs=[
                pl.BlockSpec((1, MLA_HEADS * LANES, tm_proj), feat3),
                pl.BlockSpec((1, tm_proj, MLA_HEADS * LANES), tok3),
                pl.BlockSpec((1, tm_proj // tk, MLA_HEADS * MLA_V, tk), blk4),
                pl.BlockSpec((1, DIFF_HEADS * 2 * LANES, tm_proj), feat3),
                pl.BlockSpec((1, tm_proj, DIFF_HEADS * LANES), tok3),
                pl.BlockSpec((1, tm_proj // tk, DIFF_HEADS * DIFF_V, tk), blk4),
                *cast_specs,
            ],
            out_shape=[
                jax.ShapeDtypeStruct((B, MLA_HEADS * LANES, S), bf),
                jax.ShapeDtypeStruct((B, S, MLA_HEADS * LANES), bf),
                jax.ShapeDtypeStruct((B, nk, MLA_HEADS * MLA_V, tk), bf),
                jax.ShapeDtypeStruct((B, DIFF_HEADS * 2 * LANES, S), bf),
                jax.ShapeDtypeStruct((B, S, DIFF_HEADS * LANES), bf),
                jax.ShapeDtypeStruct((B, nk, DIFF_HEADS * DIFF_V, tk), bf),
                *[jax.ShapeDtypeStruct(w.shape, bf) for w in cast_weights],
            ],
            compiler_params=pltpu.CompilerParams(
                dimension_semantics=("arbitrary", "arbitrary"), vmem_limit_bytes=VMEM_LIMIT),
            name="proj",
        )(x, attn_norm_g[l].reshape(1, -1), w_in_t, q_a_norm_g[l].reshape(-1, 1), w_q_t,
          kv_a_norm_g[l].reshape(-1, 1), w_k_t, w_v_t,
          mla_q_norm_g[l].reshape(-1, 1), mla_k_norm_g[l].reshape(-1, 1),
          diff_q_norm_g[l].reshape(-1, 1), diff_k_norm_g[l].reshape(-1, 1),
          cos_a, sin_a, cos_b, sin_b, *cast_weights)

        nq = S // tq
        n_chains = MLA_HEADS + 2 * DIFF_HEADS
        lam_init = 0.8 - 0.6 * math.exp(-0.3 * l)
        lam_spec = _const_spec((1, DIFF_D))
        q_tile = lambda b, i: (b, 0, i)
        per_batch = lambda b, i: (b, 0, 0, 0)
        o_a, o_b = pl.pallas_call(
            functools.partial(_attn_kernel, tq=tq, tk=tk, lam_init=lam_init),
            grid=(B, nq),
            in_specs=[
                lam_spec, lam_spec, lam_spec, lam_spec,
                _const_spec((DIFF_V, 1)),
                pl.BlockSpec((1, MLA_HEADS * LANES, tq), q_tile),
                pl.BlockSpec((1, nk, tk, MLA_HEADS * LANES), per_batch),
                pl.BlockSpec((1, nk, MLA_HEADS * MLA_V, tk), per_batch),
                pl.BlockSpec((1, DIFF_HEADS * 2 * LANES, tq), q_tile),
                pl.BlockSpec((1, nk, tk, DIFF_HEADS * LANES), per_batch),
                pl.BlockSpec((1, nk, DIFF_HEADS * DIFF_V, tk), per_batch),
            ],
            out_specs=[pl.BlockSpec((1, MLA_HEADS * MLA_V, tq), q_tile),
                       pl.BlockSpec((1, DIFF_HEADS * DIFF_V, tq), q_tile)],
            out_shape=[jax.ShapeDtypeStruct((B, MLA_HEADS * MLA_V, S), bf),
                       jax.ShapeDtypeStruct((B, DIFF_HEADS * DIFF_V, S), bf)],
            scratch_shapes=[
                pltpu.VMEM((2, n_chains, tk, tq), bf),
                pltpu.VMEM((2, n_chains, 1, tq), jnp.float32),
                pltpu.VMEM((n_chains, 1, tq), jnp.float32),
                pltpu.VMEM((n_chains, 1, tq), jnp.float32),
                pltpu.VMEM((MLA_HEADS * MLA_V + 2 * DIFF_HEADS * DIFF_V, tq), jnp.float32)],
            compiler_params=pltpu.CompilerParams(
                dimension_semantics=("arbitrary", "arbitrary"), vmem_limit_bytes=VMEM_LIMIT),
            name="attn",
        )(lambda_q1[l].reshape(1, -1), lambda_k1[l].reshape(1, -1),
          lambda_q2[l].reshape(1, -1), lambda_k2[l].reshape(1, -1),
          diff_subln_g[l].reshape(-1, 1),
          qt, k_a.reshape(B, nk, tk, MLA_HEADS * LANES), vt,
          dqt, dk.reshape(B, nk, tk, DIFF_HEADS * LANES), dvt)

        n_a = MLA_HEADS * MLA_V
        x = pl.pallas_call(
            _ffn_kernel,
            grid=(B, S // tm_ffn),
            in_specs=[
                pl.BlockSpec((1, tm_ffn, D_MODEL), lambda b, s: (b, s, 0)),
                pl.BlockSpec((1, n_a, tm_ffn), lambda b, s: (b, 0, s)),
                pl.BlockSpec((1, D_MODEL - n_a, tm_ffn), lambda b, s: (b, 0, s)),
                _const_spec((D_MODEL, D_MODEL)),
                _const_spec((1, D_MODEL)),
                _const_spec((D_MODEL, D_FF)),
                _const_spec((D_MODEL, D_FF)),
                _const_spec((CONV_WIDTH, D_FF)),
                _const_spec((1, D_FF)),
                _const_spec((D_FF, D_MODEL)),
            ],
            out_specs=pl.BlockSpec((1, tm_ffn, D_MODEL), lambda b, s: (b, s, 0)),
            out_shape=jax.ShapeDtypeStruct((B, S, D_MODEL), x.dtype),
            scratch_shapes=[pltpu.VMEM((SUBLANES, D_FF), jnp.float32),
                            pltpu.VMEM((tm_ffn, D_FF), bf)],
            compiler_params=pltpu.CompilerParams(
                dimension_semantics=("arbitrary", "arbitrary"), vmem_limit_bytes=VMEM_LIMIT),
            name="ffn",
        )(x, o_a, o_b, w_out_bf, ffn_norm_g[l].reshape(1, -1), w_gate_bf, w_up_bf,
          conv_w[l], conv_b[l].reshape(1, -1), w_down_bf)
    return x
```

```python
import functools
import math
from typing import Any, NamedTuple

import jax
import jax.numpy as jnp
from jax import lax
from jax.experimental import pallas as pl
from jax.experimental.pallas import tpu as pltpu

D_MODEL = 1024
MLA_HEADS = 8
MLA_NOPE = 64
MLA_ROPE = 32
MLA_V = 64
MLA_QK = MLA_NOPE + MLA_ROPE
Q_RANK = 384
KV_RANK = 256
DIFF_HEADS = 4
DIFF_D = 64
DIFF_V = 2 * DIFF_D
D_FF = 2816
CONV_WIDTH = 3
ROPE_THETA = 10000.0
EPS = 1e-6
LANES = 128
SUBLANES = 8
SUM_ROWS = 16
LOG2E = math.log2(math.e)
R_Q = 0
R_KV = R_Q + Q_RANK
R_KPE = R_KV + KV_RANK
R_DQ = R_KPE + MLA_ROPE
R_DK = R_DQ + DIFF_HEADS * 2 * DIFF_D
R_DV = R_DK + DIFF_HEADS * 2 * DIFF_D
IN_COLS = R_DV + DIFF_HEADS * DIFF_V

KEY_TILE = 256
PROJ_TOKENS = 1024
FFN_TOKENS = 1024
FF_CHUNK = 1024
V7X_VMEM_BYTES = 64 * 1024 * 1024
VMEM_LIMIT = V7X_VMEM_BYTES - 8 * 1024 * 1024


def _rms_rows(x, g):
    ms = jnp.mean(x * x, axis=-1, keepdims=True)
    return x * lax.rsqrt(ms + EPS) * g


def _rms_cols(xt, n, scale):
    ms = jnp.sum(xt * xt, axis=0, keepdims=True) * (1.0 / n)
    return scale * lax.rsqrt(scale * scale * ms + EPS)


def _rope_cols(xt, cos, sin):
    half = xt.shape[0] // 2
    x1, x2 = xt[:half], xt[half:]
    return x1 * cos - x2 * sin, x2 * cos + x1 * sin


def _proj_kernel(x_ref, g_attn_ref, w_in_ref, g_qa_ref, w_q_ref, g_kva_ref, w_k_ref, w_v_ref,
                 g_q_ref, g_k_ref, g_dq_ref, g_dk_ref,
                 cos_a_ref, sin_a_ref, cos_b_ref, sin_b_ref,
                 w_gate_ref, w_up_ref, w_down_ref, w_out_ref,
                 qt_ref, k_ref, vt_ref, dqt_ref, dk_ref, dvt_ref,
                 w_gate_bf_ref, w_up_bf_ref, w_down_bf_ref, w_out_bf_ref):
    tm = x_ref.shape[1]
    bf = jnp.bfloat16
    for src_ref, dst_ref in ((w_gate_ref, w_gate_bf_ref), (w_up_ref, w_up_bf_ref),
                             (w_down_ref, w_down_bf_ref), (w_out_ref, w_out_bf_ref)):
        dst_ref[...] = src_ref[...].astype(bf)
    x = x_ref[0]
    r_tok = lax.rsqrt(jnp.mean(x * x, axis=-1, keepdims=True) + EPS)
    r_tok = jnp.transpose(jnp.broadcast_to(r_tok, (tm, LANES)))[0:1, :]
    h = (x * g_attn_ref[...]).astype(bf)

    def in_proj(r0, r1):
        return lax.dot_general(w_in_ref[r0:r1, :], h, (((1,), (1,)), ((), ())),
                               preferred_element_type=jnp.float32)

    cos_a, sin_a = cos_a_ref[...], sin_a_ref[...]
    cos_b, sin_b = cos_b_ref[...], sin_b_ref[...]
    zeros_pad = jnp.zeros((LANES - MLA_QK, tm), jnp.float32)
    one = jnp.ones((1, tm), jnp.float32)

    def head_a(nope, pe_roped, g):
        nope = nope * _rms_cols(nope, MLA_NOPE, one) * g[:MLA_NOPE]
        return jnp.concatenate([nope, *pe_roped, zeros_pad], axis=0)

    def rope_a(pe, g, scale):
        pe = pe * _rms_cols(pe, MLA_ROPE, scale) * g[MLA_NOPE:]
        return _rope_cols(pe, cos_a, sin_a)

    def head_b(xt, g):
        xt = xt * _rms_cols(xt, DIFF_D, r_tok) * g
        return jnp.concatenate(_rope_cols(xt, cos_b, sin_b), axis=0)

    cq = in_proj(R_Q, R_KV)
    lat = in_proj(R_KV, R_DQ)
    dq = in_proj(R_DQ, R_DK)

    cq = (cq * _rms_cols(cq, Q_RANK, r_tok) * g_qa_ref[...]).astype(bf)
    q = jnp.dot(w_q_ref[...], cq, preferred_element_type=jnp.float32)
    ckv = lat[:KV_RANK]
    ckv = (ckv * _rms_cols(ckv, KV_RANK, r_tok) * g_kva_ref[...]).astype(bf)
    kn = jnp.dot(w_k_ref[...], ckv, preferred_element_type=jnp.float32)
    dk = in_proj(R_DK, R_DV)
    v = jnp.dot(w_v_ref[...], ckv, preferred_element_type=jnp.float32).astype(bf)
    dv = (in_proj(R_DV, IN_COLS) * r_tok).astype(bf)
    tk = vt_ref.shape[3]
    for t in range(tm // tk):
        vt_ref[0, t] = v[:, t * tk:(t + 1) * tk]
        dvt_ref[0, t] = dv[:, t * tk:(t + 1) * tk]

    g_q = g_q_ref[...] * (MLA_QK ** -0.5 * LOG2E)
    for hd in range(MLA_HEADS):
        r0 = hd * MLA_QK
        qt_ref[0, hd * LANES:(hd + 1) * LANES, :] = head_a(
            q[r0:r0 + MLA_NOPE], rope_a(q[r0 + MLA_NOPE:r0 + MLA_QK], g_q, one), g_q).astype(bf)

    g_k = g_k_ref[...]
    kpe = rope_a(lat[KV_RANK:], g_k, r_tok)
    for hd in range(MLA_HEADS):
        kt = head_a(kn[hd * MLA_NOPE:(hd + 1) * MLA_NOPE], kpe, g_k)
        k_ref[0, :, hd * LANES:(hd + 1) * LANES] = kt.T.astype(bf)

    g_dq = g_dq_ref[...] * (DIFF_D ** -0.5 * LOG2E)
    g_dk = g_dk_ref[...]
    zeros_half = jnp.zeros((DIFF_D, tm), bf)
    for hd in range(DIFF_HEADS):
        r0 = hd * 2 * DIFF_D
        q1 = head_b(dq[r0:r0 + DIFF_D], g_dq).astype(bf)
        q2 = head_b(dq[r0 + DIFF_D:r0 + 2 * DIFF_D], g_dq).astype(bf)
        b0 = 2 * hd * LANES
        dqt_ref[0, b0:b0 + LANES, :] = jnp.concatenate([q1, zeros_half], axis=0)
        dqt_ref[0, b0 + LANES:b0 + 2 * LANES, :] = jnp.concatenate([zeros_half, q2], axis=0)
        dkt = jnp.concatenate([head_b(dk[r0:r0 + DIFF_D], g_dk),
                               head_b(dk[r0 + DIFF_D:r0 + 2 * DIFF_D], g_dk)], axis=0)
        dk_ref[0, :, hd * LANES:(hd + 1) * LANES] = dkt.T.astype(bf)


def _causal_mask(s_t, tk, tq):
    key = lax.broadcasted_iota(jnp.int32, (tk, tq), 0)
    qry = lax.broadcasted_iota(jnp.int32, (tk, tq), 1)
    return jnp.where(key <= qry, s_t, -jnp.inf)


class _Chain(NamedTuple):
    k_ref: Any
    k_lanes: slice
    qt_ref: Any
    q_rows: slice
    vt_ref: Any
    v_rows: slice
    acc_rows: slice


def _pipelined_sweep(chains, qi, tq, tk, p_ref, al_ref, m_ref, acc_ref):
    assert tq == 2 * tk
    lower, upper = slice(0, tk), slice(tk, tq)

    def scores(ch, j, cols):
        return jnp.dot(ch.k_ref[0, j, :, ch.k_lanes], ch.qt_ref[0, ch.q_rows, cols],
                       preferred_element_type=jnp.float32)

    def stage_b(s_all, slot, cols, first=False):
        for c, s_t in enumerate(s_all):
            if first:
                m_new = jnp.max(s_t, axis=0, keepdims=True)
            else:
                m_old = m_ref[c, :, cols]
                m_new = jnp.maximum(m_old, jnp.max(s_t, axis=0, keepdims=True))
                al_ref[slot, c, :, cols] = jnp.exp2(m_old - m_new)
            m_ref[c, :, cols] = m_new
            p_ref[slot, c, :, cols] = jnp.exp2(s_t - m_new).astype(p_ref.dtype)

    ones = jnp.ones((SUM_ROWS, tk), p_ref.dtype)

    def pv_update(c, ch, j, slot, cols):
        vt_ones = jnp.concatenate([ch.vt_ref[0, j, ch.v_rows, :], ones], axis=0)
        pv = jnp.dot(vt_ones, p_ref[slot, c, :, cols], preferred_element_type=jnp.float32)
        acc_ref[ch.acc_rows, cols] = al_ref[slot, c, :, cols] * acc_ref[ch.acc_rows, cols] + pv

    def stage_ac(j, cols, prev_tile, prev_slot):
        s_all = []
        for c, ch in enumerate(chains):
            s_all.append(scores(ch, j, cols))
            pv_update(c, ch, prev_tile, prev_slot, full)
        return s_all

    full = slice(0, tq)
    d0, d1 = 2 * qi, 2 * qi + 1
    acc_ref[...] = jnp.zeros_like(acc_ref)
    al_ref[0] = jnp.ones(al_ref.shape[1:], al_ref.dtype)
    s_all = [scores(ch, d0, full) for ch in chains]
    stage_b([jnp.concatenate([_causal_mask(s_t[:, lower], tk, tk), s_t[:, upper]], axis=1)
             for s_t in s_all], 0, full, first=True)

    def two_steps(k, carry):
        stage_b(stage_ac(2 * k, full, jnp.where(k == 0, d0, 2 * k - 1), 0), 1, full)
        stage_b(stage_ac(2 * k + 1, full, 2 * k, 1), 0, full)
        return carry

    lax.fori_loop(0, qi, two_steps, 0)
    s_all = stage_ac(d1, upper, jnp.where(qi == 0, d0, 2 * qi - 1), 0)
    stage_b([_causal_mask(s_t, tk, tk) for s_t in s_all], 1, upper)
    for c, ch in enumerate(chains):
        pv_update(c, ch, d1, 1, upper)


def _attn_kernel(lq1_ref, lk1_ref, lq2_ref, lk2_ref, g_sub_ref,
                 qt_ref, k_ref, vt_ref, dqt_ref, dk_ref, dvt_ref, oa_ref, ob_ref,
                 p_ref, al_ref, m_ref, acc_ref, *, tq, tk, lam_init):
    rows_a, rows_b = MLA_V + SUM_ROWS, DIFF_V + SUM_ROWS
    base_b = MLA_HEADS * rows_a
    lane_group = lambda i: slice(i * LANES, (i + 1) * LANES)
    chains = [_Chain(k_ref, lane_group(hd), qt_ref, lane_group(hd),
                     vt_ref, slice(hd * MLA_V, (hd + 1) * MLA_V),
                     slice(hd * rows_a, (hd + 1) * rows_a)) for hd in range(MLA_HEADS)]
    chains += [_Chain(dk_ref, lane_group(c // 2), dqt_ref, lane_group(c),
                      dvt_ref, slice((c // 2) * DIFF_V, (c // 2 + 1) * DIFF_V),
                      slice(base_b + c * rows_b, base_b + (c + 1) * rows_b))
               for c in range(2 * DIFF_HEADS)]
    _pipelined_sweep(chains, pl.program_id(1), tq, tk, p_ref, al_ref, m_ref, acc_ref)

    def normalised(ch, dv):
        a0 = ch.acc_rows.start
        return acc_ref[a0:a0 + dv, :] / acc_ref[a0 + dv:a0 + dv + 1, :]

    for hd in range(MLA_HEADS):
        oa_ref[0, hd * MLA_V:(hd + 1) * MLA_V, :] = normalised(chains[hd], MLA_V).astype(oa_ref.dtype)

    lam = (jnp.exp(jnp.sum(lq1_ref[...] * lk1_ref[...], axis=-1, keepdims=True))
           - jnp.exp(jnp.sum(lq2_ref[...] * lk2_ref[...], axis=-1, keepdims=True))
           + lam_init)
    for hd in range(DIFF_HEADS):
        c = MLA_HEADS + 2 * hd
        o = normalised(chains[c], DIFF_V) - lam * normalised(chains[c + 1], DIFF_V)
        ms = jnp.mean(o * o, axis=0, keepdims=True)
        o = o * lax.rsqrt(ms + EPS) * g_sub_ref[...] * (1.0 - lam_init)
        ob_ref[0, hd * DIFF_V:(hd + 1) * DIFF_V, :] = o.astype(ob_ref.dtype)


def _ffn_kernel(x_ref, oa_ref, ob_ref, w_out_ref, g_ffn_ref, w_gate_ref, w_up_ref,
                conv_w_ref, conv_b_ref, w_down_ref, out_ref, y_ref, prev_ref):
    assert CONV_WIDTH == 3
    si = pl.program_id(1)
    tm = x_ref.shape[1]
    contract0 = (((0,), (0,)), ((), ()))
    n_a = oa_ref.shape[1]
    mix = (lax.dot_general(oa_ref[0], w_out_ref[:n_a, :], contract0,
                           preferred_element_type=jnp.float32)
           + lax.dot_general(ob_ref[0], w_out_ref[n_a:, :], contract0,
                             preferred_element_type=jnp.float32))
    x1 = x_ref[0] + mix
    h = _rms_rows(x1, g_ffn_ref[...]).astype(jnp.bfloat16)

    @pl.when(si == 0)
    def _():
        prev_ref[...] = jnp.zeros_like(prev_ref)

    for c0 in range(0, D_FF, FF_CHUNK):
        cw = min(FF_CHUNK, D_FF - c0)
        g = jnp.dot(h, w_gate_ref[:, c0:c0 + cw], preferred_element_type=jnp.float32)
        u = jnp.dot(h, w_up_ref[:, c0:c0 + cw], preferred_element_type=jnp.float32)
        row = lax.broadcasted_iota(jnp.int32, (tm, cw), 0)
        p1 = prev_ref[SUBLANES - 1:SUBLANES, c0:c0 + cw]
        p2 = prev_ref[SUBLANES - 2:SUBLANES - 1, c0:c0 + cw]
        g1 = jnp.where(row == 0, p1, pltpu.roll(g, 1, axis=0))
        g2 = jnp.where(row == 0, p2, jnp.where(row == 1, p1, pltpu.roll(g, 2, axis=0)))
        prev_ref[:, c0:c0 + cw] = g[tm - SUBLANES:tm, :]
        cg = (conv_b_ref[:, c0:c0 + cw] + g2 * conv_w_ref[0:1, c0:c0 + cw]
              + g1 * conv_w_ref[1:2, c0:c0 + cw] + g * conv_w_ref[2:3, c0:c0 + cw])
        y_ref[:, c0:c0 + cw] = (jax.nn.silu(cg) * u).astype(y_ref.dtype)

    out_ref[0] = x1 + jnp.dot(y_ref[...], w_down_ref[...], preferred_element_type=jnp.float32)


def _rope_tables(seq):
    pos = jnp.arange(seq, dtype=jnp.float32)[:, None]

    def tables(dim):
        inv = 1.0 / (ROPE_THETA ** (jnp.arange(0, dim, 2, dtype=jnp.float32) / dim))
        ang = pos * inv[None, :]
        return jnp.cos(ang), jnp.sin(ang)

    ca, sa = tables(MLA_ROPE)
    cb, sb = tables(DIFF_D)
    return ca.T, sa.T, cb.T, sb.T


def _const_spec(shape):
    return pl.BlockSpec(shape, lambda *_: (0,) * len(shape))


def kernel(x, attn_norm_g, w_in, q_a_norm_g, w_q_up, kv_a_norm_g, w_kv_up, mla_q_norm_g,
           mla_k_norm_g, diff_q_norm_g, diff_k_norm_g, lambda_q1, lambda_k1, lambda_q2, lambda_k2,
           diff_subln_g, w_out, ffn_norm_g, w_gate, w_up, conv_w, conv_b, w_down):
    B, S, _ = x.shape
    depth = w_in.shape[0]
    bf = jnp.bfloat16
    cos_a, sin_a, cos_b, sin_b = _rope_tables(S)

    tk, tq = KEY_TILE, 2 * KEY_TILE
    tm_proj, tm_ffn = PROJ_TOKENS, FFN_TOKENS
    assert tm_proj % tk == 0 and S % tq == 0 and S % tm_proj == 0 and S % tm_ffn == 0
    nk = S // tk

    for l in range(depth):
        w_in_t = w_in[l].T.astype(bf)
        w_q_t = w_q_up[l].T.astype(bf)
        wkv_t = w_kv_up[l].T.reshape(MLA_HEADS, MLA_NOPE + MLA_V, KV_RANK)
        w_k_t = wkv_t[:, :MLA_NOPE].reshape(MLA_HEADS * MLA_NOPE, KV_RANK).astype(bf)
        w_v_t = wkv_t[:, MLA_NOPE:].reshape(MLA_HEADS * MLA_V, KV_RANK).astype(bf)

        n_tok_tiles = S // tm_proj
        tok3 = lambda b, s: (b, s, 0)
        feat3 = lambda b, s: (b, 0, s)
        blk4 = lambda b, s: (b, s, 0, 0)
        rope_a_spec = pl.BlockSpec((MLA_ROPE // 2, tm_proj), lambda b, s: (0, s))
        rope_b_spec = pl.BlockSpec((DIFF_D // 2, tm_proj), lambda b, s: (0, s))
        cast_weights = (w_gate[l], w_up[l], w_down[l], w_out[l])
        n_steps = B * n_tok_tiles
        assert all(w.shape[0] % (n_steps * 2 * SUBLANES) == 0 for w in cast_weights)
        cast_specs = [pl.BlockSpec((w.shape[0] // n_steps, w.shape[1]),
                                   lambda b, s: (b * n_tok_tiles + s, 0)) for w in cast_weights]
        qt, k_a, vt, dqt, dk, dvt, w_gate_bf, w_up_bf, w_down_bf, w_out_bf = pl.pallas_call(
            _proj_kernel,
            grid=(B, n_tok_tiles),
            in_specs=[
                pl.BlockSpec((1, tm_proj, D_MODEL), tok3),
                _const_spec((1, D_MODEL)),
                _const_spec((IN_COLS, D_MODEL)),
                _const_spec((Q_RANK, 1)),
                _const_spec((MLA_HEADS * MLA_QK, Q_RANK)),
                _const_spec((KV_RANK, 1)),
                _const_spec((MLA_HEADS * MLA_NOPE, KV_RANK)),
                _const_spec((MLA_HEADS * MLA_V, KV_RANK)),
                _const_spec((MLA_QK, 1)), _const_spec((MLA_QK, 1)),
                _const_spec((DIFF_D, 1)), _const_spec((DIFF_D, 1)),
                rope_a_spec, rope_a_spec, rope_b_spec, rope_b_spec,
                *cast_specs,
            ],
            out_specs=[
                pl.BlockSpec((1, MLA_HEADS * LANES, tm_proj), feat3),
                pl.BlockSpec((1, tm_proj, MLA_HEADS * LANES), tok3),
                pl.BlockSpec((1, tm_proj // tk, MLA_HEADS * MLA_V, tk), blk4),
                pl.BlockSpec((1, DIFF_HEADS * 2 * LANES, tm_proj), feat3),
                pl.BlockSpec((1, tm_proj, DIFF_HEADS * LANES), tok3),
                pl.BlockSpec((1, tm_proj // tk, DIFF_HEADS * DIFF_V, tk), blk4),
                *cast_specs,
            ],
            out_shape=[
                jax.ShapeDtypeStruct((B, MLA_HEADS * LANES, S), bf),
                jax.ShapeDtypeStruct((B, S, MLA_HEADS * LANES), bf),
                jax.ShapeDtypeStruct((B, nk, MLA_HEADS * MLA_V, tk), bf),
                jax.ShapeDtypeStruct((B, DIFF_HEADS * 2 * LANES, S), bf),
                jax.ShapeDtypeStruct((B, S, DIFF_HEADS * LANES), bf),
                jax.ShapeDtypeStruct((B, nk, DIFF_HEADS * DIFF_V, tk), bf),
                *[jax.ShapeDtypeStruct(w.shape, bf) for w in cast_weights],
            ],
            compiler_params=pltpu.CompilerParams(
                dimension_semantics=("arbitrary", "arbitrary"), vmem_limit_bytes=VMEM_LIMIT),
            name="proj",
        )(x, attn_norm_g[l].reshape(1, -1), w_in_t, q_a_norm_g[l].reshape(-1, 1), w_q_t,
          kv_a_norm_g[l].reshape(-1, 1), w_k_t, w_v_t,
          mla_q_norm_g[l].reshape(-1, 1), mla_k_norm_g[l].reshape(-1, 1),
          diff_q_norm_g[l].reshape(-1, 1), diff_k_norm_g[l].reshape(-1, 1),
          cos_a, sin_a, cos_b, sin_b, *cast_weights)

        nq = S // tq
        n_chains = MLA_HEADS + 2 * DIFF_HEADS
        lam_init = 0.8 - 0.6 * math.exp(-0.3 * l)
        lam_spec = _const_spec((1, DIFF_D))
        q_tile = lambda b, i: (b, 0, i)
        per_batch = lambda b, i: (b, 0, 0, 0)
        o_a, o_b = pl.pallas_call(
            functools.partial(_attn_kernel, tq=tq, tk=tk, lam_init=lam_init),
            grid=(B, nq),
            in_specs=[
                lam_spec, lam_spec, lam_spec, lam_spec,
                _const_spec((DIFF_V, 1)),
                pl.BlockSpec((1, MLA_HEADS * LANES, tq), q_tile),
                pl.BlockSpec((1, nk, tk, MLA_HEADS * LANES), per_batch),
                pl.BlockSpec((1, nk, MLA_HEADS * MLA_V, tk), per_batch),
                pl.BlockSpec((1, DIFF_HEADS * 2 * LANES, tq), q_tile),
                pl.BlockSpec((1, nk, tk, DIFF_HEADS * LANES), per_batch),
                pl.BlockSpec((1, nk, DIFF_HEADS * DIFF_V, tk), per_batch),
            ],
            out_specs=[pl.BlockSpec((1, MLA_HEADS * MLA_V, tq), q_tile),
                       pl.BlockSpec((1, DIFF_HEADS * DIFF_V, tq), q_tile)],
            out_shape=[jax.ShapeDtypeStruct((B, MLA_HEADS * MLA_V, S), bf),
                       jax.ShapeDtypeStruct((B, DIFF_HEADS * DIFF_V, S), bf)],
            scratch_shapes=[
                pltpu.VMEM((2, n_chains, tk, tq), bf),
                pltpu.VMEM((2, n_chains, 1, tq), jnp.float32),
                pltpu.VMEM((n_chains, 1, tq), jnp.float32),
                pltpu.VMEM((MLA_HEADS * (MLA_V + SUM_ROWS)
                            + 2 * DIFF_HEADS * (DIFF_V + SUM_ROWS), tq), jnp.float32)],
            compiler_params=pltpu.CompilerParams(
                dimension_semantics=("arbitrary", "arbitrary"), vmem_limit_bytes=VMEM_LIMIT),
            name="attn",
        )(lambda_q1[l].reshape(1, -1), lambda_k1[l].reshape(1, -1),
          lambda_q2[l].reshape(1, -1), lambda_k2[l].reshape(1, -1),
          diff_subln_g[l].reshape(-1, 1),
          qt, k_a.reshape(B, nk, tk, MLA_HEADS * LANES), vt,
          dqt, dk.reshape(B, nk, tk, DIFF_HEADS * LANES), dvt)

        n_a = MLA_HEADS * MLA_V
        x = pl.pallas_call(
            _ffn_kernel,
            grid=(B, S // tm_ffn),
            in_specs=[
                pl.BlockSpec((1, tm_ffn, D_MODEL), lambda b, s: (b, s, 0)),
                pl.BlockSpec((1, n_a, tm_ffn), lambda b, s: (b, 0, s)),
                pl.BlockSpec((1, D_MODEL - n_a, tm_ffn), lambda b, s: (b, 0, s)),
                _const_spec((D_MODEL, D_MODEL)),
                _const_spec((1, D_MODEL)),
                _const_spec((D_MODEL, D_FF)),
                _const_spec((D_MODEL, D_FF)),
                _const_spec((CONV_WIDTH, D_FF)),
                _const_spec((1, D_FF)),
                _const_spec((D_FF, D_MODEL)),
            ],
            out_specs=pl.BlockSpec((1, tm_ffn, D_MODEL), lambda b, s: (b, s, 0)),
            out_shape=jax.ShapeDtypeStruct((B, S, D_MODEL), x.dtype),
            scratch_shapes=[pltpu.VMEM((tm_ffn, D_FF), bf),
                            pltpu.VMEM((SUBLANES, D_FF), jnp.float32)],
            compiler_params=pltpu.CompilerParams(
                dimension_semantics=("arbitrary", "arbitrary"), vmem_limit_bytes=VMEM_LIMIT),
            name="ffn",
        )(x, o_a, o_b, w_out_bf, ffn_norm_g[l].reshape(1, -1), w_gate_bf, w_up_bf,
          conv_w[l], conv_b[l].reshape(1, -1), w_down_bf)
    return x
```

```python
import functools
import math
from typing import Any, NamedTuple

import jax
import jax.numpy as jnp
from jax import lax
from jax.experimental import pallas as pl
from jax.experimental.pallas import tpu as pltpu

D_MODEL = 1024
MLA_HEADS = 8
MLA_NOPE = 64
MLA_ROPE = 32
MLA_V = 64
MLA_QK = MLA_NOPE + MLA_ROPE
Q_RANK = 384
KV_RANK = 256
DIFF_HEADS = 4
DIFF_D = 64
DIFF_V = 2 * DIFF_D
D_FF = 2816
CONV_WIDTH = 3
ROPE_THETA = 10000.0
EPS = 1e-6
LANES = 128
SUBLANES = 8
SUM_ROWS = 16
LOG2E = math.log2(math.e)
R_Q = 0
R_KV = R_Q + Q_RANK
R_KPE = R_KV + KV_RANK
R_DQ = R_KPE + MLA_ROPE
R_DK = R_DQ + DIFF_HEADS * 2 * DIFF_D
R_DV = R_DK + DIFF_HEADS * 2 * DIFF_D
IN_COLS = R_DV + DIFF_HEADS * DIFF_V

KEY_TILE = 256
PROJ_TOKENS = 1024
FFN_TOKENS = 1024
FF_CHUNK = 1024
V7X_VMEM_BYTES = 64 * 1024 * 1024
VMEM_LIMIT = V7X_VMEM_BYTES - 8 * 1024 * 1024


def _rms_rows(x, g):
    ms = jnp.mean(x * x, axis=-1, keepdims=True)
    return x * lax.rsqrt(ms + EPS) * g


def _rms_cols(xt, n, scale):
    ms = jnp.sum(xt * xt, axis=0, keepdims=True) * (1.0 / n)
    return scale * lax.rsqrt(scale * scale * ms + EPS)


def _rope_cols(xt, cos, sin):
    half = xt.shape[0] // 2
    x1, x2 = xt[:half], xt[half:]
    return x1 * cos - x2 * sin, x2 * cos + x1 * sin


def _proj_kernel(x_ref, g_attn_ref, w_in_ref, g_qa_ref, w_q_ref, g_kva_ref, w_k_ref, w_v_ref,
                 g_q_ref, g_k_ref, g_dq_ref, g_dk_ref,
                 cos_a_ref, sin_a_ref, cos_b_ref, sin_b_ref,
                 w_gate_ref, w_up_ref, w_down_ref, w_out_ref,
                 qt_ref, k_ref, vt_ref, dqt_ref, dk_ref, dvt_ref,
                 w_gate_bf_ref, w_up_bf_ref, w_down_bf_ref, w_out_bf_ref):
    tm = x_ref.shape[1]
    bf = jnp.bfloat16
    for src_ref, dst_ref in ((w_gate_ref, w_gate_bf_ref), (w_up_ref, w_up_bf_ref),
                             (w_down_ref, w_down_bf_ref), (w_out_ref, w_out_bf_ref)):
        dst_ref[...] = src_ref[...].astype(bf)
    x = x_ref[0]
    r_tok = lax.rsqrt(jnp.mean(x * x, axis=-1, keepdims=True) + EPS)
    r_tok = jnp.transpose(jnp.broadcast_to(r_tok, (tm, LANES)))[0:1, :]
    h = (x * g_attn_ref[...]).astype(bf)

    def in_proj(r0, r1):
        return lax.dot_general(w_in_ref[r0:r1, :], h, (((1,), (1,)), ((), ())),
                               preferred_element_type=jnp.float32)

    cos_a, sin_a = cos_a_ref[...], sin_a_ref[...]
    cos_b, sin_b = cos_b_ref[...], sin_b_ref[...]
    zeros_pad = jnp.zeros((LANES - MLA_QK, tm), jnp.float32)
    one = jnp.ones((1, tm), jnp.float32)

    def head_a(nope, pe_roped, g):
        nope = nope * _rms_cols(nope, MLA_NOPE, one) * g[:MLA_NOPE]
        return jnp.concatenate([nope, *pe_roped, zeros_pad], axis=0)

    def rope_a(pe, g, scale):
        pe = pe * _rms_cols(pe, MLA_ROPE, scale) * g[MLA_NOPE:]
        return _rope_cols(pe, cos_a, sin_a)

    def head_b(xt, g):
        xt = xt * _rms_cols(xt, DIFF_D, r_tok) * g
        return jnp.concatenate(_rope_cols(xt, cos_b, sin_b), axis=0)

    cq = in_proj(R_Q, R_KV)
    lat = in_proj(R_KV, R_DQ)
    dq = in_proj(R_DQ, R_DK)

    cq = (cq * _rms_cols(cq, Q_RANK, r_tok) * g_qa_ref[...]).astype(bf)
    q = jnp.dot(w_q_ref[...], cq, preferred_element_type=jnp.float32)
    ckv = lat[:KV_RANK]
    ckv = (ckv * _rms_cols(ckv, KV_RANK, r_tok) * g_kva_ref[...]).astype(bf)
    kn = jnp.dot(w_k_ref[...], ckv, preferred_element_type=jnp.float32)
    dk = in_proj(R_DK, R_DV)
    v = jnp.dot(w_v_ref[...], ckv, preferred_element_type=jnp.float32).astype(bf)
    dv = (in_proj(R_DV, IN_COLS) * r_tok).astype(bf)
    tk = vt_ref.shape[3]
    for t in range(tm // tk):
        vt_ref[0, t] = v[:, t * tk:(t + 1) * tk]
        dvt_ref[0, t] = dv[:, t * tk:(t + 1) * tk]

    g_q = g_q_ref[...] * (MLA_QK ** -0.5 * LOG2E)
    for hd in range(MLA_HEADS):
        r0 = hd * MLA_QK
        qt_ref[0, hd * LANES:(hd + 1) * LANES, :] = head_a(
            q[r0:r0 + MLA_NOPE], rope_a(q[r0 + MLA_NOPE:r0 + MLA_QK], g_q, one), g_q).astype(bf)

    g_k = g_k_ref[...]
    kpe = rope_a(lat[KV_RANK:], g_k, r_tok)
    for hd in range(MLA_HEADS):
        kt = head_a(kn[hd * MLA_NOPE:(hd + 1) * MLA_NOPE], kpe, g_k)
        k_ref[0, :, hd * LANES:(hd + 1) * LANES] = kt.T.astype(bf)

    g_dq = g_dq_ref[...] * (DIFF_D ** -0.5 * LOG2E)
    g_dk = g_dk_ref[...]
    zeros_half = jnp.zeros((DIFF_D, tm), bf)
    for hd in range(DIFF_HEADS):
        r0 = hd * 2 * DIFF_D
        q1 = head_b(dq[r0:r0 + DIFF_D], g_dq).astype(bf)
        q2 = head_b(dq[r0 + DIFF_D:r0 + 2 * DIFF_D], g_dq).astype(bf)
        b0 = 2 * hd * LANES
        dqt_ref[0, b0:b0 + LANES, :] = jnp.concatenate([q1, zeros_half], axis=0)
        dqt_ref[0, b0 + LANES:b0 + 2 * LANES, :] = jnp.concatenate([zeros_half, q2], axis=0)
        dkt = jnp.concatenate([head_b(dk[r0:r0 + DIFF_D], g_dk),
                               head_b(dk[r0 + DIFF_D:r0 + 2 * DIFF_D], g_dk)], axis=0)
        dk_ref[0, :, hd * LANES:(hd + 1) * LANES] = dkt.T.astype(bf)


def _causal_mask(s_t, tk, tq):
    key = lax.broadcasted_iota(jnp.int32, (tk, tq), 0)
    qry = lax.broadcasted_iota(jnp.int32, (tk, tq), 1)
    return jnp.where(key <= qry, s_t, -jnp.inf)


class _Chain(NamedTuple):
    k_ref: Any
    k_lanes: slice
    qt_ref: Any
    q_rows: slice
    vt_ref: Any
    v_rows: slice
    acc_rows: slice


def _pipelined_sweep(chains, qi, tq, tk, p_ref, al_ref, m_ref, acc_ref, side_matmul):
    assert tq == 2 * tk
    lower, upper = slice(0, tk), slice(tk, tq)

    def scores(ch, j, cols):
        return jnp.dot(ch.k_ref[0, j, :, ch.k_lanes], ch.qt_ref[0, ch.q_rows, cols],
                       preferred_element_type=jnp.float32)

    def stage_b(s_all, slot, cols, first=False):
        for c, s_t in enumerate(s_all):
            if first:
                m_new = jnp.max(s_t, axis=0, keepdims=True)
            else:
                m_old = m_ref[c, :, cols]
                m_new = jnp.maximum(m_old, jnp.max(s_t, axis=0, keepdims=True))
                al_ref[slot, c, :, cols] = jnp.exp2(m_old - m_new)
            m_ref[c, :, cols] = m_new
            p_ref[slot, c, :, cols] = jnp.exp2(s_t - m_new).astype(p_ref.dtype)

    ones = jnp.ones((SUM_ROWS, tk), p_ref.dtype)

    def pv_update(c, ch, j, slot, cols):
        vt_ones = jnp.concatenate([ch.vt_ref[0, j, ch.v_rows, :], ones], axis=0)
        pv = jnp.dot(vt_ones, p_ref[slot, c, :, cols], preferred_element_type=jnp.float32)
        acc_ref[ch.acc_rows, cols] = al_ref[slot, c, :, cols] * acc_ref[ch.acc_rows, cols] + pv

    def stage_ac(j, cols, prev_tile, prev_slot):
        s_all = []
        for c, ch in enumerate(chains):
            s_all.append(scores(ch, j, cols))
            pv_update(c, ch, prev_tile, prev_slot, full)
        return s_all

    full = slice(0, tq)
    d0, d1 = 2 * qi, 2 * qi + 1
    acc_ref[...] = jnp.zeros_like(acc_ref)
    al_ref[0] = jnp.ones(al_ref.shape[1:], al_ref.dtype)
    s_all = [scores(ch, d0, full) for ch in chains]
    side_matmul()
    stage_b([jnp.concatenate([_causal_mask(s_t[:, lower], tk, tk), s_t[:, upper]], axis=1)
             for s_t in s_all], 0, full, first=True)

    def two_steps(k, carry):
        stage_b(stage_ac(2 * k, full, jnp.where(k == 0, d0, 2 * k - 1), 0), 1, full)
        stage_b(stage_ac(2 * k + 1, full, 2 * k, 1), 0, full)
        return carry

    lax.fori_loop(0, qi, two_steps, 0)
    s_all = stage_ac(d1, upper, jnp.where(qi == 0, d0, 2 * qi - 1), 0)
    stage_b([_causal_mask(s_t, tk, tk) for s_t in s_all], 1, upper)
    for c, ch in enumerate(chains):
        pv_update(c, ch, d1, 1, upper)


def _attn_kernel(lq1_ref, lk1_ref, lq2_ref, lk2_ref, g_sub_ref,
                 qt_ref, k_ref, vt_ref, dqt_ref, dk_ref, dvt_ref, x_ref, w_out_ref, x1_ref,
                 p_ref, al_ref, m_ref, acc_ref, o_ref, *, tq, tk, lam_init, n_q_tiles, n_tiles):
    t = pl.program_id(0)
    contract0 = (((0,), (0,)), ((), ()))

    @pl.when(t == 0)
    def _():
        o_ref[...] = jnp.zeros_like(o_ref)

    def out_proj():
        x1_ref[0] = x_ref[0] + lax.dot_general(o_ref[...], w_out_ref[...], contract0,
                                               preferred_element_type=jnp.float32)

    rows_a, rows_b = MLA_V + SUM_ROWS, DIFF_V + SUM_ROWS
    base_b = MLA_HEADS * rows_a
    n_a = MLA_HEADS * MLA_V
    lane_group = lambda i: slice(i * LANES, (i + 1) * LANES)
    chains = [_Chain(k_ref, lane_group(hd), qt_ref, lane_group(hd),
                     vt_ref, slice(hd * MLA_V, (hd + 1) * MLA_V),
                     slice(hd * rows_a, (hd + 1) * rows_a)) for hd in range(MLA_HEADS)]
    chains += [_Chain(dk_ref, lane_group(c // 2), dqt_ref, lane_group(c),
                      dvt_ref, slice((c // 2) * DIFF_V, (c // 2 + 1) * DIFF_V),
                      slice(base_b + c * rows_b, base_b + (c + 1) * rows_b))
               for c in range(2 * DIFF_HEADS)]

    def normalised(ch, dv):
        a0 = ch.acc_rows.start
        return acc_ref[a0:a0 + dv, :] / acc_ref[a0 + dv:a0 + dv + 1, :]

    @pl.when(t < n_tiles)
    def _():
        _pipelined_sweep(chains, t % n_q_tiles, tq, tk, p_ref, al_ref, m_ref, acc_ref, out_proj)

        for hd in range(MLA_HEADS):
            o_ref[hd * MLA_V:(hd + 1) * MLA_V, :] = normalised(chains[hd], MLA_V).astype(o_ref.dtype)

        lam = (jnp.exp(jnp.sum(lq1_ref[...] * lk1_ref[...], axis=-1, keepdims=True))
               - jnp.exp(jnp.sum(lq2_ref[...] * lk2_ref[...], axis=-1, keepdims=True))
               + lam_init)
        for hd in range(DIFF_HEADS):
            c = MLA_HEADS + 2 * hd
            o = normalised(chains[c], DIFF_V) - lam * normalised(chains[c + 1], DIFF_V)
            ms = jnp.mean(o * o, axis=0, keepdims=True)
            o = o * lax.rsqrt(ms + EPS) * g_sub_ref[...] * (1.0 - lam_init)
            o_ref[n_a + hd * DIFF_V:n_a + (hd + 1) * DIFF_V, :] = o.astype(o_ref.dtype)

    @pl.when(t == n_tiles)
    def _():
        out_proj()


def _ffn_kernel(x1_ref, g_ffn_ref, w_gate_ref, w_up_ref,
                conv_w_ref, conv_b_ref, w_down_ref, out_ref, y_ref, prev_ref):
    assert CONV_WIDTH == 3
    si = pl.program_id(1)
    tm = x1_ref.shape[1]
    x1 = x1_ref[0]
    h = _rms_rows(x1, g_ffn_ref[...]).astype(jnp.bfloat16)

    @pl.when(si == 0)
    def _():
        prev_ref[...] = jnp.zeros_like(prev_ref)

    for c0 in range(0, D_FF, FF_CHUNK):
        cw = min(FF_CHUNK, D_FF - c0)
        g = jnp.dot(h, w_gate_ref[:, c0:c0 + cw], preferred_element_type=jnp.float32)
        u = jnp.dot(h, w_up_ref[:, c0:c0 + cw], preferred_element_type=jnp.float32)
        row = lax.broadcasted_iota(jnp.int32, (tm, cw), 0)
        p1 = prev_ref[SUBLANES - 1:SUBLANES, c0:c0 + cw]
        p2 = prev_ref[SUBLANES - 2:SUBLANES - 1, c0:c0 + cw]
        g1 = jnp.where(row == 0, p1, pltpu.roll(g, 1, axis=0))
        g2 = jnp.where(row == 0, p2, jnp.where(row == 1, p1, pltpu.roll(g, 2, axis=0)))
        prev_ref[:, c0:c0 + cw] = g[tm - SUBLANES:tm, :]
        cg = (conv_b_ref[:, c0:c0 + cw] + g2 * conv_w_ref[0:1, c0:c0 + cw]
              + g1 * conv_w_ref[1:2, c0:c0 + cw] + g * conv_w_ref[2:3, c0:c0 + cw])
        y_ref[:, c0:c0 + cw] = (jax.nn.silu(cg) * u).astype(y_ref.dtype)

    out_ref[0] = x1 + jnp.dot(y_ref[...], w_down_ref[...], preferred_element_type=jnp.float32)


def _rope_tables(seq):
    pos = jnp.arange(seq, dtype=jnp.float32)[:, None]

    def tables(dim):
        inv = 1.0 / (ROPE_THETA ** (jnp.arange(0, dim, 2, dtype=jnp.float32) / dim))
        ang = pos * inv[None, :]
        return jnp.cos(ang), jnp.sin(ang)

    ca, sa = tables(MLA_ROPE)
    cb, sb = tables(DIFF_D)
    return ca.T, sa.T, cb.T, sb.T


def _const_spec(shape):
    return pl.BlockSpec(shape, lambda *_: (0,) * len(shape))


def kernel(x, attn_norm_g, w_in, q_a_norm_g, w_q_up, kv_a_norm_g, w_kv_up, mla_q_norm_g,
           mla_k_norm_g, diff_q_norm_g, diff_k_norm_g, lambda_q1, lambda_k1, lambda_q2, lambda_k2,
           diff_subln_g, w_out, ffn_norm_g, w_gate, w_up, conv_w, conv_b, w_down):
    B, S, _ = x.shape
    depth = w_in.shape[0]
    bf = jnp.bfloat16
    cos_a, sin_a, cos_b, sin_b = _rope_tables(S)

    tk, tq = KEY_TILE, 2 * KEY_TILE
    tm_proj, tm_ffn = PROJ_TOKENS, FFN_TOKENS
    assert tm_proj % tk == 0 and S % tq == 0 and S % tm_proj == 0 and S % tm_ffn == 0
    nk = S // tk

    for l in range(depth):
        w_in_t = w_in[l].T.astype(bf)
        w_q_t = w_q_up[l].T.astype(bf)
        wkv_t = w_kv_up[l].T.reshape(MLA_HEADS, MLA_NOPE + MLA_V, KV_RANK)
        w_k_t = wkv_t[:, :MLA_NOPE].reshape(MLA_HEADS * MLA_NOPE, KV_RANK).astype(bf)
        w_v_t = wkv_t[:, MLA_NOPE:].reshape(MLA_HEADS * MLA_V, KV_RANK).astype(bf)

        n_tok_tiles = S // tm_proj
        tok3 = lambda b, s: (b, s, 0)
        feat3 = lambda b, s: (b, 0, s)
        blk4 = lambda b, s: (b, s, 0, 0)
        rope_a_spec = pl.BlockSpec((MLA_ROPE // 2, tm_proj), lambda b, s: (0, s))
        rope_b_spec = pl.BlockSpec((DIFF_D // 2, tm_proj), lambda b, s: (0, s))
        cast_weights = (w_gate[l], w_up[l], w_down[l], w_out[l])
        n_steps = B * n_tok_tiles
        assert all(w.shape[0] % (n_steps * 2 * SUBLANES) == 0 for w in cast_weights)
        cast_specs = [pl.BlockSpec((w.shape[0] // n_steps, w.shape[1]),
                                   lambda b, s: (b * n_tok_tiles + s, 0)) for w in cast_weights]
        qt, k_a, vt, dqt, dk, dvt, w_gate_bf, w_up_bf, w_down_bf, w_out_bf = pl.pallas_call(
            _proj_kernel,
            grid=(B, n_tok_tiles),
            in_specs=[
                pl.BlockSpec((1, tm_proj, D_MODEL), tok3),
                _const_spec((1, D_MODEL)),
                _const_spec((IN_COLS, D_MODEL)),
                _const_spec((Q_RANK, 1)),
                _const_spec((MLA_HEADS * MLA_QK, Q_RANK)),
                _const_spec((KV_RANK, 1)),
                _const_spec((MLA_HEADS * MLA_NOPE, KV_RANK)),
                _const_spec((MLA_HEADS * MLA_V, KV_RANK)),
                _const_spec((MLA_QK, 1)), _const_spec((MLA_QK, 1)),
                _const_spec((DIFF_D, 1)), _const_spec((DIFF_D, 1)),
                rope_a_spec, rope_a_spec, rope_b_spec, rope_b_spec,
                *cast_specs,
            ],
            out_specs=[
                pl.BlockSpec((1, MLA_HEADS * LANES, tm_proj), feat3),
                pl.BlockSpec((1, tm_proj, MLA_HEADS * LANES), tok3),
                pl.BlockSpec((1, tm_proj // tk, MLA_HEADS * MLA_V, tk), blk4),
                pl.BlockSpec((1, DIFF_HEADS * 2 * LANES, tm_proj), feat3),
                pl.BlockSpec((1, tm_proj, DIFF_HEADS * LANES), tok3),
                pl.BlockSpec((1, tm_proj // tk, DIFF_HEADS * DIFF_V, tk), blk4),
                *cast_specs,
            ],
            out_shape=[
                jax.ShapeDtypeStruct((B, MLA_HEADS * LANES, S), bf),
                jax.ShapeDtypeStruct((B, S, MLA_HEADS * LANES), bf),
                jax.ShapeDtypeStruct((B, nk, MLA_HEADS * MLA_V, tk), bf),
                jax.ShapeDtypeStruct((B, DIFF_HEADS * 2 * LANES, S), bf),
                jax.ShapeDtypeStruct((B, S, DIFF_HEADS * LANES), bf),
                jax.ShapeDtypeStruct((B, nk, DIFF_HEADS * DIFF_V, tk), bf),
                *[jax.ShapeDtypeStruct(w.shape, bf) for w in cast_weights],
            ],
            compiler_params=pltpu.CompilerParams(
                dimension_semantics=("arbitrary", "arbitrary"), vmem_limit_bytes=VMEM_LIMIT),
            name="proj",
        )(x, attn_norm_g[l].reshape(1, -1), w_in_t, q_a_norm_g[l].reshape(-1, 1), w_q_t,
          kv_a_norm_g[l].reshape(-1, 1), w_k_t, w_v_t,
          mla_q_norm_g[l].reshape(-1, 1), mla_k_norm_g[l].reshape(-1, 1),
          diff_q_norm_g[l].reshape(-1, 1), diff_k_norm_g[l].reshape(-1, 1),
          cos_a, sin_a, cos_b, sin_b, *cast_weights)

        nq = S // tq
        n_tiles = B * nq
        n_chains = MLA_HEADS + 2 * DIFF_HEADS
        lam_init = 0.8 - 0.6 * math.exp(-0.3 * l)
        lam_spec = _const_spec((1, DIFF_D))
        cur = lambda t: jnp.minimum(t, n_tiles - 1)
        lag = lambda t: jnp.maximum(t - 1, 0)
        q_tile = lambda t: (cur(t) // nq, 0, cur(t) % nq)
        per_batch = lambda t: (cur(t) // nq, 0, 0, 0)
        lag_tile = lambda t: (lag(t) // nq, lag(t) % nq, 0)
        x1 = pl.pallas_call(
            functools.partial(_attn_kernel, tq=tq, tk=tk, lam_init=lam_init,
                              n_q_tiles=nq, n_tiles=n_tiles),
            grid=(n_tiles + 1,),
            in_specs=[
                lam_spec, lam_spec, lam_spec, lam_spec,
                _const_spec((DIFF_V, 1)),
                pl.BlockSpec((1, MLA_HEADS * LANES, tq), q_tile),
                pl.BlockSpec((1, nk, tk, MLA_HEADS * LANES), per_batch),
                pl.BlockSpec((1, nk, MLA_HEADS * MLA_V, tk), per_batch),
                pl.BlockSpec((1, DIFF_HEADS * 2 * LANES, tq), q_tile),
                pl.BlockSpec((1, nk, tk, DIFF_HEADS * LANES), per_batch),
                pl.BlockSpec((1, nk, DIFF_HEADS * DIFF_V, tk), per_batch),
                pl.BlockSpec((1, tq, D_MODEL), lag_tile),
                _const_spec((D_MODEL, D_MODEL)),
            ],
            out_specs=pl.BlockSpec((1, tq, D_MODEL), lag_tile),
            out_shape=jax.ShapeDtypeStruct((B, S, D_MODEL), x.dtype),
            scratch_shapes=[
                pltpu.VMEM((2, n_chains, tk, tq), bf),
                pltpu.VMEM((2, n_chains, 1, tq), jnp.float32),
                pltpu.VMEM((n_chains, 1, tq), jnp.float32),
                pltpu.VMEM((MLA_HEADS * (MLA_V + SUM_ROWS)
                            + 2 * DIFF_HEADS * (DIFF_V + SUM_ROWS), tq), jnp.float32),
                pltpu.VMEM((D_MODEL, tq), bf)],
            compiler_params=pltpu.CompilerParams(
                dimension_semantics=("arbitrary",), vmem_limit_bytes=VMEM_LIMIT),
            name="attn",
        )(lambda_q1[l].reshape(1, -1), lambda_k1[l].reshape(1, -1),
          lambda_q2[l].reshape(1, -1), lambda_k2[l].reshape(1, -1),
          diff_subln_g[l].reshape(-1, 1),
          qt, k_a.reshape(B, nk, tk, MLA_HEADS * LANES), vt,
          dqt, dk.reshape(B, nk, tk, DIFF_HEADS * LANES), dvt, x, w_out_bf)

        x = pl.pallas_call(
            _ffn_kernel,
            grid=(B, S // tm_ffn),
            in_specs=[
                pl.BlockSpec((1, tm_ffn, D_MODEL), lambda b, s: (b, s, 0)),
                _const_spec((1, D_MODEL)),
                _const_spec((D_MODEL, D_FF)),
                _const_spec((D_MODEL, D_FF)),
                _const_spec((CONV_WIDTH, D_FF)),
                _const_spec((1, D_FF)),
                _const_spec((D_FF, D_MODEL)),
            ],
            out_specs=pl.BlockSpec((1, tm_ffn, D_MODEL), lambda b, s: (b, s, 0)),
            out_shape=jax.ShapeDtypeStruct((B, S, D_MODEL), x.dtype),
            scratch_shapes=[pltpu.VMEM((tm_ffn, D_FF), bf),
                            pltpu.VMEM((SUBLANES, D_FF), jnp.float32)],
            compiler_params=pltpu.CompilerParams(
                dimension_semantics=("arbitrary", "arbitrary"), vmem_limit_bytes=VMEM_LIMIT),
            name="ffn",
        )(x1, ffn_norm_g[l].reshape(1, -1), w_gate_bf, w_up_bf,
          conv_w[l], conv_b[l].reshape(1, -1), w_down_bf)
    return x
```

```python
import functools
import math
from typing import Any, NamedTuple

import jax
import jax.numpy as jnp
from jax import lax
from jax.experimental import pallas as pl
from jax.experimental.pallas import tpu as pltpu

D_MODEL = 1024
MLA_HEADS = 8
MLA_NOPE = 64
MLA_ROPE = 32
MLA_V = 64
MLA_QK = MLA_NOPE + MLA_ROPE
Q_RANK = 384
KV_RANK = 256
DIFF_HEADS = 4
DIFF_D = 64
DIFF_V = 2 * DIFF_D
D_FF = 2816
CONV_WIDTH = 3
ROPE_THETA = 10000.0
EPS = 1e-6
LANES = 128
SUBLANES = 8
SUM_ROWS = 16
LOG2E = math.log2(math.e)
R_Q = 0
R_KV = R_Q + Q_RANK
R_KPE = R_KV + KV_RANK
R_DQ = R_KPE + MLA_ROPE
R_DK = R_DQ + DIFF_HEADS * 2 * DIFF_D
R_DV = R_DK + DIFF_HEADS * 2 * DIFF_D
IN_COLS = R_DV + DIFF_HEADS * DIFF_V

KEY_TILE = 256
PROJ_TOKENS = 1024
FFN_TOKENS = 1024
FF_CHUNK = 1024
V7X_VMEM_BYTES = 64 * 1024 * 1024
VMEM_LIMIT = V7X_VMEM_BYTES - 8 * 1024 * 1024


def _rms_rows(x, g):
    ms = jnp.mean(x * x, axis=-1, keepdims=True)
    return x * lax.rsqrt(ms + EPS) * g


def _rms_cols(xt, n, scale):
    ms = jnp.sum(xt * xt, axis=0, keepdims=True) * (1.0 / n)
    return scale * lax.rsqrt(scale * scale * ms + EPS)


def _rope_cols(xt, cos, sin):
    half = xt.shape[0] // 2
    x1, x2 = xt[:half], xt[half:]
    return x1 * cos - x2 * sin, x2 * cos + x1 * sin


def _proj_kernel(x_ref, g_attn_ref, w_in_ref, g_qa_ref, w_q_ref, g_kva_ref, w_k_ref, w_v_ref,
                 g_q_ref, g_k_ref, g_dq_ref, g_dk_ref,
                 cos_a_ref, sin_a_ref, cos_b_ref, sin_b_ref,
                 w_gate_ref, w_up_ref, w_down_ref, w_out_ref,
                 qt_ref, k_ref, vt_ref, dqt_ref, dk_ref, dvt_ref,
                 w_gate_bf_ref, w_up_bf_ref, w_down_bf_ref, w_out_bf_ref):
    tm = x_ref.shape[1]
    bf = jnp.bfloat16
    for src_ref, dst_ref in ((w_gate_ref, w_gate_bf_ref), (w_up_ref, w_up_bf_ref),
                             (w_down_ref, w_down_bf_ref), (w_out_ref, w_out_bf_ref)):
        dst_ref[...] = src_ref[...].astype(bf)
    x = x_ref[0]
    r_tok = lax.rsqrt(jnp.mean(x * x, axis=-1, keepdims=True) + EPS)
    r_tok = jnp.transpose(jnp.broadcast_to(r_tok, (tm, LANES)))[0:1, :]
    h = (x * g_attn_ref[...]).astype(bf)

    def in_proj(r0, r1):
        return lax.dot_general(w_in_ref[r0:r1, :], h, (((1,), (1,)), ((), ())),
                               preferred_element_type=jnp.float32)

    cos_a, sin_a = cos_a_ref[...], sin_a_ref[...]
    cos_b, sin_b = cos_b_ref[...], sin_b_ref[...]
    zeros_pad = jnp.zeros((LANES - MLA_QK, tm), jnp.float32)
    one = jnp.ones((1, tm), jnp.float32)

    def head_a(nope, pe_roped, g):
        nope = nope * _rms_cols(nope, MLA_NOPE, one) * g[:MLA_NOPE]
        return jnp.concatenate([nope, *pe_roped, zeros_pad], axis=0)

    def rope_a(pe, g, scale):
        pe = pe * _rms_cols(pe, MLA_ROPE, scale) * g[MLA_NOPE:]
        return _rope_cols(pe, cos_a, sin_a)

    def head_b(xt, g):
        xt = xt * _rms_cols(xt, DIFF_D, r_tok) * g
        return jnp.concatenate(_rope_cols(xt, cos_b, sin_b), axis=0)

    cq = in_proj(R_Q, R_KV)
    lat = in_proj(R_KV, R_DQ)
    dq = in_proj(R_DQ, R_DK)

    cq = (cq * _rms_cols(cq, Q_RANK, r_tok) * g_qa_ref[...]).astype(bf)
    q = jnp.dot(w_q_ref[...], cq, preferred_element_type=jnp.float32)
    ckv = lat[:KV_RANK]
    ckv = (ckv * _rms_cols(ckv, KV_RANK, r_tok) * g_kva_ref[...]).astype(bf)
    kn = jnp.dot(w_k_ref[...], ckv, preferred_element_type=jnp.float32)
    dk = in_proj(R_DK, R_DV)
    v = jnp.dot(w_v_ref[...], ckv, preferred_element_type=jnp.float32).astype(bf)
    dv = (in_proj(R_DV, IN_COLS) * r_tok).astype(bf)
    tk = vt_ref.shape[3]
    for t in range(tm // tk):
        vt_ref[0, t] = v[:, t * tk:(t + 1) * tk]
        dvt_ref[0, t] = dv[:, t * tk:(t + 1) * tk]

    g_q = g_q_ref[...] * (MLA_QK ** -0.5 * LOG2E)
    for hd in range(MLA_HEADS):
        r0 = hd * MLA_QK
        qt_ref[0, hd * LANES:(hd + 1) * LANES, :] = head_a(
            q[r0:r0 + MLA_NOPE], rope_a(q[r0 + MLA_NOPE:r0 + MLA_QK], g_q, one), g_q).astype(bf)

    g_k = g_k_ref[...]
    kpe = rope_a(lat[KV_RANK:], g_k, r_tok)
    for hd in range(MLA_HEADS):
        kt = head_a(kn[hd * MLA_NOPE:(hd + 1) * MLA_NOPE], kpe, g_k)
        k_ref[0, :, hd * LANES:(hd + 1) * LANES] = kt.T.astype(bf)

    g_dq = g_dq_ref[...] * (DIFF_D ** -0.5 * LOG2E)
    g_dk = g_dk_ref[...]
    zeros_half = jnp.zeros((DIFF_D, tm), bf)
    for hd in range(DIFF_HEADS):
        r0 = hd * 2 * DIFF_D
        q1 = head_b(dq[r0:r0 + DIFF_D], g_dq).astype(bf)
        q2 = head_b(dq[r0 + DIFF_D:r0 + 2 * DIFF_D], g_dq).astype(bf)
        b0 = 2 * hd * LANES
        dqt_ref[0, b0:b0 + LANES, :] = jnp.concatenate([q1, zeros_half], axis=0)
        dqt_ref[0, b0 + LANES:b0 + 2 * LANES, :] = jnp.concatenate([zeros_half, q2], axis=0)
        dkt = jnp.concatenate([head_b(dk[r0:r0 + DIFF_D], g_dk),
                               head_b(dk[r0 + DIFF_D:r0 + 2 * DIFF_D], g_dk)], axis=0)
        dk_ref[0, :, hd * LANES:(hd + 1) * LANES] = dkt.T.astype(bf)


def _causal_mask(s_t, tk, tq):
    key = lax.broadcasted_iota(jnp.int32, (tk, tq), 0)
    qry = lax.broadcasted_iota(jnp.int32, (tk, tq), 1)
    return jnp.where(key <= qry, s_t, -jnp.inf)


class _Chain(NamedTuple):
    k_ref: Any
    k_lanes: slice
    qt_ref: Any
    q_rows: slice
    vt_ref: Any
    v_rows: slice
    acc_rows: slice


def _pipelined_sweep(chains, qi, tq, tk, p_ref, al_ref, m_ref, acc_ref, side_first, side_last):
    assert tq == 2 * tk
    lower, upper = slice(0, tk), slice(tk, tq)

    def scores(ch, j, cols):
        return jnp.dot(ch.k_ref[0, j, :, ch.k_lanes], ch.qt_ref[0, ch.q_rows, cols],
                       preferred_element_type=jnp.float32)

    def stage_b(s_all, slot, cols, first=False):
        for c, s_t in enumerate(s_all):
            if first:
                m_new = jnp.max(s_t, axis=0, keepdims=True)
            else:
                m_old = m_ref[c, :, cols]
                m_new = jnp.maximum(m_old, jnp.max(s_t, axis=0, keepdims=True))
                al_ref[slot, c, :, cols] = jnp.exp2(m_old - m_new)
            m_ref[c, :, cols] = m_new
            p_ref[slot, c, :, cols] = jnp.exp2(s_t - m_new).astype(p_ref.dtype)

    ones = jnp.ones((SUM_ROWS, tk), p_ref.dtype)

    def pv_update(c, ch, j, slot, cols):
        vt_ones = jnp.concatenate([ch.vt_ref[0, j, ch.v_rows, :], ones], axis=0)
        pv = jnp.dot(vt_ones, p_ref[slot, c, :, cols], preferred_element_type=jnp.float32)
        acc_ref[ch.acc_rows, cols] = al_ref[slot, c, :, cols] * acc_ref[ch.acc_rows, cols] + pv

    def stage_ac(j, cols, prev_tile, prev_slot):
        s_all = []
        for c, ch in enumerate(chains):
            s_all.append(scores(ch, j, cols))
            pv_update(c, ch, prev_tile, prev_slot, full)
        return s_all

    full = slice(0, tq)
    d0, d1 = 2 * qi, 2 * qi + 1
    side_first()
    acc_ref[...] = jnp.zeros_like(acc_ref)
    al_ref[0] = jnp.ones(al_ref.shape[1:], al_ref.dtype)
    s_all = [scores(ch, d0, full) for ch in chains]
    stage_b([jnp.concatenate([_causal_mask(s_t[:, lower], tk, tk), s_t[:, upper]], axis=1)
             for s_t in s_all], 0, full, first=True)

    def two_steps(k, carry):
        stage_b(stage_ac(2 * k, full, jnp.where(k == 0, d0, 2 * k - 1), 0), 1, full)
        stage_b(stage_ac(2 * k + 1, full, 2 * k, 1), 0, full)
        return carry

    lax.fori_loop(0, qi, two_steps, 0)
    s_all = stage_ac(d1, upper, jnp.where(qi == 0, d0, 2 * qi - 1), 0)
    stage_b([_causal_mask(s_t, tk, tk) for s_t in s_all], 1, upper)
    for c, ch in enumerate(chains):
        pv_update(c, ch, d1, 1, upper)
    side_last()


def _attn_kernel(lq1_ref, lk1_ref, lq2_ref, lk2_ref, g_sub_ref,
                 qt_ref, k_ref, vt_ref, dqt_ref, dk_ref, dvt_ref, x_ref, w_out_ref, x1_ref,
                 p_ref, al_ref, m_ref, acc_ref, o_ref, *, tq, tk, lam_init, n_q_tiles, n_tiles):
    t = pl.program_id(0)
    contract0 = (((0,), (0,)), ((), ()))

    @pl.when(t == 0)
    def _():
        o_ref[...] = jnp.zeros_like(o_ref)

    n_a = MLA_HEADS * MLA_V
    wr, rd = t % 2, (t + 1) % 2

    def out_proj_a():
        x1_ref[0] = x_ref[0] + lax.dot_general(o_ref[rd, :n_a, :], w_out_ref[:n_a, :], contract0,
                                               preferred_element_type=jnp.float32)

    def out_proj_b():
        x1_ref[0] += lax.dot_general(o_ref[rd, n_a:, :], w_out_ref[n_a:, :], contract0,
                                     preferred_element_type=jnp.float32)

    rows_a, rows_b = MLA_V + SUM_ROWS, DIFF_V + SUM_ROWS
    base_b = MLA_HEADS * rows_a
    lane_group = lambda i: slice(i * LANES, (i + 1) * LANES)
    chains = [_Chain(k_ref, lane_group(hd), qt_ref, lane_group(hd),
                     vt_ref, slice(hd * MLA_V, (hd + 1) * MLA_V),
                     slice(hd * rows_a, (hd + 1) * rows_a)) for hd in range(MLA_HEADS)]
    chains += [_Chain(dk_ref, lane_group(c // 2), dqt_ref, lane_group(c),
                      dvt_ref, slice((c // 2) * DIFF_V, (c // 2 + 1) * DIFF_V),
                      slice(base_b + c * rows_b, base_b + (c + 1) * rows_b))
               for c in range(2 * DIFF_HEADS)]

    def normalised(ch, dv):
        a0 = ch.acc_rows.start
        return acc_ref[a0:a0 + dv, :] / acc_ref[a0 + dv:a0 + dv + 1, :]

    @pl.when(t < n_tiles)
    def _():
        _pipelined_sweep(chains, t % n_q_tiles, tq, tk, p_ref, al_ref, m_ref, acc_ref,
                         out_proj_a, out_proj_b)

        for hd in range(MLA_HEADS):
            o_ref[wr, hd * MLA_V:(hd + 1) * MLA_V, :] = normalised(chains[hd], MLA_V).astype(o_ref.dtype)

        lam = (jnp.exp(jnp.sum(lq1_ref[...] * lk1_ref[...], axis=-1, keepdims=True))
               - jnp.exp(jnp.sum(lq2_ref[...] * lk2_ref[...], axis=-1, keepdims=True))
               + lam_init)
        for hd in range(DIFF_HEADS):
            c = MLA_HEADS + 2 * hd
            o = normalised(chains[c], DIFF_V) - lam * normalised(chains[c + 1], DIFF_V)
            ms = jnp.mean(o * o, axis=0, keepdims=True)
            o = o * lax.rsqrt(ms + EPS) * g_sub_ref[...] * (1.0 - lam_init)
            o_ref[wr, n_a + hd * DIFF_V:n_a + (hd + 1) * DIFF_V, :] = o.astype(o_ref.dtype)

    @pl.when(t == n_tiles)
    def _():
        out_proj_a()
        out_proj_b()


def _ffn_kernel(x1_ref, g_ffn_ref, w_gate_ref, w_up_ref,
                conv_w_ref, conv_b_ref, w_down_ref, out_ref, y_ref, prev_ref):
    assert CONV_WIDTH == 3
    si = pl.program_id(1)
    tm = x1_ref.shape[1]
    x1 = x1_ref[0]
    h = _rms_rows(x1, g_ffn_ref[...]).astype(jnp.bfloat16)

    @pl.when(si == 0)
    def _():
        prev_ref[...] = jnp.zeros_like(prev_ref)

    for c0 in range(0, D_FF, FF_CHUNK):
        cw = min(FF_CHUNK, D_FF - c0)
        g = jnp.dot(h, w_gate_ref[:, c0:c0 + cw], preferred_element_type=jnp.float32)
        u = jnp.dot(h, w_up_ref[:, c0:c0 + cw], preferred_element_type=jnp.float32)
        row = lax.broadcasted_iota(jnp.int32, (tm, cw), 0)
        p1 = prev_ref[SUBLANES - 1:SUBLANES, c0:c0 + cw]
        p2 = prev_ref[SUBLANES - 2:SUBLANES - 1, c0:c0 + cw]
        g1 = jnp.where(row == 0, p1, pltpu.roll(g, 1, axis=0))
        g2 = jnp.where(row == 0, p2, jnp.where(row == 1, p1, pltpu.roll(g, 2, axis=0)))
        prev_ref[:, c0:c0 + cw] = g[tm - SUBLANES:tm, :]
        cg = (conv_b_ref[:, c0:c0 + cw] + g2 * conv_w_ref[0:1, c0:c0 + cw]
              + g1 * conv_w_ref[1:2, c0:c0 + cw] + g * conv_w_ref[2:3, c0:c0 + cw])
        y_ref[:, c0:c0 + cw] = (jax.nn.silu(cg) * u).astype(y_ref.dtype)

    out_ref[0] = x1 + jnp.dot(y_ref[...], w_down_ref[...], preferred_element_type=jnp.float32)


def _rope_tables(seq):
    pos = jnp.arange(seq, dtype=jnp.float32)[:, None]

    def tables(dim):
        inv = 1.0 / (ROPE_THETA ** (jnp.arange(0, dim, 2, dtype=jnp.float32) / dim))
        ang = pos * inv[None, :]
        return jnp.cos(ang), jnp.sin(ang)

    ca, sa = tables(MLA_ROPE)
    cb, sb = tables(DIFF_D)
    return ca.T, sa.T, cb.T, sb.T


def _const_spec(shape):
    return pl.BlockSpec(shape, lambda *_: (0,) * len(shape))


def kernel(x, attn_norm_g, w_in, q_a_norm_g, w_q_up, kv_a_norm_g, w_kv_up, mla_q_norm_g,
           mla_k_norm_g, diff_q_norm_g, diff_k_norm_g, lambda_q1, lambda_k1, lambda_q2, lambda_k2,
           diff_subln_g, w_out, ffn_norm_g, w_gate, w_up, conv_w, conv_b, w_down):
    B, S, _ = x.shape
    depth = w_in.shape[0]
    bf = jnp.bfloat16
    cos_a, sin_a, cos_b, sin_b = _rope_tables(S)

    tk, tq = KEY_TILE, 2 * KEY_TILE
    tm_proj, tm_ffn = PROJ_TOKENS, FFN_TOKENS
    assert tm_proj % tk == 0 and S % tq == 0 and S % tm_proj == 0 and S % tm_ffn == 0
    nk = S // tk

    for l in range(depth):
        w_in_t = w_in[l].T.astype(bf)
        w_q_t = w_q_up[l].T.astype(bf)
        wkv_t = w_kv_up[l].T.reshape(MLA_HEADS, MLA_NOPE + MLA_V, KV_RANK)
        w_k_t = wkv_t[:, :MLA_NOPE].reshape(MLA_HEADS * MLA_NOPE, KV_RANK).astype(bf)
        w_v_t = wkv_t[:, MLA_NOPE:].reshape(MLA_HEADS * MLA_V, KV_RANK).astype(bf)

        n_tok_tiles = S // tm_proj
        tok3 = lambda b, s: (b, s, 0)
        feat3 = lambda b, s: (b, 0, s)
        blk4 = lambda b, s: (b, s, 0, 0)
        rope_a_spec = pl.BlockSpec((MLA_ROPE // 2, tm_proj), lambda b, s: (0, s))
        rope_b_spec = pl.BlockSpec((DIFF_D // 2, tm_proj), lambda b, s: (0, s))
        cast_weights = (w_gate[l], w_up[l], w_down[l], w_out[l])
        n_steps = B * n_tok_tiles
        assert all(w.shape[0] % (n_steps * 2 * SUBLANES) == 0 for w in cast_weights)
        cast_specs = [pl.BlockSpec((w.shape[0] // n_steps, w.shape[1]),
                                   lambda b, s: (b * n_tok_tiles + s, 0)) for w in cast_weights]
        qt, k_a, vt, dqt, dk, dvt, w_gate_bf, w_up_bf, w_down_bf, w_out_bf = pl.pallas_call(
            _proj_kernel,
            grid=(B, n_tok_tiles),
            in_specs=[
                pl.BlockSpec((1, tm_proj, D_MODEL), tok3),
                _const_spec((1, D_MODEL)),
                _const_spec((IN_COLS, D_MODEL)),
                _const_spec((Q_RANK, 1)),
                _const_spec((MLA_HEADS * MLA_QK, Q_RANK)),
                _const_spec((KV_RANK, 1)),
                _const_spec((MLA_HEADS * MLA_NOPE, KV_RANK)),
                _const_spec((MLA_HEADS * MLA_V, KV_RANK)),
                _const_spec((MLA_QK, 1)), _const_spec((MLA_QK, 1)),
                _const_spec((DIFF_D, 1)), _const_spec((DIFF_D, 1)),
                rope_a_spec, rope_a_spec, rope_b_spec, rope_b_spec,
                *cast_specs,
            ],
            out_specs=[
                pl.BlockSpec((1, MLA_HEADS * LANES, tm_proj), feat3),
                pl.BlockSpec((1, tm_proj, MLA_HEADS * LANES), tok3),
                pl.BlockSpec((1, tm_proj // tk, MLA_HEADS * MLA_V, tk), blk4),
                pl.BlockSpec((1, DIFF_HEADS * 2 * LANES, tm_proj), feat3),
                pl.BlockSpec((1, tm_proj, DIFF_HEADS * LANES), tok3),
                pl.BlockSpec((1, tm_proj // tk, DIFF_HEADS * DIFF_V, tk), blk4),
                *cast_specs,
            ],
            out_shape=[
                jax.ShapeDtypeStruct((B, MLA_HEADS * LANES, S), bf),
                jax.ShapeDtypeStruct((B, S, MLA_HEADS * LANES), bf),
                jax.ShapeDtypeStruct((B, nk, MLA_HEADS * MLA_V, tk), bf),
                jax.ShapeDtypeStruct((B, DIFF_HEADS * 2 * LANES, S), bf),
                jax.ShapeDtypeStruct((B, S, DIFF_HEADS * LANES), bf),
                jax.ShapeDtypeStruct((B, nk, DIFF_HEADS * DIFF_V, tk), bf),
                *[jax.ShapeDtypeStruct(w.shape, bf) for w in cast_weights],
            ],
            compiler_params=pltpu.CompilerParams(
                dimension_semantics=("arbitrary", "arbitrary"), vmem_limit_bytes=VMEM_LIMIT),
            name="proj",
        )(x, attn_norm_g[l].reshape(1, -1), w_in_t, q_a_norm_g[l].reshape(-1, 1), w_q_t,
          kv_a_norm_g[l].reshape(-1, 1), w_k_t, w_v_t,
          mla_q_norm_g[l].reshape(-1, 1), mla_k_norm_g[l].reshape(-1, 1),
          diff_q_norm_g[l].reshape(-1, 1), diff_k_norm_g[l].reshape(-1, 1),
          cos_a, sin_a, cos_b, sin_b, *cast_weights)

        nq = S // tq
        n_tiles = B * nq
        n_chains = MLA_HEADS + 2 * DIFF_HEADS
        lam_init = 0.8 - 0.6 * math.exp(-0.3 * l)
        lam_spec = _const_spec((1, DIFF_D))
        cur = lambda t: jnp.minimum(t, n_tiles - 1)
        lag = lambda t: jnp.maximum(t - 1, 0)
        q_tile = lambda t: (cur(t) // nq, 0, cur(t) % nq)
        per_batch = lambda t: (cur(t) // nq, 0, 0, 0)
        lag_tile = lambda t: (lag(t) // nq, lag(t) % nq, 0)
        x1 = pl.pallas_call(
            functools.partial(_attn_kernel, tq=tq, tk=tk, lam_init=lam_init,
                              n_q_tiles=nq, n_tiles=n_tiles),
            grid=(n_tiles + 1,),
            in_specs=[
                lam_spec, lam_spec, lam_spec, lam_spec,
                _const_spec((DIFF_V, 1)),
                pl.BlockSpec((1, MLA_HEADS * LANES, tq), q_tile),
                pl.BlockSpec((1, nk, tk, MLA_HEADS * LANES), per_batch),
                pl.BlockSpec((1, nk, MLA_HEADS * MLA_V, tk), per_batch),
                pl.BlockSpec((1, DIFF_HEADS * 2 * LANES, tq), q_tile),
                pl.BlockSpec((1, nk, tk, DIFF_HEADS * LANES), per_batch),
                pl.BlockSpec((1, nk, DIFF_HEADS * DIFF_V, tk), per_batch),
                pl.BlockSpec((1, tq, D_MODEL), lag_tile),
                _const_spec((D_MODEL, D_MODEL)),
            ],
            out_specs=pl.BlockSpec((1, tq, D_MODEL), lag_tile),
            out_shape=jax.ShapeDtypeStruct((B, S, D_MODEL), x.dtype),
            scratch_shapes=[
                pltpu.VMEM((2, n_chains, tk, tq), bf),
                pltpu.VMEM((2, n_chains, 1, tq), jnp.float32),
                pltpu.VMEM((n_chains, 1, tq), jnp.float32),
                pltpu.VMEM((MLA_HEADS * (MLA_V + SUM_ROWS)
                            + 2 * DIFF_HEADS * (DIFF_V + SUM_ROWS), tq), jnp.float32),
                pltpu.VMEM((2, D_MODEL, tq), bf)],
            compiler_params=pltpu.CompilerParams(
                dimension_semantics=("arbitrary",), vmem_limit_bytes=VMEM_LIMIT),
            name="attn",
        )(lambda_q1[l].reshape(1, -1), lambda_k1[l].reshape(1, -1),
          lambda_q2[l].reshape(1, -1), lambda_k2[l].reshape(1, -1),
          diff_subln_g[l].reshape(-1, 1),
          qt, k_a.reshape(B, nk, tk, MLA_HEADS * LANES), vt,
          dqt, dk.reshape(B, nk, tk, DIFF_HEADS * LANES), dvt, x, w_out_bf)

        x = pl.pallas_call(
            _ffn_kernel,
            grid=(B, S // tm_ffn),
            in_specs=[
                pl.BlockSpec((1, tm_ffn, D_MODEL), lambda b, s: (b, s, 0)),
                _const_spec((1, D_MODEL)),
                _const_spec((D_MODEL, D_FF)),
                _const_spec((D_MODEL, D_FF)),
                _const_spec((CONV_WIDTH, D_FF)),
                _const_spec((1, D_FF)),
                _const_spec((D_FF, D_MODEL)),
            ],
            out_specs=pl.BlockSpec((1, tm_ffn, D_MODEL), lambda b, s: (b, s, 0)),
            out_shape=jax.ShapeDtypeStruct((B, S, D_MODEL), x.dtype),
            scratch_shapes=[pltpu.VMEM((tm_ffn, D_FF), bf),
                            pltpu.VMEM((SUBLANES, D_FF), jnp.float32)],
            compiler_params=pltpu.CompilerParams(
                dimension_semantics=("arbitrary", "arbitrary"), vmem_limit_bytes=VMEM_LIMIT),
            name="ffn",
        )(x1, ffn_norm_g[l].reshape(1, -1), w_gate_bf, w_up_bf,
          conv_w[l], conv_b[l].reshape(1, -1), w_down_bf)
    return x
```

```python
import functools
import math
from typing import Any, NamedTuple

import jax
import jax.numpy as jnp
from jax import lax
from jax.experimental import pallas as pl
from jax.experimental.pallas import tpu as pltpu

D_MODEL = 1024
MLA_HEADS = 8
MLA_NOPE = 64
MLA_ROPE = 32
MLA_V = 64
MLA_QK = MLA_NOPE + MLA_ROPE
Q_RANK = 384
KV_RANK = 256
DIFF_HEADS = 4
DIFF_D = 64
DIFF_V = 2 * DIFF_D
D_FF = 2816
CONV_WIDTH = 3
ROPE_THETA = 10000.0
EPS = 1e-6
LANES = 128
SUBLANES = 8
SUM_ROWS = 16
LOG2E = math.log2(math.e)
R_Q = 0
R_KV = R_Q + Q_RANK
R_KPE = R_KV + KV_RANK
R_DQ = R_KPE + MLA_ROPE
R_DK = R_DQ + DIFF_HEADS * 2 * DIFF_D
R_DV = R_DK + DIFF_HEADS * 2 * DIFF_D
IN_COLS = R_DV + DIFF_HEADS * DIFF_V

KEY_TILE = 256
PROJ_TOKENS = 1024
FFN_TOKENS = 1024
FF_CHUNK = 1024
V7X_VMEM_BYTES = 64 * 1024 * 1024
VMEM_LIMIT = V7X_VMEM_BYTES - 8 * 1024 * 1024


def _rms_rows(x, g):
    ms = jnp.mean(x * x, axis=-1, keepdims=True)
    return x * lax.rsqrt(ms + EPS) * g


def _rms_cols(xt, n, scale):
    ms = jnp.sum(xt * xt, axis=0, keepdims=True) * (1.0 / n)
    return scale * lax.rsqrt(scale * scale * ms + EPS)


def _rope_cols(xt, cos, sin):
    half = xt.shape[0] // 2
    x1, x2 = xt[:half], xt[half:]
    return x1 * cos - x2 * sin, x2 * cos + x1 * sin


def _proj_kernel(x_ref, g_attn_ref, w_in_ref, g_qa_ref, w_q_ref, g_kva_ref, w_k_ref, w_v_ref,
                 g_q_ref, g_k_ref, g_dq_ref, g_dk_ref,
                 cos_a_ref, sin_a_ref, cos_b_ref, sin_b_ref,
                 w_gate_ref, w_up_ref, w_down_ref, w_out_ref,
                 qt_ref, k_ref, vt_ref, dqt_ref, dk_ref, dvt_ref,
                 w_gate_bf_ref, w_up_bf_ref, w_down_bf_ref, w_out_bf_ref):
    tm = x_ref.shape[1]
    bf = jnp.bfloat16
    for src_ref, dst_ref in ((w_gate_ref, w_gate_bf_ref), (w_up_ref, w_up_bf_ref),
                             (w_down_ref, w_down_bf_ref), (w_out_ref, w_out_bf_ref)):
        dst_ref[...] = src_ref[...].astype(bf)
    x = x_ref[0]
    r_tok = lax.rsqrt(jnp.mean(x * x, axis=-1, keepdims=True) + EPS)
    r_tok = jnp.transpose(jnp.broadcast_to(r_tok, (tm, LANES)))[0:1, :]
    h = (x * g_attn_ref[...]).astype(bf)

    def in_proj(r0, r1):
        return lax.dot_general(w_in_ref[r0:r1, :], h, (((1,), (1,)), ((), ())),
                               preferred_element_type=jnp.float32)

    cos_a, sin_a = cos_a_ref[...], sin_a_ref[...]
    cos_b, sin_b = cos_b_ref[...], sin_b_ref[...]
    zeros_pad = jnp.zeros((LANES - MLA_QK, tm), jnp.float32)
    one = jnp.ones((1, tm), jnp.float32)

    def head_a(nope, pe_roped, g):
        nope = nope * _rms_cols(nope, MLA_NOPE, one) * g[:MLA_NOPE]
        return jnp.concatenate([nope, *pe_roped, zeros_pad], axis=0)

    def rope_a(pe, g, scale):
        pe = pe * _rms_cols(pe, MLA_ROPE, scale) * g[MLA_NOPE:]
        return _rope_cols(pe, cos_a, sin_a)

    def head_b(xt, g):
        xt = xt * _rms_cols(xt, DIFF_D, r_tok) * g
        return jnp.concatenate(_rope_cols(xt, cos_b, sin_b), axis=0)

    cq = in_proj(R_Q, R_KV)
    lat = in_proj(R_KV, R_DQ)
    dq = in_proj(R_DQ, R_DK)

    cq = (cq * _rms_cols(cq, Q_RANK, r_tok) * g_qa_ref[...]).astype(bf)
    q = jnp.dot(w_q_ref[...], cq, preferred_element_type=jnp.float32)
    ckv = lat[:KV_RANK]
    ckv = (ckv * _rms_cols(ckv, KV_RANK, r_tok) * g_kva_ref[...]).astype(bf)
    kn = jnp.dot(w_k_ref[...], ckv, preferred_element_type=jnp.float32)
    dk = in_proj(R_DK, R_DV)
    v = jnp.dot(w_v_ref[...], ckv, preferred_element_type=jnp.float32).astype(bf)
    dv = (in_proj(R_DV, IN_COLS) * r_tok).astype(bf)
    tk = vt_ref.shape[3]
    for t in range(tm // tk):
        vt_ref[0, t] = v[:, t * tk:(t + 1) * tk]
        dvt_ref[0, t] = dv[:, t * tk:(t + 1) * tk]

    g_q = g_q_ref[...] * (MLA_QK ** -0.5 * LOG2E)
    for hd in range(MLA_HEADS):
        r0 = hd * MLA_QK
        qt_ref[0, hd * LANES:(hd + 1) * LANES, :] = head_a(
            q[r0:r0 + MLA_NOPE], rope_a(q[r0 + MLA_NOPE:r0 + MLA_QK], g_q, one), g_q).astype(bf)

    g_k = g_k_ref[...]
    kpe = rope_a(lat[KV_RANK:], g_k, r_tok)
    for hd in range(MLA_HEADS):
        kt = head_a(kn[hd * MLA_NOPE:(hd + 1) * MLA_NOPE], kpe, g_k)
        k_ref[0, :, hd * LANES:(hd + 1) * LANES] = kt.T.astype(bf)

    g_dq = g_dq_ref[...] * (DIFF_D ** -0.5 * LOG2E)
    g_dk = g_dk_ref[...]
    zeros_half = jnp.zeros((DIFF_D, tm), bf)
    for hd in range(DIFF_HEADS):
        r0 = hd * 2 * DIFF_D
        q1 = head_b(dq[r0:r0 + DIFF_D], g_dq).astype(bf)
        q2 = head_b(dq[r0 + DIFF_D:r0 + 2 * DIFF_D], g_dq).astype(bf)
        b0 = 2 * hd * LANES
        dqt_ref[0, b0:b0 + LANES, :] = jnp.concatenate([q1, zeros_half], axis=0)
        dqt_ref[0, b0 + LANES:b0 + 2 * LANES, :] = jnp.concatenate([zeros_half, q2], axis=0)
        dkt = jnp.concatenate([head_b(dk[r0:r0 + DIFF_D], g_dk),
                               head_b(dk[r0 + DIFF_D:r0 + 2 * DIFF_D], g_dk)], axis=0)
        dk_ref[0, :, hd * LANES:(hd + 1) * LANES] = dkt.T.astype(bf)


def _causal_mask(s_t, tk, tq):
    key = lax.broadcasted_iota(jnp.int32, (tk, tq), 0)
    qry = lax.broadcasted_iota(jnp.int32, (tk, tq), 1)
    return jnp.where(key <= qry, s_t, -jnp.inf)


class _Chain(NamedTuple):
    k_ref: Any
    k_lanes: slice
    qt_ref: Any
    q_rows: slice
    vt_ref: Any
    v_rows: slice
    acc_rows: slice


def _pipelined_sweep(chains, qi, tq, tk, p_ref, al_ref, m_ref, acc_ref):
    assert tq == 2 * tk
    lower, upper = slice(0, tk), slice(tk, tq)

    def scores(ch, j, cols):
        return jnp.dot(ch.k_ref[0, j, :, ch.k_lanes], ch.qt_ref[0, ch.q_rows, cols],
                       preferred_element_type=jnp.float32)

    def stage_b(s_all, slot, cols, first=False):
        for c, s_t in enumerate(s_all):
            if first:
                m_new = jnp.max(s_t, axis=0, keepdims=True)
            else:
                m_old = m_ref[c, :, cols]
                m_new = jnp.maximum(m_old, jnp.max(s_t, axis=0, keepdims=True))
                al_ref[slot, c, :, cols] = jnp.exp2(m_old - m_new)
            m_ref[c, :, cols] = m_new
            p_ref[slot, c, :, cols] = jnp.exp2(s_t - m_new).astype(p_ref.dtype)

    ones = jnp.ones((SUM_ROWS, tk), p_ref.dtype)

    def pv_update(c, ch, j, slot, cols):
        vt_ones = jnp.concatenate([ch.vt_ref[0, j, ch.v_rows, :], ones], axis=0)
        pv = jnp.dot(vt_ones, p_ref[slot, c, :, cols], preferred_element_type=jnp.float32)
        acc_ref[ch.acc_rows, cols] = al_ref[slot, c, :, cols] * acc_ref[ch.acc_rows, cols] + pv

    def stage_ac(j, cols, prev_tile, prev_slot):
        s_all = []
        for c, ch in enumerate(chains):
            s_all.append(scores(ch, j, cols))
            pv_update(c, ch, prev_tile, prev_slot, full)
        return s_all

    full = slice(0, tq)
    d0, d1 = 2 * qi, 2 * qi + 1
    acc_ref[...] = jnp.zeros_like(acc_ref)
    al_ref[0] = jnp.ones(al_ref.shape[1:], al_ref.dtype)
    s_all = [scores(ch, d0, full) for ch in chains]
    stage_b([jnp.concatenate([_causal_mask(s_t[:, lower], tk, tk), s_t[:, upper]], axis=1)
             for s_t in s_all], 0, full, first=True)

    def two_steps(k, carry):
        stage_b(stage_ac(2 * k, full, jnp.where(k == 0, d0, 2 * k - 1), 0), 1, full)
        stage_b(stage_ac(2 * k + 1, full, 2 * k, 1), 0, full)
        return carry

    lax.fori_loop(0, qi, two_steps, 0)
    s_all = stage_ac(d1, upper, jnp.where(qi == 0, d0, 2 * qi - 1), 0)
    stage_b([_causal_mask(s_t, tk, tk) for s_t in s_all], 1, upper)
    for c, ch in enumerate(chains):
        pv_update(c, ch, d1, 1, upper)


def _attn_kernel(lq1_ref, lk1_ref, lq2_ref, lk2_ref, g_sub_ref,
                 qt_ref, k_ref, vt_ref, dqt_ref, dk_ref, dvt_ref, oa_ref, ob_ref,
                 p_ref, al_ref, m_ref, acc_ref, *, tq, tk, lam_init):
    rows_a, rows_b = MLA_V + SUM_ROWS, DIFF_V + SUM_ROWS
    base_b = MLA_HEADS * rows_a
    lane_group = lambda i: slice(i * LANES, (i + 1) * LANES)
    chains = [_Chain(k_ref, lane_group(hd), qt_ref, lane_group(hd),
                     vt_ref, slice(hd * MLA_V, (hd + 1) * MLA_V),
                     slice(hd * rows_a, (hd + 1) * rows_a)) for hd in range(MLA_HEADS)]
    chains += [_Chain(dk_ref, lane_group(c // 2), dqt_ref, lane_group(c),
                      dvt_ref, slice((c // 2) * DIFF_V, (c // 2 + 1) * DIFF_V),
                      slice(base_b + c * rows_b, base_b + (c + 1) * rows_b))
               for c in range(2 * DIFF_HEADS)]
    _pipelined_sweep(chains, pl.program_id(1), tq, tk, p_ref, al_ref, m_ref, acc_ref)

    def normalised(ch, dv):
        a0 = ch.acc_rows.start
        return acc_ref[a0:a0 + dv, :] / acc_ref[a0 + dv:a0 + dv + 1, :]

    for hd in range(MLA_HEADS):
        oa_ref[0, hd * MLA_V:(hd + 1) * MLA_V, :] = normalised(chains[hd], MLA_V).astype(oa_ref.dtype)

    lam = (jnp.exp(jnp.sum(lq1_ref[...] * lk1_ref[...], axis=-1, keepdims=True))
           - jnp.exp(jnp.sum(lq2_ref[...] * lk2_ref[...], axis=-1, keepdims=True))
           + lam_init)
    for hd in range(DIFF_HEADS):
        c = MLA_HEADS + 2 * hd
        o = normalised(chains[c], DIFF_V) - lam * normalised(chains[c + 1], DIFF_V)
        ms = jnp.mean(o * o, axis=0, keepdims=True)
        o = o * lax.rsqrt(ms + EPS) * g_sub_ref[...] * (1.0 - lam_init)
        ob_ref[0, hd * DIFF_V:(hd + 1) * DIFF_V, :] = o.astype(ob_ref.dtype)


def _ffn_kernel(x_ref, oa_ref, ob_ref, w_out_ref, g_ffn_ref, w_gate_ref, w_up_ref,
                conv_w_ref, conv_b_ref, w_down_ref, out_ref, y_ref, prev_ref):
    assert CONV_WIDTH == 3
    si = pl.program_id(1)
    tm = x_ref.shape[1]
    contract0 = (((0,), (0,)), ((), ()))
    n_a = oa_ref.shape[1]
    mix = (lax.dot_general(oa_ref[0], w_out_ref[:n_a, :], contract0,
                           preferred_element_type=jnp.float32)
           + lax.dot_general(ob_ref[0], w_out_ref[n_a:, :], contract0,
                             preferred_element_type=jnp.float32))
    x1 = x_ref[0] + mix
    h = _rms_rows(x1, g_ffn_ref[...]).astype(jnp.bfloat16)

    @pl.when(si == 0)
    def _():
        prev_ref[...] = jnp.zeros_like(prev_ref)

    for c0 in range(0, D_FF, FF_CHUNK):
        cw = min(FF_CHUNK, D_FF - c0)
        g = jnp.dot(h, w_gate_ref[:, c0:c0 + cw], preferred_element_type=jnp.float32)
        u = jnp.dot(h, w_up_ref[:, c0:c0 + cw], preferred_element_type=jnp.float32)
        row = lax.broadcasted_iota(jnp.int32, (tm, cw), 0)
        p1 = prev_ref[SUBLANES - 1:SUBLANES, c0:c0 + cw]
        p2 = prev_ref[SUBLANES - 2:SUBLANES - 1, c0:c0 + cw]
        g1 = jnp.where(row == 0, p1, pltpu.roll(g, 1, axis=0))
        g2 = jnp.where(row == 0, p2, jnp.where(row == 1, p1, pltpu.roll(g, 2, axis=0)))
        prev_ref[:, c0:c0 + cw] = g[tm - SUBLANES:tm, :]
        cg = (conv_b_ref[:, c0:c0 + cw] + g2 * conv_w_ref[0:1, c0:c0 + cw]
              + g1 * conv_w_ref[1:2, c0:c0 + cw] + g * conv_w_ref[2:3, c0:c0 + cw])
        y_ref[:, c0:c0 + cw] = (jax.nn.silu(cg) * u).astype(y_ref.dtype)

    out_ref[0] = x1 + jnp.dot(y_ref[...], w_down_ref[...], preferred_element_type=jnp.float32)


def _rope_tables(seq):
    with jax.ensure_compile_time_eval():
        pos = jnp.arange(seq, dtype=jnp.float32)[:, None]

        def tables(dim):
            inv = 1.0 / (ROPE_THETA ** (jnp.arange(0, dim, 2, dtype=jnp.float32) / dim))
            ang = pos * inv[None, :]
            return jnp.cos(ang), jnp.sin(ang)

        ca, sa = tables(MLA_ROPE)
        cb, sb = tables(DIFF_D)
        return ca.T, sa.T, cb.T, sb.T


def _const_spec(shape):
    return pl.BlockSpec(shape, lambda *_: (0,) * len(shape))


def kernel(x, attn_norm_g, w_in, q_a_norm_g, w_q_up, kv_a_norm_g, w_kv_up, mla_q_norm_g,
           mla_k_norm_g, diff_q_norm_g, diff_k_norm_g, lambda_q1, lambda_k1, lambda_q2, lambda_k2,
           diff_subln_g, w_out, ffn_norm_g, w_gate, w_up, conv_w, conv_b, w_down):
    B, S, _ = x.shape
    depth = w_in.shape[0]
    bf = jnp.bfloat16
    cos_a, sin_a, cos_b, sin_b = _rope_tables(S)

    tk, tq = KEY_TILE, 2 * KEY_TILE
    tm_proj, tm_ffn = PROJ_TOKENS, FFN_TOKENS
    assert tm_proj % tk == 0 and S % tq == 0 and S % tm_proj == 0 and S % tm_ffn == 0
    nk = S // tk

    for l in range(depth):
        w_in_t = w_in[l].T.astype(bf)
        w_q_t = w_q_up[l].T.astype(bf)
        wkv_t = w_kv_up[l].T.reshape(MLA_HEADS, MLA_NOPE + MLA_V, KV_RANK)
        w_k_t = wkv_t[:, :MLA_NOPE].reshape(MLA_HEADS * MLA_NOPE, KV_RANK).astype(bf)
        w_v_t = wkv_t[:, MLA_NOPE:].reshape(MLA_HEADS * MLA_V, KV_RANK).astype(bf)

        n_tok_tiles = S // tm_proj
        tok3 = lambda b, s: (b, s, 0)
        feat3 = lambda b, s: (b, 0, s)
        blk4 = lambda b, s: (b, s, 0, 0)
        rope_a_spec = pl.BlockSpec((MLA_ROPE // 2, tm_proj), lambda b, s: (0, s))
        rope_b_spec = pl.BlockSpec((DIFF_D // 2, tm_proj), lambda b, s: (0, s))
        cast_weights = (w_gate[l], w_up[l], w_down[l], w_out[l])
        n_steps = B * n_tok_tiles
        assert all(w.shape[0] % (n_steps * 2 * SUBLANES) == 0 for w in cast_weights)
        cast_specs = [pl.BlockSpec((w.shape[0] // n_steps, w.shape[1]),
                                   lambda b, s: (b * n_tok_tiles + s, 0)) for w in cast_weights]
        qt, k_a, vt, dqt, dk, dvt, w_gate_bf, w_up_bf, w_down_bf, w_out_bf = pl.pallas_call(
            _proj_kernel,
            grid=(B, n_tok_tiles),
            in_specs=[
                pl.BlockSpec((1, tm_proj, D_MODEL), tok3),
                _const_spec((1, D_MODEL)),
                _const_spec((IN_COLS, D_MODEL)),
                _const_spec((Q_RANK, 1)),
                _const_spec((MLA_HEADS * MLA_QK, Q_RANK)),
                _const_spec((KV_RANK, 1)),
                _const_spec((MLA_HEADS * MLA_NOPE, KV_RANK)),
                _const_spec((MLA_HEADS * MLA_V, KV_RANK)),
                _const_spec((MLA_QK, 1)), _const_spec((MLA_QK, 1)),
                _const_spec((DIFF_D, 1)), _const_spec((DIFF_D, 1)),
                rope_a_spec, rope_a_spec, rope_b_spec, rope_b_spec,
                *cast_specs,
            ],
            out_specs=[
                pl.BlockSpec((1, MLA_HEADS * LANES, tm_proj), feat3),
                pl.BlockSpec((1, tm_proj, MLA_HEADS * LANES), tok3),
                pl.BlockSpec((1, tm_proj // tk, MLA_HEADS * MLA_V, tk), blk4),
                pl.BlockSpec((1, DIFF_HEADS * 2 * LANES, tm_proj), feat3),
                pl.BlockSpec((1, tm_proj, DIFF_HEADS * LANES), tok3),
                pl.BlockSpec((1, tm_proj // tk, DIFF_HEADS * DIFF_V, tk), blk4),
                *cast_specs,
            ],
            out_shape=[
                jax.ShapeDtypeStruct((B, MLA_HEADS * LANES, S), bf),
                jax.ShapeDtypeStruct((B, S, MLA_HEADS * LANES), bf),
                jax.ShapeDtypeStruct((B, nk, MLA_HEADS * MLA_V, tk), bf),
                jax.ShapeDtypeStruct((B, DIFF_HEADS * 2 * LANES, S), bf),
                jax.ShapeDtypeStruct((B, S, DIFF_HEADS * LANES), bf),
                jax.ShapeDtypeStruct((B, nk, DIFF_HEADS * DIFF_V, tk), bf),
                *[jax.ShapeDtypeStruct(w.shape, bf) for w in cast_weights],
            ],
            compiler_params=pltpu.CompilerParams(
                dimension_semantics=("arbitrary", "arbitrary"), vmem_limit_bytes=VMEM_LIMIT),
            name="proj",
        )(x, attn_norm_g[l].reshape(1, -1), w_in_t, q_a_norm_g[l].reshape(-1, 1), w_q_t,
          kv_a_norm_g[l].reshape(-1, 1), w_k_t, w_v_t,
          mla_q_norm_g[l].reshape(-1, 1), mla_k_norm_g[l].reshape(-1, 1),
          diff_q_norm_g[l].reshape(-1, 1), diff_k_norm_g[l].reshape(-1, 1),
          cos_a, sin_a, cos_b, sin_b, *cast_weights)

        nq = S // tq
        n_chains = MLA_HEADS + 2 * DIFF_HEADS
        lam_init = 0.8 - 0.6 * math.exp(-0.3 * l)
        lam_spec = _const_spec((1, DIFF_D))
        q_tile = lambda b, i: (b, 0, i)
        per_batch = lambda b, i: (b, 0, 0, 0)
        o_a, o_b = pl.pallas_call(
            functools.partial(_attn_kernel, tq=tq, tk=tk, lam_init=lam_init),
            grid=(B, nq),
            in_specs=[
                lam_spec, lam_spec, lam_spec, lam_spec,
                _const_spec((DIFF_V, 1)),
                pl.BlockSpec((1, MLA_HEADS * LANES, tq), q_tile),
                pl.BlockSpec((1, nk, tk, MLA_HEADS * LANES), per_batch),
                pl.BlockSpec((1, nk, MLA_HEADS * MLA_V, tk), per_batch),
                pl.BlockSpec((1, DIFF_HEADS * 2 * LANES, tq), q_tile),
                pl.BlockSpec((1, nk, tk, DIFF_HEADS * LANES), per_batch),
                pl.BlockSpec((1, nk, DIFF_HEADS * DIFF_V, tk), per_batch),
            ],
            out_specs=[pl.BlockSpec((1, MLA_HEADS * MLA_V, tq), q_tile),
                       pl.BlockSpec((1, DIFF_HEADS * DIFF_V, tq), q_tile)],
            out_shape=[jax.ShapeDtypeStruct((B, MLA_HEADS * MLA_V, S), bf),
                       jax.ShapeDtypeStruct((B, DIFF_HEADS * DIFF_V, S), bf)],
            scratch_shapes=[
                pltpu.VMEM((2, n_chains, tk, tq), bf),
                pltpu.VMEM((2, n_chains, 1, tq), jnp.float32),
                pltpu.VMEM((n_chains, 1, tq), jnp.float32),
                pltpu.VMEM((MLA_HEADS * (MLA_V + SUM_ROWS)
                            + 2 * DIFF_HEADS * (DIFF_V + SUM_ROWS), tq), jnp.float32)],
            compiler_params=pltpu.CompilerParams(
                dimension_semantics=("arbitrary", "arbitrary"), vmem_limit_bytes=VMEM_LIMIT),
            name="attn",
        )(lambda_q1[l].reshape(1, -1), lambda_k1[l].reshape(1, -1),
          lambda_q2[l].reshape(1, -1), lambda_k2[l].reshape(1, -1),
          diff_subln_g[l].reshape(-1, 1),
          qt, k_a.reshape(B, nk, tk, MLA_HEADS * LANES), vt,
          dqt, dk.reshape(B, nk, tk, DIFF_HEADS * LANES), dvt)

        n_a = MLA_HEADS * MLA_V
        x = pl.pallas_call(
            _ffn_kernel,
            grid=(B, S // tm_ffn),
            in_specs=[
                pl.BlockSpec((1, tm_ffn, D_MODEL), lambda b, s: (b, s, 0)),
                pl.BlockSpec((1, n_a, tm_ffn), lambda b, s: (b, 0, s)),
                pl.BlockSpec((1, D_MODEL - n_a, tm_ffn), lambda b, s: (b, 0, s)),
                _const_spec((D_MODEL, D_MODEL)),
                _const_spec((1, D_MODEL)),
                _const_spec((D_MODEL, D_FF)),
                _const_spec((D_MODEL, D_FF)),
                _const_spec((CONV_WIDTH, D_FF)),
                _const_spec((1, D_FF)),
                _const_spec((D_FF, D_MODEL)),
            ],
            out_specs=pl.BlockSpec((1, tm_ffn, D_MODEL), lambda b, s: (b, s, 0)),
            out_shape=jax.ShapeDtypeStruct((B, S, D_MODEL), x.dtype),
            scratch_shapes=[pltpu.VMEM((tm_ffn, D_FF), bf),
                            pltpu.VMEM((SUBLANES, D_FF), jnp.float32)],
            compiler_params=pltpu.CompilerParams(
                dimension_semantics=("arbitrary", "arbitrary"), vmem_limit_bytes=VMEM_LIMIT),
            name="ffn",
        )(x, o_a, o_b, w_out_bf, ffn_norm_g[l].reshape(1, -1), w_gate_bf, w_up_bf,
          conv_w[l], conv_b[l].reshape(1, -1), w_down_bf)
    return x
```
